```python
import math
import jax
import jax.numpy as jnp
from jax import lax
import numpy as np

D_MODEL = 1024
BATCH = 2
SEQ = 8192
DEPTH = 1

N_MEM = 256
EPS = 1e-6
NEG_INF = -1e30
Q_BLOCK = 128

MLA_HEADS = 8
MLA_NOPE = 64
MLA_ROPE = 32
MLA_V = 64
MLA_Q_RANK = 256
MLA_KV_RANK = 128
ROPE_THETA = 10000.0

NSA_HEADS = 8
NSA_KV_GROUPS = 2
NSA_HD = 64
CMP_LEN = 32
CMP_STRIDE = 16
CMP_HIDDEN = 128
SLC_LEN = 64
SLC_TOPK = 16
WIN = 512
FORCE_SCORE = 1e9

REL_BUCKETS = 32
REL_MAX_EXACT = 16
REL_MAX_DIST = 512

MEM_HEADS = 4
MEM_HD = 64

N_EXPERTS = 32
TOP_K = 4
D_FF = 1024
SWIGLU_LIMIT = 7.0
SWIGLU_ALPHA = 1.702
MOE_BLOCK = 128

D_MIX = MLA_HEADS * MLA_V + NSA_HEADS * NSA_HD
IN_SPLIT_SIZES = (MLA_Q_RANK, MLA_KV_RANK, MLA_ROPE, NSA_HEADS * NSA_HD) + (NSA_KV_GROUPS * NSA_HD,) * 6 + (3 * NSA_HEADS,)
IN_COLS = sum(IN_SPLIT_SIZES)

kernel_name = "hymba_mla_nsa_t5bias_memxattn_gptoss_moe"


def rms_norm(x, g):
    xf = x.astype(jnp.float32)
    y = xf * lax.rsqrt(jnp.mean(xf * xf, axis=-1, keepdims=True) + EPS)
    return (y * g.astype(jnp.float32)).astype(x.dtype)


def masked_softmax(logits, mask):
    logits = jnp.where(mask, logits.astype(jnp.float32), NEG_INF)
    m = jnp.max(logits, axis=-1, keepdims=True)
    e = jnp.exp(logits - m) * mask
    return e / jnp.maximum(jnp.sum(e, axis=-1, keepdims=True), 1e-30)


def rope(x, pos):
    half = x.shape[-1] // 2
    inv = ROPE_THETA ** (-jnp.arange(half, dtype=jnp.float32) / half)
    ang = pos.astype(jnp.float32)[:, None] * inv
    cos, sin = jnp.cos(ang)[:, None, :], jnp.sin(ang)[:, None, :]
    x1 = x[..., :half].astype(jnp.float32)
    x2 = x[..., half:].astype(jnp.float32)
    return jnp.concatenate([x1 * cos - x2 * sin, x2 * cos + x1 * sin], axis=-1).astype(x.dtype)


def rel_bucket(dist):
    n = jnp.maximum(dist, 0)
    nf = jnp.maximum(n, 1).astype(jnp.float32)
    large = REL_MAX_EXACT + (jnp.log(nf / REL_MAX_EXACT) / math.log(REL_MAX_DIST / REL_MAX_EXACT)
                             * (REL_BUCKETS - REL_MAX_EXACT)).astype(jnp.int32)
    large = jnp.minimum(large, REL_BUCKETS - 1)
    return jnp.where(n < REL_MAX_EXACT, n, large)


def causal_block_attention(q, k, v, scale):
    B, S, H, _ = q.shape
    nb = S // Q_BLOCK
    qb = q.reshape(B, nb, Q_BLOCK, H, q.shape[-1]).swapaxes(0, 1)
    kpos = jnp.arange(S)

    def one_block(args):
        qi, i = args
        qpos = i * Q_BLOCK + jnp.arange(Q_BLOCK)
        s = jnp.einsum('bqhd,bkhd->bhqk', qi, k).astype(jnp.float32) * scale
        p = masked_softmax(s, kpos[None, :] <= qpos[:, None])
        return jnp.einsum('bhqk,bkhd->bqhd', p.astype(v.dtype), v)

    out = lax.map(one_block, (qb, jnp.arange(nb)))
    return out.swapaxes(0, 1).reshape(B, S, H, v.shape[-1])


def mla_mixer(cq, ckv, krope, g_cq, w_uq, g_ckv, w_ukv, g_q, g_k):
    B, S, _ = cq.shape
    H = MLA_HEADS
    q = (rms_norm(cq, g_cq) @ w_uq).reshape(B, S, H, MLA_NOPE + MLA_ROPE)
    kv = (rms_norm(ckv, g_ckv) @ w_ukv).reshape(B, S, H, MLA_NOPE + MLA_V)
    k_nope, v = kv[..., :MLA_NOPE], kv[..., MLA_NOPE:]
    k = jnp.concatenate([k_nope, jnp.broadcast_to(krope[:, :, None, :], (B, S, H, MLA_ROPE))], axis=-1)
    q = rms_norm(q, g_q)
    k = rms_norm(k, g_k)
    pos = jnp.arange(S)
    q = jnp.concatenate([q[..., :MLA_NOPE], rope(q[..., MLA_NOPE:], pos)], axis=-1)
    k = jnp.concatenate([k[..., :MLA_NOPE], rope(k[..., MLA_NOPE:], pos)], axis=-1)
    o = causal_block_attention(q, k, v, (MLA_NOPE + MLA_ROPE) ** -0.5)
    return o.reshape(B, S, H * MLA_V)


def compress_blocks(t, cmp_idx, pos_emb, w1, w2):
    blk = t[:, cmp_idx] + pos_emb[None, None, :, None, :]
    hid = jax.nn.gelu(jnp.einsum('bnlgd,ldf->bngf', blk, w1))
    return jnp.einsum('bngf,fd->bngd', hid, w2)


def nsa_mixer(q, kc, vc, ks, vs, kw, vw, gate_logits, cmp_k_pos, cmp_k_w1, cmp_k_w2,
              cmp_v_pos, cmp_v_w1, cmp_v_w2, g_q, g_k, rel_table):
    B, S = q.shape[:2]
    G, HPG, D = NSA_KV_GROUPS, NSA_HEADS // NSA_KV_GROUPS, NSA_HD
    n_cmp = (S - CMP_LEN) // CMP_STRIDE + 1
    n_slc = S // SLC_LEN
    n_sel = min(SLC_TOPK, n_slc)
    nb = S // Q_BLOCK
    scale = D ** -0.5

    q = rms_norm(q.reshape(B, S, G, HPG, D), g_q)
    kc, vc, ks, vs, kw, vw = (t.reshape(B, S, G, D) for t in (kc, vc, ks, vs, kw, vw))

    cmp_idx = jnp.arange(n_cmp)[:, None] * CMP_STRIDE + jnp.arange(CMP_LEN)[None, :]
    k_cmp = rms_norm(compress_blocks(kc, cmp_idx, cmp_k_pos, cmp_k_w1, cmp_k_w2), g_k)
    v_cmp = compress_blocks(vc, cmp_idx, cmp_v_pos, cmp_v_w1, cmp_v_w2)
    cmp_start, cmp_end = cmp_idx[:, 0], cmp_idx[:, -1]
    slc_start = jnp.arange(n_slc) * SLC_LEN
    overlap = ((cmp_start[:, None] < slc_start[None, :] + SLC_LEN)
               & (cmp_end[:, None] >= slc_start[None, :])).astype(jnp.float32)

    k_blk = rms_norm(ks, g_k).transpose(0, 2, 1, 3).reshape(B, G, n_slc, SLC_LEN, D)
    v_blk = vs.transpose(0, 2, 1, 3).reshape(B, G, n_slc, SLC_LEN, D)

    pad = ((0, 0), (WIN, 0), (0, 0), (0, 0))
    k_pad = jnp.pad(rms_norm(kw, g_k), pad)
    v_pad = jnp.pad(vw, pad)

    tbl_gh = rel_table.reshape(REL_BUCKETS, G, HPG)
    tbl_g = tbl_gh.transpose(1, 0, 2)
    b_idx = jnp.arange(B)[:, None, None, None]
    g_idx = jnp.arange(G)[None, :, None, None]

    def bias_gh(dist):
        return jnp.moveaxis(tbl_gh[rel_bucket(dist)], (-2, -1), (0, 1)).astype(jnp.float32)

    qb = q.reshape(B, nb, Q_BLOCK, G, HPG, D).swapaxes(0, 1)
    gb = jax.nn.sigmoid(gate_logits.astype(jnp.float32)).reshape(B, nb, Q_BLOCK, G, HPG, 3).swapaxes(0, 1)

    def one_block(args):
        qi, gi, i = args
        q0 = i * Q_BLOCK
        qpos = q0 + jnp.arange(Q_BLOCK)

        s = jnp.einsum('bqghd,bngd->bghqn', qi, k_cmp).astype(jnp.float32) * scale
        s = s + bias_gh(qpos[:, None] - cmp_end[None, :])
        p_cmp = masked_softmax(s, cmp_end[None, :] <= qpos[:, None])
        o_cmp = jnp.einsum('bghqn,bngd->bqghd', p_cmp.astype(v_cmp.dtype), v_cmp)

        imp = jnp.einsum('bghqn,nj->bgqj', p_cmp, overlap)
        j = jnp.arange(n_slc)[None, :]
        cur = (qpos // SLC_LEN)[:, None]
        forced = (j == 0) | (j == cur) | (j == cur - 1)
        imp = jnp.where(forced, FORCE_SCORE, imp)
        imp = jnp.where(slc_start[None, :] <= qpos[:, None], imp, NEG_INF)
        _, sel = lax.top_k(imp, n_sel)
        k_sel = k_blk[b_idx, g_idx, sel].reshape(B, G, Q_BLOCK, n_sel * SLC_LEN, D)
        v_sel = v_blk[b_idx, g_idx, sel].reshape(B, G, Q_BLOCK, n_sel * SLC_LEN, D)
        tok = (sel[..., None] * SLC_LEN + jnp.arange(SLC_LEN)).reshape(B, G, Q_BLOCK, n_sel * SLC_LEN)
        bucket = rel_bucket(qpos[None, None, :, None] - tok)
        bias_sel = jnp.moveaxis(tbl_g[g_idx, bucket], -1, 2).astype(jnp.float32)
        s = jnp.einsum('bqghd,bgqnd->bghqn', qi, k_sel).astype(jnp.float32) * scale + bias_sel
        p = masked_softmax(s, (tok <= qpos[None, None, :, None])[:, :, None])
        o_slc = jnp.einsum('bghqn,bgqnd->bqghd', p.astype(v_sel.dtype), v_sel)

        kwin = lax.dynamic_slice_in_dim(k_pad, q0, WIN + Q_BLOCK, axis=1)
        vwin = lax.dynamic_slice_in_dim(v_pad, q0, WIN + Q_BLOCK, axis=1)
        kpos = q0 - WIN + jnp.arange(WIN + Q_BLOCK)
        dist = qpos[:, None] - kpos[None, :]
        s = jnp.einsum('bqghd,bkgd->bghqk', qi, kwin).astype(jnp.float32) * scale + bias_gh(dist)
        p = masked_softmax(s, (dist >= 0) & (dist < WIN) & (kpos[None, :] >= 0))
        o_win = jnp.einsum('bghqk,bkgd->bqghd', p.astype(vwin.dtype), vwin)

        out = gi[..., 0:1] * o_cmp + gi[..., 1:2] * o_slc + gi[..., 2:3] * o_win
        return out.astype(q.dtype).reshape(B, Q_BLOCK, G * HPG * D)

    out = lax.map(one_block, (qb, gb, jnp.arange(nb)))
    return out.swapaxes(0, 1).reshape(B, S, NSA_HEADS * D)


def memory_cross_attention(h, mem_n, w_mq, w_mkv, g_mq, g_mk, w_mo):
    B, S, _ = h.shape
    M = mem_n.shape[1]
    q = rms_norm((h @ w_mq).reshape(B, S, MEM_HEADS, MEM_HD), g_mq)
    kv = (mem_n @ w_mkv).reshape(B, M, 2, MEM_HEADS, MEM_HD)
    k = rms_norm(kv[:, :, 0], g_mk)
    v = kv[:, :, 1]
    s = jnp.einsum('bqhd,bkhd->bhqk', q, k).astype(jnp.float32) * MEM_HD ** -0.5
    p = jax.nn.softmax(s, axis=-1)
    o = jnp.einsum('bhqk,bkhd->bqhd', p.astype(v.dtype), v).reshape(B, S, MEM_HEADS * MEM_HD)
    return o @ w_mo


def moe_ffn(h, router_w, router_b, w_gate_up, b_gate_up, w_down, b_down):
    B, S, D = h.shape
    N = B * S
    NK = N * TOP_K
    t = h.reshape(N, D)
    logits = (t @ router_w + router_b).astype(jnp.float32)
    top_logit, top_idx = lax.top_k(logits, TOP_K)
    gates = jax.nn.softmax(top_logit, axis=-1)
    flat_e = top_idx.reshape(NK)
    flat_tok = jnp.arange(NK, dtype=jnp.int32) // TOP_K
    order = jnp.argsort(flat_e)
    se, stok, sgate = flat_e[order], flat_tok[order], gates.reshape(NK)[order]
    counts = jnp.bincount(flat_e, length=N_EXPERTS)
    start = jnp.cumsum(counts) - counts
    padded = (counts + MOE_BLOCK - 1) // MOE_BLOCK * MOE_BLOCK
    pad_end = jnp.cumsum(padded)
    pad_start = pad_end - padded
    dest = pad_start[se] + jnp.arange(NK) - start[se]
    P = ((NK + MOE_BLOCK - 1) // MOE_BLOCK + N_EXPERTS) * MOE_BLOCK
    buf_tok = jnp.zeros((P,), jnp.int32).at[dest].set(stok)
    buf_gate = jnp.zeros((P,), jnp.float32).at[dest].set(sgate)
    n_blk = P // MOE_BLOCK
    blk_e = jnp.minimum(jnp.searchsorted(pad_end, jnp.arange(n_blk) * MOE_BLOCK, side='right'), N_EXPERTS - 1)

    def expert_block(args):
        tok, e = args
        gu = t[tok] @ w_gate_up[e] + b_gate_up[e]
        gate = jnp.minimum(gu[:, 0::2], SWIGLU_LIMIT)
        up = jnp.clip(gu[:, 1::2], -SWIGLU_LIMIT, SWIGLU_LIMIT)
        act = (up + 1.0) * gate * jax.nn.sigmoid(SWIGLU_ALPHA * gate)
        return act @ w_down[e] + b_down[e]

    ys = lax.map(expert_block, (buf_tok.reshape(n_blk, MOE_BLOCK), blk_e)).reshape(P, D)
    out = jax.ops.segment_sum(ys * buf_gate[:, None].astype(ys.dtype), buf_tok, num_segments=N)
    return out.reshape(B, S, D)


def setup_inputs(seed: int = 0) -> dict:
    key = jax.random.key(seed)
    keys = iter(jax.random.split(key, 64))

    def nrm(shape, scale):
        return jax.random.normal(next(keys), shape, jnp.float32) * scale

    def gain(*shape):
        return 1.0 + nrm(shape, 0.05)

    L = DEPTH
    return {
        "x": nrm((BATCH, SEQ, D_MODEL), 1.0),
        "mem": nrm((BATCH, N_MEM, D_MODEL), 1.0),
        "g_attn_norm": gain(L, D_MODEL),
        "w_in": nrm((L, D_MODEL, IN_COLS), D_MODEL ** -0.5),
        "g_cq": gain(L, MLA_Q_RANK),
        "w_uq": nrm((L, MLA_Q_RANK, MLA_HEADS * (MLA_NOPE + MLA_ROPE)), MLA_Q_RANK ** -0.5),
        "g_ckv": gain(L, MLA_KV_RANK),
        "w_ukv": nrm((L, MLA_KV_RANK, MLA_HEADS * (MLA_NOPE + MLA_V)), MLA_KV_RANK ** -0.5),
        "g_q_mla": gain(L, MLA_NOPE + MLA_ROPE),
        "g_k_mla": gain(L, MLA_NOPE + MLA_ROPE),
        "cmp_k_pos": nrm((L, CMP_LEN, NSA_HD), 0.1),
        "cmp_k_w1": nrm((L, CMP_LEN, NSA_HD, CMP_HIDDEN), (CMP_LEN * NSA_HD) ** -0.5),
        "cmp_k_w2": nrm((L, CMP_HIDDEN, NSA_HD), CMP_HIDDEN ** -0.5),
        "cmp_v_pos": nrm((L, CMP_LEN, NSA_HD), 0.1),
        "cmp_v_w1": nrm((L, CMP_LEN, NSA_HD, CMP_HIDDEN), (CMP_LEN * NSA_HD) ** -0.5),
        "cmp_v_w2": nrm((L, CMP_HIDDEN, NSA_HD), CMP_HIDDEN ** -0.5),
        "g_q_nsa": gain(L, NSA_HD),
        "g_k_nsa": gain(L, NSA_HD),
        "rel_table": nrm((REL_BUCKETS, NSA_HEADS), 0.5),
        "g_out_mla": gain(L, MLA_HEADS * MLA_V),
        "g_out_nsa": gain(L, NSA_HEADS * NSA_HD),
        "w_out": nrm((L, D_MIX, D_MODEL), D_MIX ** -0.5),
        "g_mem_norm": gain(L, D_MODEL),
        "g_mem_src": gain(L, D_MODEL),
        "w_mq": nrm((L, D_MODEL, MEM_HEADS * MEM_HD), D_MODEL ** -0.5),
        "w_mkv": nrm((L, D_MODEL, 2 * MEM_HEADS * MEM_HD), D_MODEL ** -0.5),
        "g_mq": gain(L, MEM_HD),
        "g_mk": gain(L, MEM_HD),
        "w_mo": nrm((L, MEM_HEADS * MEM_HD, D_MODEL), (MEM_HEADS * MEM_HD) ** -0.5),
        "g_moe_norm": gain(L, D_MODEL),
        "router_w": nrm((L, D_MODEL, N_EXPERTS), D_MODEL ** -0.5),
        "router_b": nrm((L, N_EXPERTS), 0.01),
        "w_gate_up": nrm((L, N_EXPERTS, D_MODEL, 2 * D_FF), D_MODEL ** -0.5),
        "b_gate_up": nrm((L, N_EXPERTS, 2 * D_FF), 0.01),
        "w_down": nrm((L, N_EXPERTS, D_FF, D_MODEL), D_FF ** -0.5),
        "b_down": nrm((L, N_EXPERTS, D_MODEL), 0.01),
    }


def reference(x, mem, g_attn_norm, w_in, g_cq, w_uq, g_ckv, w_ukv, g_q_mla, g_k_mla,
              cmp_k_pos, cmp_k_w1, cmp_k_w2, cmp_v_pos, cmp_v_w1, cmp_v_w2, g_q_nsa, g_k_nsa,
              rel_table, g_out_mla, g_out_nsa, w_out, g_mem_norm, g_mem_src, w_mq, w_mkv,
              g_mq, g_mk, w_mo, g_moe_norm, router_w, router_b, w_gate_up, b_gate_up,
              w_down, b_down):
    split_points = np.cumsum(IN_SPLIT_SIZES)[:-1].tolist()
    for l in range(DEPTH):
        h = rms_norm(x, g_attn_norm[l])
        parts = jnp.split(h @ w_in[l], split_points, axis=-1)
        cq, ckv, krope, q_nsa = parts[0], parts[1], parts[2], parts[3]
        kc, vc, ks, vs, kw, vw = parts[4], parts[5], parts[6], parts[7], parts[8], parts[9]
        gate_logits = parts[10]
        y_mla = mla_mixer(cq, ckv, krope, g_cq[l], w_uq[l], g_ckv[l], w_ukv[l], g_q_mla[l], g_k_mla[l])
        y_nsa = nsa_mixer(q_nsa, kc, vc, ks, vs, kw, vw, gate_logits, cmp_k_pos[l], cmp_k_w1[l],
                          cmp_k_w2[l], cmp_v_pos[l], cmp_v_w1[l], cmp_v_w2[l], g_q_nsa[l], g_k_nsa[l],
                          rel_table)
        mix = jnp.concatenate([rms_norm(y_mla, g_out_mla[l]), rms_norm(y_nsa, g_out_nsa[l])], axis=-1)
        x = x + mix @ w_out[l]
        x = x + memory_cross_attention(rms_norm(x, g_mem_norm[l]), rms_norm(mem, g_mem_src[l]),
                                       w_mq[l], w_mkv[l], g_mq[l], g_mk[l], w_mo[l])
        x = x + moe_ffn(rms_norm(x, g_moe_norm[l]), router_w[l], router_b[l], w_gate_up[l],
                        b_gate_up[l], w_down[l], b_down[l])
    return x
```

```python
import functools
import math

import numpy as np
import jax
import jax.numpy as jnp
from jax import lax
from jax.experimental import pallas as pl
from jax.experimental.pallas import tpu as pltpu

F32 = jnp.float32
BF16 = jnp.bfloat16

EPS = 1e-6
NEG_INF = -1e30
LANES = 128

MLA_HEADS = 8
MLA_NOPE = 64
MLA_ROPE = 32
MLA_V = 64
MLA_Q_RANK = 256
MLA_KV_RANK = 128
ROPE_THETA = 10000.0

NSA_HEADS = 8
NSA_KV_GROUPS = 2
NSA_HPG = NSA_HEADS // NSA_KV_GROUPS
NSA_HD = 64
CMP_LEN = 32
CMP_STRIDE = 16
CMP_HIDDEN = 128
SLC_LEN = 64
SLC_TOPK = 16
WIN = 512
FORCE_SCORE = 1e9
NSA_TQ = 128
NSA_TK = 512

REL_BUCKETS = 32
REL_MAX_EXACT = 16
REL_MAX_DIST = 512

MEM_HEADS = 4
MEM_HD = 64

N_EXPERTS = 32
TOP_K = 4
SWIGLU_LIMIT = 7.0
SWIGLU_ALPHA = 1.702
MOE_BLK = 256

VMEM_LIMIT = 56 * 1024 * 1024


def _cparams(sem, vmem=VMEM_LIMIT):
    return pltpu.CompilerParams(dimension_semantics=sem, vmem_limit_bytes=vmem)


def _rms(x, g):
    return x * lax.rsqrt(jnp.mean(x * x, axis=-1, keepdims=True) + EPS) * g


def _seg_rms64(t, g2):
    lane = lax.broadcasted_iota(jnp.int32, t.shape, 1)
    sq = t * t
    lo = jnp.sum(jnp.where(lane < 64, sq, 0.0), axis=-1, keepdims=True)
    hi = jnp.sum(jnp.where(lane >= 64, sq, 0.0), axis=-1, keepdims=True)
    ms = jnp.where(lane < 64, lo, hi) * (1.0 / 64.0)
    return t * lax.rsqrt(ms + EPS) * g2


def _dot(a, b):
    return jnp.dot(a, b, preferred_element_type=F32)


def _dot_nt(a, b):
    return lax.dot_general(a, b, (((1,), (1,)), ((), ())), preferred_element_type=F32)


IN_COLS_PAD = 1792


def _in_proj_kernel(x_ref, g_ref, w_ref, mla_ref, qn_ref, kv_ref, misc_ref):
    h = _rms(x_ref[...], g_ref[...])
    p = _dot(h.astype(BF16), w_ref[...])
    mla_ref[...] = p[:, 0:384]
    qn_ref[...] = p[:, 384:896]
    for j in range(6):
        kv_ref[j] = p[:, 896 + 128 * j:1024 + 128 * j]
    misc_ref[...] = p[:, 1664:1792]


def _in_proj(x2d, g, w):
    n, d = x2d.shape
    tm = 512
    return pl.pallas_call(
        _in_proj_kernel,
        grid=(n // tm,),
        in_specs=[pl.BlockSpec((tm, d), lambda i: (i, 0)),
                  pl.BlockSpec((1, d), lambda i: (0, 0)),
                  pl.BlockSpec((d, IN_COLS_PAD), lambda i: (0, 0))],
        out_specs=[pl.BlockSpec((tm, 384), lambda i: (i, 0)),
                   pl.BlockSpec((tm, 512), lambda i: (i, 0)),
                   pl.BlockSpec((6, tm, 128), lambda i: (0, i, 0)),
                   pl.BlockSpec((tm, 128), lambda i: (i, 0))],
        out_shape=[jax.ShapeDtypeStruct((n, 384), F32),
                   jax.ShapeDtypeStruct((n, 512), F32),
                   jax.ShapeDtypeStruct((6, n, 128), F32),
                   jax.ShapeDtypeStruct((n, 128), F32)],
        compiler_params=_cparams(("parallel",)),
        name="in_proj",
    )(x2d, g, w)


def _mla_prep_kernel(mla_ref, misc_ref, gcq_ref, wuq_ref, gckv_ref, wuk_ref, wuv_ref, gq_ref, gk_ref,
                     rc_ref, rs1_ref, rs2_ref, q_ref, k_ref, v_ref):
    p = mla_ref[...]
    cqn = _rms(p[:, 0:MLA_Q_RANK], gcq_ref[...]).astype(BF16)
    ckvn = _rms(p[:, MLA_Q_RANK:MLA_Q_RANK + MLA_KV_RANK], gckv_ref[...]).astype(BF16)
    qall = _dot(cqn, wuq_ref[...])
    kall = _dot(ckvn, wuk_ref[...])
    vall = _dot(ckvn, wuv_ref[...])
    misc = misc_ref[...]
    lane = lax.broadcasted_iota(jnp.int32, misc.shape, 1)
    in_rope = (lane >= MLA_NOPE) & (lane < MLA_NOPE + MLA_ROPE)
    krope = jnp.where(in_rope, pltpu.roll(misc, MLA_NOPE, axis=1), 0.0)
    rc, rs1, rs2 = rc_ref[...], rs1_ref[...], rs2_ref[...]
    gq, gk = gq_ref[...], gk_ref[...]
    inv_dk = 1.0 / (MLA_NOPE + MLA_ROPE)
    scale = (MLA_NOPE + MLA_ROPE) ** -0.5

    def norm_rope(t, g):
        t = t * lax.rsqrt(jnp.sum(t * t, axis=-1, keepdims=True) * inv_dk + EPS) * g
        half = MLA_ROPE // 2
        return t * rc + pltpu.roll(t, LANES - half, axis=1) * rs1 + pltpu.roll(t, half, axis=1) * rs2

    for h in range(MLA_HEADS):
        qh = norm_rope(qall[:, h * LANES:(h + 1) * LANES], gq)
        q_ref[h] = (qh * scale).astype(BF16)
        kh = norm_rope(kall[:, h * LANES:(h + 1) * LANES] + krope, gk)
        k_ref[h] = kh.astype(BF16)
        v_ref[h] = vall[:, h * MLA_V:(h + 1) * MLA_V].astype(BF16)


def _mla_prep(mla, misc, gcq, wuq, gckv, wuk, wuv, gq, gk, rc, rs1, rs2, seq):
    n = mla.shape[0]
    tm = 256
    ns = seq // tm
    full = lambda a: pl.BlockSpec(a.shape, lambda i: (0,) * a.ndim)
    rope_spec = pl.BlockSpec((tm, LANES), lambda i: (i % ns, 0))
    return pl.pallas_call(
        _mla_prep_kernel,
        grid=(n // tm,),
        in_specs=[pl.BlockSpec((tm, 384), lambda i: (i, 0)),
                  pl.BlockSpec((tm, LANES), lambda i: (i, 0)),
                  full(gcq), full(wuq), full(gckv), full(wuk), full(wuv), full(gq), full(gk),
                  rope_spec, rope_spec, rope_spec],
        out_specs=[pl.BlockSpec((MLA_HEADS, tm, LANES), lambda i: (0, i, 0)),
                   pl.BlockSpec((MLA_HEADS, tm, LANES), lambda i: (0, i, 0)),
                   pl.BlockSpec((MLA_HEADS, tm, MLA_V), lambda i: (0, i, 0))],
        out_shape=[jax.ShapeDtypeStruct((MLA_HEADS, n, LANES), BF16),
                   jax.ShapeDtypeStruct((MLA_HEADS, n, LANES), BF16),
                   jax.ShapeDtypeStruct((MLA_HEADS, n, MLA_V), BF16)],
        compiler_params=_cparams(("parallel",)),
        name="mla_prep",
    )(mla, misc, gcq, wuq, gckv, wuk, wuv, gq, gk, rc, rs1, rs2)


def _mla_attn_kernel(q_ref, k_ref, v_ref, o_ref, m_ref, l_ref, acc_ref, *, tq, tk):
    qi, ki = pl.program_id(1), pl.program_id(2)

    @pl.when(ki == 0)
    def _():
        m_ref[...] = jnp.full(m_ref.shape, NEG_INF, F32)
        l_ref[...] = jnp.zeros(l_ref.shape, F32)
        acc_ref[...] = jnp.zeros(acc_ref.shape, F32)

    def step(masked):
        def head(h, carry):
            s = _dot_nt(q_ref[h], k_ref[h])
            if masked:
                row = lax.broadcasted_iota(jnp.int32, s.shape, 0)
                col = lax.broadcasted_iota(jnp.int32, s.shape, 1)
                s = jnp.where(col <= row, s, NEG_INF)
            m_old = m_ref[h]
            m_new = jnp.maximum(m_old, jnp.max(s, axis=-1, keepdims=True))
            alpha = jnp.exp(m_old - m_new)
            p = jnp.exp(s - m_new)
            l_ref[h] = alpha * l_ref[h] + jnp.sum(p, axis=-1, keepdims=True)
            acc_ref[h] = alpha * acc_ref[h] + _dot(p.astype(BF16), v_ref[h])
            m_ref[h] = m_new
            return carry
        lax.fori_loop(0, MLA_HEADS, head, 0)

    @pl.when(ki < qi)
    def _():
        step(False)

    @pl.when(ki == qi)
    def _():
        step(True)

    @pl.when(ki == pl.num_programs(2) - 1)
    def _():
        for h in range(MLA_HEADS):
            o_ref[:, h * MLA_V:(h + 1) * MLA_V] = acc_ref[h] / l_ref[h]


def _mla_attn(q, k, v, batch, seq):
    n = q.shape[1]
    tq = tk = 512
    nq = seq // tq
    kern = functools.partial(_mla_attn_kernel, tq=tq, tk=tk)
    return pl.pallas_call(
        kern,
        grid=(batch, nq, nq),
        in_specs=[pl.BlockSpec((MLA_HEADS, tq, LANES), lambda b, i, j: (0, b * nq + i, 0)),
                  pl.BlockSpec((MLA_HEADS, tk, LANES), lambda b, i, j: (0, b * nq + jnp.minimum(i, j), 0)),
                  pl.BlockSpec((MLA_HEADS, tk, MLA_V), lambda b, i, j: (0, b * nq + jnp.minimum(i, j), 0))],
        out_specs=pl.BlockSpec((tq, MLA_HEADS * MLA_V), lambda b, i, j: (b * nq + i, 0)),
        out_shape=jax.ShapeDtypeStruct((n, MLA_HEADS * MLA_V), F32),
        scratch_shapes=[pltpu.VMEM((MLA_HEADS, tq, 1), F32),
                        pltpu.VMEM((MLA_HEADS, tq, 1), F32),
                        pltpu.VMEM((MLA_HEADS, tq, MLA_V), F32)],
        compiler_params=_cparams(("parallel", "parallel", "arbitrary")),
        name="mla_attn",
    )(q, k, v)


def _nsa_cmp_kernel(kc_ref, vc_ref, pak_ref, pbk_ref, wak_ref, wbk_ref, w2k_ref,
                    pav_ref, pbv_ref, wav_ref, wbv_ref, w2v_ref, gk_ref, kout_ref, vout_ref):
    def compress(chunks, pa, pb, wa, wb, w2):
        nc = chunks.shape[0]
        ha = _dot((chunks + pa).astype(BF16), wa)
        hb = _dot((chunks + pb).astype(BF16), wb)
        hid = jax.nn.gelu(ha + pltpu.roll(hb, nc - 1, axis=0))
        return [_dot(hid[:, g * CMP_HIDDEN:(g + 1) * CMP_HIDDEN].astype(BF16), w2)
                for g in range(NSA_KV_GROUPS)]

    kc = compress(kc_ref[0], pak_ref[...], pbk_ref[...], wak_ref[...], wbk_ref[...], w2k_ref[...])
    vc = compress(vc_ref[0], pav_ref[...], pbv_ref[...], wav_ref[...], wbv_ref[...], w2v_ref[...])
    for g in range(NSA_KV_GROUPS):
        kout_ref[0, g] = _rms(kc[g], gk_ref[...]).astype(BF16)
        vout_ref[0, g] = vc[g].astype(BF16)


def _nsa_cmp(kc_chunks, vc_chunks, wk, wv, gk):
    b, nc, width = kc_chunks.shape
    full = lambda a: pl.BlockSpec(a.shape, lambda i: (0,) * a.ndim)
    chunk_spec = pl.BlockSpec((1, nc, width), lambda i: (i, 0, 0))
    out_spec = pl.BlockSpec((1, NSA_KV_GROUPS, nc, NSA_HD), lambda i: (i, 0, 0, 0))
    out_sds = jax.ShapeDtypeStruct((b, NSA_KV_GROUPS, nc, NSA_HD), BF16)
    return pl.pallas_call(
        _nsa_cmp_kernel,
        grid=(b,),
        in_specs=[chunk_spec, chunk_spec] + [full(a) for a in wk] + [full(a) for a in wv] + [full(gk)],
        out_specs=[out_spec, out_spec],
        out_shape=[out_sds, out_sds],
        compiler_params=_cparams(("parallel",)),
        name="nsa_cmp",
    )(kc_chunks, vc_chunks, *wk, *wv, gk)


def _nsa_prep_kernel(qn_ref, ks_ref, vs_ref, kw_ref, vw_ref, misc_ref, gq_ref, gk_ref,
                     q_ref, kso_ref, vso_ref, kwo_ref, vwo_ref, gate_ref):
    gq2, gk2 = gq_ref[...], gk_ref[...]
    scale = NSA_HD ** -0.5
    for c in range(NSA_HEADS // 2):
        t = _seg_rms64(qn_ref[:, c * LANES:(c + 1) * LANES], gq2) * scale
        q_ref[2 * c] = t[:, :NSA_HD].astype(BF16)
        q_ref[2 * c + 1] = t[:, NSA_HD:].astype(BF16)
    ksn = _seg_rms64(ks_ref[0], gk2)
    kwn = _seg_rms64(kw_ref[0], gk2)
    vs, vw = vs_ref[0], vw_ref[0]
    for g in range(NSA_KV_GROUPS):
        sl = slice(g * NSA_HD, (g + 1) * NSA_HD)
        kso_ref[g] = ksn[:, sl].astype(BF16)
        kwo_ref[g] = kwn[:, sl].astype(BF16)
        vso_ref[g] = vs[:, sl].astype(BF16)
        vwo_ref[g] = vw[:, sl].astype(BF16)
    sig = jax.nn.sigmoid(misc_ref[...])
    per_group = 3 * NSA_HPG
    for g in range(NSA_KV_GROUPS):
        gate_ref[g] = pltpu.roll(sig, LANES - (MLA_ROPE + per_group * g), axis=1)


def _nsa_prep(qn, kv, misc, gq2, gk2):
    n = qn.shape[0]
    tm = 512
    full = lambda a: pl.BlockSpec(a.shape, lambda i: (0,) * a.ndim)
    kv_spec = lambda j: pl.BlockSpec((1, tm, LANES), lambda i, j=j: (j, i, 0))
    g_spec = pl.BlockSpec((NSA_KV_GROUPS, tm, NSA_HD), lambda i: (0, i, 0))
    g_sds = jax.ShapeDtypeStruct((NSA_KV_GROUPS, n, NSA_HD), BF16)
    return pl.pallas_call(
        _nsa_prep_kernel,
        grid=(n // tm,),
        in_specs=[pl.BlockSpec((tm, 512), lambda i: (i, 0)),
                  kv_spec(2), kv_spec(3), kv_spec(4), kv_spec(5),
                  pl.BlockSpec((tm, LANES), lambda i: (i, 0)), full(gq2), full(gk2)],
        out_specs=[pl.BlockSpec((NSA_HEADS, tm, NSA_HD), lambda i: (0, i, 0)),
                   g_spec, g_spec, g_spec, g_spec,
                   pl.BlockSpec((NSA_KV_GROUPS, tm, LANES), lambda i: (0, i, 0))],
        out_shape=[jax.ShapeDtypeStruct((NSA_HEADS, n, NSA_HD), BF16),
                   g_sds, g_sds, g_sds, g_sds,
                   jax.ShapeDtypeStruct((NSA_KV_GROUPS, n, LANES), F32)],
        compiler_params=_cparams(("parallel",)),
        name="nsa_prep",
    )(qn, kv, kv, kv, kv, misc, gq2, gk2)


def _nsa_attn_kernel(q_ref, kc_ref, vc_ref, ks_ref, vs_ref, kw_ref, vw_ref, gate_ref,
                     ecmp_ref, tn_ref, ovl_ref, efull_ref, o_ref, *, nc, n_sel):
    tq, tk, hpg = NSA_TQ, NSA_TK, NSA_HPG
    i = pl.program_id(2)
    q0 = i * tq
    q4 = q_ref[...].reshape(hpg * tq, NSA_HD)

    def rows_cols(width):
        row = lax.broadcasted_iota(jnp.int32, (tq, width), 0)
        col = lax.broadcasted_iota(jnp.int32, (tq, width), 1)
        return row, col

    def softmax_rows(s3, valid):
        s3 = jnp.where(valid[None], s3, NEG_INF)
        m = jnp.max(s3, axis=-1, keepdims=True)
        e = jnp.exp(s3 - m) * valid[None].astype(F32)
        return e / jnp.maximum(jnp.sum(e, axis=-1, keepdims=True), 1e-30)

    row, col = rows_cols(nc)
    s = _dot_nt(q4, kc_ref[0, 0])
    shift = nc + (CMP_STRIDE // 2) * i
    bias = jnp.stack([pltpu.roll(ecmp_ref[h], shift, axis=1)[:, :nc] for h in range(hpg)])
    valid = (CMP_STRIDE * col + (CMP_LEN - 1)) <= (q0 + row)
    p_cmp = softmax_rows(s.reshape(hpg, tq, nc) + bias, valid)
    o_cmp = _dot(p_cmp.reshape(hpg * tq, nc).astype(BF16), vc_ref[0, 0])

    imp = jnp.dot(jnp.sum(p_cmp, axis=0), ovl_ref[...], preferred_element_type=F32,
                  precision=lax.Precision.HIGHEST)
    row, j = rows_cols(LANES)
    qpos = q0 + row
    cur = qpos // SLC_LEN
    forced = (j == 0) | (j == cur) | (j == cur - 1)
    imp = jnp.where(forced, FORCE_SCORE, imp)
    imp = jnp.where(j * SLC_LEN <= qpos, imp, NEG_INF)
    sel = jnp.zeros((tq, LANES), F32)
    for _ in range(n_sel):
        best = jnp.max(imp, axis=-1, keepdims=True)
        first = jnp.min(jnp.where(imp == best, j, LANES), axis=-1, keepdims=True)
        hit = j == first
        sel = jnp.where(hit, 1.0, sel)
        imp = jnp.where(hit, -jnp.inf, imp)
    sel = sel.astype(BF16)

    row, col = rows_cols(tk)

    def sel_tile(kt, carry):
        m_old, l_old, acc = carry
        k0 = pl.multiple_of(kt * tk, tk)
        s = _dot_nt(q4, ks_ref[0, 0, pl.ds(k0, tk), :])
        start = pl.multiple_of(jnp.maximum(k0 + tk + WIN - q0, 0), LANES)
        bias = tn_ref[:, :, pl.ds(start, tk)]
        in_sel = _dot(sel, efull_ref[:, pl.ds(k0, tk)]) > 0.5
        valid = in_sel & ((k0 + col) <= (q0 + row))
        s3 = jnp.where(valid[None], s.reshape(hpg, tq, tk) + bias, NEG_INF).reshape(hpg * tq, tk)
        m_new = jnp.maximum(m_old, jnp.max(s3, axis=-1, keepdims=True))
        vf = jnp.broadcast_to(valid.astype(F32)[None], (hpg, tq, tk)).reshape(hpg * tq, tk)
        p = jnp.exp(s3 - m_new) * vf
        alpha = jnp.exp(m_old - m_new)
        l_new = alpha * l_old + jnp.sum(p, axis=-1, keepdims=True)
        acc = alpha * acc + _dot(p.astype(BF16), vs_ref[0, 0, pl.ds(k0, tk), :])
        return m_new, l_new, acc

    n_tiles = (q0 + tq + tk - 1) // tk
    init = (jnp.full((hpg * tq, 1), NEG_INF, F32), jnp.zeros((hpg * tq, 1), F32),
            jnp.zeros((hpg * tq, NSA_HD), F32))
    _, l_slc, acc_slc = lax.fori_loop(0, n_tiles, sel_tile, init)
    o_slc = acc_slc / jnp.maximum(l_slc, 1e-30)

    wk = WIN + tq
    row, col = rows_cols(wk)
    s = _dot_nt(q4, kw_ref[0, 0, pl.ds(pl.multiple_of(q0, tq), wk), :])
    dist = row + WIN - col
    valid = (dist >= 0) & (dist < WIN) & (q0 - WIN + col >= 0)
    p_win = softmax_rows(s.reshape(hpg, tq, wk) + tn_ref[:, :, tk:tk + wk], valid)
    o_win = _dot(p_win.reshape(hpg * tq, wk).astype(BF16),
                 vw_ref[0, 0, pl.ds(pl.multiple_of(q0, tq), wk), :])

    gates = gate_ref[0]
    outs = []
    for h in range(hpg):
        rs = slice(h * tq, (h + 1) * tq)
        outs.append(gates[:, 3 * h:3 * h + 1] * o_cmp[rs] + gates[:, 3 * h + 1:3 * h + 2] * o_slc[rs]
                    + gates[:, 3 * h + 2:3 * h + 3] * o_win[rs])
    o_ref[...] = jnp.concatenate(outs, axis=1)


def _nsa_attn(q, kcmp, vcmp, ks, vs, kwp, vwp, gates, ecmp, tn, ovl, efull, batch, seq):
    n = q.shape[1]
    nc = kcmp.shape[2]
    nq = seq // NSA_TQ
    n_sel = min(SLC_TOPK, seq // SLC_LEN)
    g_, hpg = NSA_KV_GROUPS, NSA_HPG
    kern = functools.partial(_nsa_attn_kernel, nc=nc, n_sel=n_sel)
    seq_spec = lambda length: pl.BlockSpec((1, 1, length, NSA_HD), lambda b, g, i: (g, b, 0, 0))
    cmp_spec = pl.BlockSpec((1, 1, nc, NSA_HD), lambda b, g, i: (b, g, 0, 0))
    return pl.pallas_call(
        kern,
        grid=(batch, g_, nq),
        in_specs=[pl.BlockSpec((hpg, NSA_TQ, NSA_HD), lambda b, g, i: (g, b * nq + i, 0)),
                  cmp_spec, cmp_spec,
                  seq_spec(seq), seq_spec(seq), seq_spec(seq + WIN), seq_spec(seq + WIN),
                  pl.BlockSpec((1, NSA_TQ, LANES), lambda b, g, i: (g, b * nq + i, 0)),
                  pl.BlockSpec((hpg, NSA_TQ, 2 * nc), lambda b, g, i: (g, 0, 0)),
                  pl.BlockSpec((hpg, NSA_TQ, 2 * NSA_TK + WIN), lambda b, g, i: (g, 0, 0)),
                  pl.BlockSpec(ovl.shape, lambda b, g, i: (0, 0)),
                  pl.BlockSpec(efull.shape, lambda b, g, i: (0, 0))],
        out_specs=pl.BlockSpec((NSA_TQ, hpg * NSA_HD), lambda b, g, i: (b * nq + i, g)),
        out_shape=jax.ShapeDtypeStruct((n, NSA_HEADS * NSA_HD), F32),
        compiler_params=_cparams(("parallel", "parallel", "arbitrary")),
        name="nsa_attn",
    )(q, kcmp, vcmp, ks, vs, kwp, vwp, gates, ecmp, tn, ovl, efull)


def _mem_kv_kernel(mem_ref, g_ref, w_ref, gk_ref, k_ref, v_ref):
    mn = _rms(mem_ref[0], g_ref[...]).astype(BF16)
    kv = _dot(mn, w_ref[...])
    width = MEM_HEADS * MEM_HD
    for c in range(width // LANES):
        kn = _seg_rms64(kv[:, c * LANES:(c + 1) * LANES], gk_ref[...])
        vv = kv[:, width + c * LANES:width + (c + 1) * LANES]
        for half in range(2):
            sl = slice(half * MEM_HD, (half + 1) * MEM_HD)
            k_ref[0, 2 * c + half] = kn[:, sl].astype(BF16)
            v_ref[0, 2 * c + half] = vv[:, sl].astype(BF16)


def _mem_kv(mem, g, w, gk2):
    b, m, d = mem.shape
    full = lambda a: pl.BlockSpec(a.shape, lambda i: (0,) * a.ndim)
    spec = pl.BlockSpec((1, MEM_HEADS, m, MEM_HD), lambda i: (i, 0, 0, 0))
    sds = jax.ShapeDtypeStruct((b, MEM_HEADS, m, MEM_HD), BF16)
    return pl.pallas_call(
        _mem_kv_kernel,
        grid=(b,),
        in_specs=[pl.BlockSpec((1, m, d), lambda i: (i, 0, 0)), full(g), full(w), full(gk2)],
        out_specs=[spec, spec],
        out_shape=[sds, sds],
        compiler_params=_cparams(("parallel",)),
        name="mem_kv",
    )(mem, g, w, gk2)


def _post_attn_kernel(x_ref, ya_ref, yb_ref, goa_ref, gob_ref, woa_ref, wob_ref, gmn_ref, wmq_ref, gmq_ref,
                      km_ref, vm_ref, wmo_ref, gmoe_ref, rw_ref, rb_ref,
                      x2_ref, h2_ref, topi_ref, gate_ref, cnt_ref):
    first = (pl.program_id(0) == 0) & (pl.program_id(1) == 0)

    @pl.when(first)
    def _():
        cnt_ref[...] = jnp.zeros(cnt_ref.shape, F32)

    mixa = _rms(ya_ref[...], goa_ref[...]).astype(BF16)
    mixb = _rms(yb_ref[...], gob_ref[...]).astype(BF16)
    x1 = x_ref[...] + _dot(mixa, woa_ref[...]) + _dot(mixb, wob_ref[...])

    h = _rms(x1, gmn_ref[...]).astype(BF16)
    q = _dot(h, wmq_ref[...])
    scale = MEM_HD ** -0.5
    outs = []
    for c in range(MEM_HEADS * MEM_HD // LANES):
        qn = _seg_rms64(q[:, c * LANES:(c + 1) * LANES], gmq_ref[...]) * scale
        for half in range(2):
            hd = 2 * c + half
            qh = qn[:, half * MEM_HD:(half + 1) * MEM_HD].astype(BF16)
            s = _dot_nt(qh, km_ref[0, hd])
            e = jnp.exp(s - jnp.max(s, axis=-1, keepdims=True))
            p = e / jnp.sum(e, axis=-1, keepdims=True)
            outs.append(_dot(p.astype(BF16), vm_ref[0, hd]))
    o = jnp.concatenate(outs, axis=1).astype(BF16)
    x2 = x1 + _dot(o, wmo_ref[...])
    x2_ref[...] = x2

    h2 = _rms(x2, gmoe_ref[...])
    h2_ref[...] = h2
    logits = jnp.dot(h2, rw_ref[...], preferred_element_type=F32,
                     precision=lax.Precision.HIGHEST) + rb_ref[...]
    lane = lax.broadcasted_iota(jnp.int32, logits.shape, 1)
    topi = jnp.zeros(logits.shape, jnp.int32)
    topv = jnp.full(logits.shape, NEG_INF, F32)
    onehot = jnp.zeros(logits.shape, F32)
    for k in range(TOP_K):
        best = jnp.max(logits, axis=-1, keepdims=True)
        first_idx = jnp.min(jnp.where(logits == best, lane, LANES), axis=-1, keepdims=True)
        hit = lane == first_idx
        topi = jnp.where(lane == k, first_idx, topi)
        topv = jnp.where(lane == k, best, topv)
        onehot = jnp.where(hit, 1.0, onehot)
        logits = jnp.where(hit, -jnp.inf, logits)
    e = jnp.where(lane < TOP_K, jnp.exp(topv - jnp.max(topv, axis=-1, keepdims=True)), 0.0)
    gate_ref[...] = e / jnp.sum(e, axis=-1, keepdims=True)
    topi_ref[...] = topi
    cnt_ref[...] += jnp.sum(onehot, axis=0, keepdims=True)


def _post_attn(x2d, ya, yb, goa, gob, woa, wob, gmn, wmq, gmq2, km, vm, wmo, gmoe, rw, rb, batch, seq):
    n, d = x2d.shape
    tm = 256
    ns = seq // tm
    full = lambda a: pl.BlockSpec(a.shape, lambda b, i: (0,) * a.ndim)
    tok = lambda width: pl.BlockSpec((tm, width), lambda b, i: (b * ns + i, 0))
    mem_spec = pl.BlockSpec((1,) + km.shape[1:], lambda b, i: (b, 0, 0, 0))
    return pl.pallas_call(
        _post_attn_kernel,
        grid=(batch, ns),
        in_specs=[tok(d), tok(ya.shape[1]), tok(yb.shape[1]), full(goa), full(gob), full(woa), full(wob),
                  full(gmn), full(wmq), full(gmq2), mem_spec, mem_spec, full(wmo), full(gmoe), full(rw), full(rb)],
        out_specs=[tok(d), tok(d), tok(LANES), tok(LANES), pl.BlockSpec((1, LANES), lambda b, i: (0, 0))],
        out_shape=[jax.ShapeDtypeStruct((n, d), F32),
                   jax.ShapeDtypeStruct((n, d), F32),
                   jax.ShapeDtypeStruct((n, LANES), jnp.int32),
                   jax.ShapeDtypeStruct((n, LANES), F32),
                   jax.ShapeDtypeStruct((1, LANES), F32)],
        compiler_params=_cparams(("arbitrary", "arbitrary")),
        name="post_attn",
    )(x2d, ya, yb, goa, gob, woa, wob, gmn, wmq, gmq2, km, vm, wmo, gmoe, rw, rb)


def _moe_pos_kernel(topi_ref, start_ref, tri_ref, pos_ref, carry_ref):
    @pl.when(pl.program_id(0) == 0)
    def _():
        carry_ref[...] = jnp.zeros(carry_ref.shape, F32)

    topi = topi_ref[...]
    lane = lax.broadcasted_iota(jnp.int32, topi.shape, 1)
    hits = [lane == topi[:, k:k + 1] for k in range(TOP_K)]
    onehot = sum(h.astype(F32) for h in hits)
    before = _dot(tri_ref[...], onehot.astype(BF16))
    base = start_ref[...] + carry_ref[...] + before
    pos = jnp.zeros(topi.shape, jnp.int32)
    for k in range(TOP_K):
        pk = jnp.sum(jnp.where(hits[k], base, 0.0), axis=-1, keepdims=True).astype(jnp.int32)
        pos = jnp.where(lane == k, pk, pos)
    pos_ref[...] = pos
    carry_ref[...] += jnp.sum(onehot, axis=0, keepdims=True)


def _moe_pos(topi, pad_start, tri):
    n = topi.shape[0]
    tm = tri.shape[0]
    return pl.pallas_call(
        _moe_pos_kernel,
        grid=(n // tm,),
        in_specs=[pl.BlockSpec((tm, LANES), lambda i: (i, 0)),
                  pl.BlockSpec((1, LANES), lambda i: (0, 0)),
                  pl.BlockSpec((tm, tm), lambda i: (0, 0))],
        out_specs=pl.BlockSpec((tm, LANES), lambda i: (i, 0)),
        out_shape=jax.ShapeDtypeStruct((n, LANES), jnp.int32),
        scratch_shapes=[pltpu.VMEM((1, LANES), F32)],
        compiler_params=_cparams(("arbitrary",)),
        name="moe_pos",
    )(topi, pad_start, tri)


def _moe_scatter_kernel(pos_ref, h_ref, zero_ref, xs_ref, sem, *, tm):
    del zero_ref
    t0 = pl.program_id(0) * tm

    def copy(t, k):
        return pltpu.make_async_copy(h_ref.at[pl.ds(t0 + t, 1)],
                                     xs_ref.at[pl.ds(pos_ref[t * TOP_K + k], 1)], sem)

    def issue(t, c):
        for k in range(TOP_K):
            copy(t, k).start()
        return c

    def drain(t, c):
        for k in range(TOP_K):
            copy(t, k).wait()
        return c

    lax.fori_loop(0, tm, issue, 0)
    lax.fori_loop(0, tm, drain, 0)


def _moe_scatter(pos_flat, h2, xs_zero):
    n, d = h2.shape
    tm = 512
    kern = functools.partial(_moe_scatter_kernel, tm=tm)
    return pl.pallas_call(
        kern,
        grid=(n // tm,),
        in_specs=[pl.BlockSpec((tm * TOP_K,), lambda i: (i,), memory_space=pltpu.SMEM),
                  pl.BlockSpec(memory_space=pl.ANY),
                  pl.BlockSpec(memory_space=pl.ANY)],
        out_specs=pl.BlockSpec(memory_space=pl.ANY),
        out_shape=jax.ShapeDtypeStruct(xs_zero.shape, xs_zero.dtype),
        scratch_shapes=[pltpu.SemaphoreType.DMA(())],
        input_output_aliases={2: 0},
        compiler_params=_cparams(("arbitrary",)),
        name="moe_scatter",
    )(pos_flat, h2, xs_zero)


def _moe_ffn_kernel(blk_e_ref, n_used_ref, x_ref, wg_ref, wu_ref, bg_ref, bu_ref, wd_ref, bd_ref, y_ref):
    i = pl.program_id(0)

    @pl.when(i < n_used_ref[0])
    def _():
        x = x_ref[...].astype(BF16)
        gate = jnp.minimum(_dot(x, wg_ref[0]) + bg_ref[0], SWIGLU_LIMIT)
        up = jnp.clip(_dot(x, wu_ref[0]) + bu_ref[0], -SWIGLU_LIMIT, SWIGLU_LIMIT)
        act = (up + 1.0) * gate * jax.nn.sigmoid(SWIGLU_ALPHA * gate)
        y_ref[...] = _dot(act.astype(BF16), wd_ref[0]) + bd_ref[0]

    @pl.when(i >= n_used_ref[0])
    def _():
        y_ref[...] = jnp.zeros(y_ref.shape, F32)


def _moe_ffn(blk_e, n_used, xs, wg, wu, bg, bu, wd, bd):
    p, d = xs.shape
    dff = wg.shape[2]
    n_blk = p // MOE_BLK
    w_spec = lambda r, c: pl.BlockSpec((1, r, c), lambda i, be, nu: (be[i], 0, 0))
    grid_spec = pltpu.PrefetchScalarGridSpec(
        num_scalar_prefetch=2,
        grid=(n_blk,),
        in_specs=[pl.BlockSpec((MOE_BLK, d), lambda i, be, nu: (i, 0)),
                  w_spec(d, dff), w_spec(d, dff), w_spec(1, dff), w_spec(1, dff),
                  w_spec(dff, d), w_spec(1, d)],
        out_specs=pl.BlockSpec((MOE_BLK, d), lambda i, be, nu: (i, 0)),
    )
    return pl.pallas_call(
        _moe_ffn_kernel,
        grid_spec=grid_spec,
        out_shape=jax.ShapeDtypeStruct((p, d), F32),
        compiler_params=_cparams(("arbitrary",)),
        name="moe_ffn",
    )(blk_e, n_used, xs, wg, wu, bg, bu, wd, bd)


def _moe_combine_kernel(pos_ref, gate_ref, x_ref, ys_ref, o_ref, buf_ref, sem, *, tm):
    def copy(t, k):
        return pltpu.make_async_copy(ys_ref.at[pl.ds(pos_ref[t * TOP_K + k], 1)],
                                     buf_ref.at[k, pl.ds(t, 1)], sem)

    def issue(t, c):
        for k in range(TOP_K):
            copy(t, k).start()
        return c

    def drain(t, c):
        for k in range(TOP_K):
            copy(t, k).wait()
        return c

    lax.fori_loop(0, tm, issue, 0)
    lax.fori_loop(0, tm, drain, 0)
    gates = gate_ref[...]
    acc = x_ref[...]
    for k in range(TOP_K):
        acc = acc + gates[:, k:k + 1] * buf_ref[k]
    o_ref[...] = acc


def _moe_combine(pos_flat, gates, x2, ys):
    n, d = x2.shape
    tm = 256
    kern = functools.partial(_moe_combine_kernel, tm=tm)
    return pl.pallas_call(
        kern,
        grid=(n // tm,),
        in_specs=[pl.BlockSpec((tm * TOP_K,), lambda i: (i,), memory_space=pltpu.SMEM),
                  pl.BlockSpec((tm, LANES), lambda i: (i, 0)),
                  pl.BlockSpec((tm, d), lambda i: (i, 0)),
                  pl.BlockSpec(memory_space=pl.ANY)],
        out_specs=pl.BlockSpec((tm, d), lambda i: (i, 0)),
        out_shape=jax.ShapeDtypeStruct((n, d), F32),
        scratch_shapes=[pltpu.VMEM((TOP_K, tm, d), F32), pltpu.SemaphoreType.DMA(())],
        compiler_params=_cparams(("arbitrary",)),
        name="moe_combine",
    )(pos_flat, gates, x2, ys)


def _rel_bucket_np(dist):
    n = np.maximum(dist, 0)
    nf = np.maximum(n, 1).astype(np.float32)
    ratio = np.log(nf / np.float32(REL_MAX_EXACT)) / np.float32(math.log(REL_MAX_DIST / REL_MAX_EXACT))
    large = REL_MAX_EXACT + (ratio * np.float32(REL_BUCKETS - REL_MAX_EXACT)).astype(np.int32)
    large = np.minimum(large, REL_BUCKETS - 1)
    return np.where(n < REL_MAX_EXACT, n, large)


def _bias_tables(rel_table, nc):
    q = np.arange(NSA_TQ)[:, None]
    c = np.arange(2 * nc)[None, :]
    b_cmp = _rel_bucket_np(q - (CMP_LEN - 1) - CMP_STRIDE * (c - nc))
    c = np.arange(2 * NSA_TK + WIN)[None, :]
    b_tok = _rel_bucket_np(q + WIN - (c - NSA_TK))
    tbl = rel_table.astype(F32).T
    return tbl[:, b_cmp], tbl[:, b_tok]


def kernel(x, mem, g_attn_norm, w_in, g_cq, w_uq, g_ckv, w_ukv, g_q_mla, g_k_mla, cmp_k_pos, cmp_k_w1, cmp_k_w2, cmp_v_pos, cmp_v_w1, cmp_v_w2, g_q_nsa, g_k_nsa, rel_table, g_out_mla, g_out_nsa, w_out, g_mem_norm, g_mem_src, w_mq, w_mkv, g_mq, g_mk, w_mo, g_moe_norm, router_w, router_b, w_gate_up, b_gate_up, w_down, b_down):
    batch, seq, d = x.shape
    n = batch * seq
    depth = w_in.shape[0]
    assert seq % 512 == 0 and seq // SLC_LEN <= LANES
    row = lambda v: v.reshape(1, -1).astype(F32)
    tile2 = lambda v: jnp.concatenate([v, v]).reshape(1, -1).astype(F32)

    x2d = x.reshape(n, d)
    for l in range(depth):
        wi = w_in[l]
        w_in_r = jnp.concatenate(
            [wi[:, 0:384], wi[:, 416:928], wi[:, 928:1696], wi[:, 384:416], wi[:, 1696:1720],
             jnp.zeros((d, IN_COLS_PAD - 1720), wi.dtype)], axis=1).astype(BF16)
        dq = MLA_NOPE + MLA_ROPE
        wuq_r = jnp.pad(w_uq[l].reshape(MLA_Q_RANK, MLA_HEADS, dq), ((0, 0), (0, 0), (0, LANES - dq)))
        wuq_r = wuq_r.reshape(MLA_Q_RANK, MLA_HEADS * LANES).astype(BF16)
        wukv = w_ukv[l].reshape(MLA_KV_RANK, MLA_HEADS, MLA_NOPE + MLA_V)
        wuk_r = jnp.pad(wukv[:, :, :MLA_NOPE], ((0, 0), (0, 0), (0, LANES - MLA_NOPE)))
        wuk_r = wuk_r.reshape(MLA_KV_RANK, MLA_HEADS * LANES).astype(BF16)
        wuv_r = wukv[:, :, MLA_NOPE:].reshape(MLA_KV_RANK, MLA_HEADS * MLA_V).astype(BF16)
        gq_pad = jnp.pad(g_q_mla[l], (0, LANES - dq)).reshape(1, LANES)
        gk_pad = jnp.pad(g_k_mla[l], (0, LANES - dq)).reshape(1, LANES)

        half = MLA_ROPE // 2
        inv = ROPE_THETA ** (-jnp.arange(half, dtype=F32) / half)
        ang = jnp.arange(seq, dtype=F32)[:, None] * inv
        cos, sin = jnp.cos(ang), jnp.sin(ang)
        zeros = jnp.zeros((seq, half), F32)
        rc = jnp.concatenate([jnp.ones((seq, MLA_NOPE), F32), cos, cos, jnp.zeros((seq, LANES - dq), F32)], 1)
        rs1 = jnp.concatenate([jnp.zeros((seq, MLA_NOPE), F32), -sin, zeros, jnp.zeros((seq, LANES - dq), F32)], 1)
        rs2 = jnp.concatenate([jnp.zeros((seq, MLA_NOPE), F32), zeros, sin, jnp.zeros((seq, LANES - dq), F32)], 1)

        mla_in, qn, kv6, misc = _in_proj(x2d, row(g_attn_norm[l]), w_in_r)
        q_m, k_m, v_m = _mla_prep(mla_in, misc, row(g_cq[l]), wuq_r, row(g_ckv[l]), wuk_r, wuv_r,
                                  gq_pad, gk_pad, rc, rs1, rs2, seq)
        y_mla = _mla_attn(q_m, k_m, v_m, batch, seq)

        nc = seq // CMP_STRIDE
        half_len = CMP_LEN // 2
        eye_g = jnp.eye(NSA_KV_GROUPS, dtype=F32)

        def cmp_weights(pos, w1, w2):
            out = []
            for part in range(2):
                sl = slice(part * half_len, (part + 1) * half_len)
                out.append(jnp.broadcast_to(pos[sl][:, None, :], (half_len, NSA_KV_GROUPS, NSA_HD))
                           .reshape(1, -1))
            for part in range(2):
                sl = slice(part * half_len, (part + 1) * half_len)
                wexp = jnp.einsum('ldf,gh->lgdhf', w1[sl], eye_g)
                out.append(wexp.reshape(half_len * NSA_KV_GROUPS * NSA_HD, NSA_KV_GROUPS * CMP_HIDDEN).astype(BF16))
            out.append(w2.astype(BF16))
            return out

        chunk_w = CMP_STRIDE * NSA_KV_GROUPS * NSA_HD
        kcmp, vcmp = _nsa_cmp(kv6[0].reshape(batch, nc, chunk_w), kv6[1].reshape(batch, nc, chunk_w),
                              cmp_weights(cmp_k_pos[l], cmp_k_w1[l], cmp_k_w2[l]),
                              cmp_weights(cmp_v_pos[l], cmp_v_w1[l], cmp_v_w2[l]), row(g_k_nsa[l]))
        q_n, ks, vs, kw, vw, gates_n = _nsa_prep(qn, kv6, misc, tile2(g_q_nsa[l]), tile2(g_k_nsa[l]))
        by_batch = lambda t: t.reshape(NSA_KV_GROUPS, batch, seq, NSA_HD)
        front_pad = lambda t: jnp.pad(by_batch(t), ((0, 0), (0, 0), (WIN, 0), (0, 0)))
        ecmp, tn = _bias_tables(rel_table, nc)
        n_idx = np.arange(nc)[:, None]
        j_idx = np.arange(LANES)[None, :]
        ovl = ((CMP_STRIDE * n_idx < SLC_LEN * j_idx + SLC_LEN)
               & (CMP_STRIDE * n_idx + CMP_LEN - 1 >= SLC_LEN * j_idx)
               & (n_idx < nc - 1) & (j_idx < seq // SLC_LEN)).astype(np.float32)
        efull = (np.arange(seq)[None, :] // SLC_LEN == np.arange(LANES)[:, None]).astype(np.float32)
        y_nsa = _nsa_attn(q_n, kcmp, vcmp, by_batch(ks), by_batch(vs), front_pad(kw), front_pad(vw), gates_n,
                          ecmp, tn, jnp.asarray(ovl), jnp.asarray(efull, dtype=BF16), batch, seq)

        k_mem, v_mem = _mem_kv(mem, row(g_mem_src[l]), w_mkv[l].reshape(d, 2 * MEM_HEADS * MEM_HD).astype(BF16),
                               tile2(g_mk[l]))
        rw = jnp.pad(router_w[l], ((0, 0), (0, LANES - N_EXPERTS)))
        rb = jnp.pad(router_b[l], (0, LANES - N_EXPERTS), constant_values=NEG_INF).reshape(1, LANES)
        wo = w_out[l].astype(BF16)
        n_mla = MLA_HEADS * MLA_V
        x2, h2, topi, gates_e, counts = _post_attn(
            x2d, y_mla, y_nsa, row(g_out_mla[l]), row(g_out_nsa[l]), wo[:n_mla], wo[n_mla:],
            row(g_mem_norm[l]), w_mq[l].astype(BF16), tile2(g_mq[l]), k_mem, v_mem, w_mo[l].astype(BF16),
            row(g_moe_norm[l]), rw, rb, batch, seq)

        cnt = counts[0].astype(jnp.int32)
        padded = (cnt + MOE_BLK - 1) // MOE_BLK * MOE_BLK
        pad_end = jnp.cumsum(padded)
        pad_start = (pad_end - padded).astype(F32).reshape(1, LANES)
        p_rows = (n * TOP_K // MOE_BLK + N_EXPERTS) * MOE_BLK
        n_blk = p_rows // MOE_BLK
        blk_e = jnp.minimum(jnp.searchsorted(pad_end[:N_EXPERTS], jnp.arange(n_blk) * MOE_BLK, side='right'),
                            N_EXPERTS - 1).astype(jnp.int32)
        n_used = (pad_end[N_EXPERTS - 1] // MOE_BLK).astype(jnp.int32).reshape(1)
        tm_pos = 256
        tri = (np.arange(tm_pos)[None, :] < np.arange(tm_pos)[:, None]).astype(np.float32)
        pos = _moe_pos(topi, pad_start, jnp.asarray(tri, dtype=BF16))
        pos_flat = pos[:, :TOP_K].reshape(n * TOP_K)
        xs = _moe_scatter(pos_flat, h2, jnp.zeros((p_rows, d), F32))
        wgu = w_gate_up[l]
        bgu = b_gate_up[l]
        ys = _moe_ffn(blk_e, n_used, xs, wgu[:, :, 0::2].astype(BF16), wgu[:, :, 1::2].astype(BF16),
                      bgu[:, None, 0::2], bgu[:, None, 1::2], w_down[l].astype(BF16), b_down[l][:, None, :])
        x2d = _moe_combine(pos_flat, gates_e, x2, ys)
    return x2d.reshape(batch, seq, d)
```

```python
import functools
import math

import numpy as np
import jax
import jax.numpy as jnp
from jax import lax
from jax.experimental import pallas as pl
from jax.experimental.pallas import tpu as pltpu

F32 = jnp.float32
BF16 = jnp.bfloat16

EPS = 1e-6
NEG_INF = -1e30
LANES = 128

MLA_HEADS = 8
MLA_NOPE = 64
MLA_ROPE = 32
MLA_V = 64
MLA_Q_RANK = 256
MLA_KV_RANK = 128
ROPE_THETA = 10000.0

NSA_HEADS = 8
NSA_KV_GROUPS = 2
NSA_HPG = NSA_HEADS // NSA_KV_GROUPS
NSA_HD = 64
CMP_LEN = 32
CMP_STRIDE = 16
CMP_HIDDEN = 128
SLC_LEN = 64
SLC_TOPK = 16
WIN = 512
FORCE_SCORE = 1e9
NSA_TQ = 128
NSA_TK = 512

REL_BUCKETS = 32
REL_MAX_EXACT = 16
REL_MAX_DIST = 512

MEM_HEADS = 4
MEM_HD = 64

N_EXPERTS = 32
TOP_K = 4
SWIGLU_LIMIT = 7.0
SWIGLU_ALPHA = 1.702
MOE_BLK = 256

VMEM_LIMIT = 56 * 1024 * 1024


def _cparams(sem, vmem=VMEM_LIMIT):
    return pltpu.CompilerParams(dimension_semantics=sem, vmem_limit_bytes=vmem)


def _rms(x, g):
    return x * lax.rsqrt(jnp.mean(x * x, axis=-1, keepdims=True) + EPS) * g


def _seg_rms64(t, g2):
    lane = lax.broadcasted_iota(jnp.int32, t.shape, 1)
    sq = t * t
    lo = jnp.sum(jnp.where(lane < 64, sq, 0.0), axis=-1, keepdims=True)
    hi = jnp.sum(jnp.where(lane >= 64, sq, 0.0), axis=-1, keepdims=True)
    ms = jnp.where(lane < 64, lo, hi) * (1.0 / 64.0)
    return t * lax.rsqrt(ms + EPS) * g2


def _dot(a, b):
    return jnp.dot(a, b, preferred_element_type=F32)


def _dot_nt(a, b):
    return lax.dot_general(a, b, (((1,), (1,)), ((), ())), preferred_element_type=F32)


IN_COLS_PAD = 1792


def _in_proj_kernel(x_ref, g_ref, w_ref, mla_ref, qn_ref, kv_ref, misc_ref):
    h = _rms(x_ref[...], g_ref[...])
    p = _dot(h.astype(BF16), w_ref[...])
    mla_ref[...] = p[:, 0:384]
    qn_ref[...] = p[:, 384:896]
    for j in range(6):
        kv_ref[j] = p[:, 896 + 128 * j:1024 + 128 * j]
    misc_ref[...] = p[:, 1664:1792]


def _in_proj(x2d, g, w):
    n, d = x2d.shape
    tm = 512
    return pl.pallas_call(
        _in_proj_kernel,
        grid=(n // tm,),
        in_specs=[pl.BlockSpec((tm, d), lambda i: (i, 0)),
                  pl.BlockSpec((1, d), lambda i: (0, 0)),
                  pl.BlockSpec((d, IN_COLS_PAD), lambda i: (0, 0))],
        out_specs=[pl.BlockSpec((tm, 384), lambda i: (i, 0)),
                   pl.BlockSpec((tm, 512), lambda i: (i, 0)),
                   pl.BlockSpec((6, tm, 128), lambda i: (0, i, 0)),
                   pl.BlockSpec((tm, 128), lambda i: (i, 0))],
        out_shape=[jax.ShapeDtypeStruct((n, 384), F32),
                   jax.ShapeDtypeStruct((n, 512), F32),
                   jax.ShapeDtypeStruct((6, n, 128), F32),
                   jax.ShapeDtypeStruct((n, 128), F32)],
        compiler_params=_cparams(("parallel",)),
        name="in_proj",
    )(x2d, g, w)


def _mla_prep_kernel(mla_ref, misc_ref, gcq_ref, wuq_ref, gckv_ref, wuk_ref, wuv_ref, gq_ref, gk_ref,
                     rc_ref, rs1_ref, rs2_ref, q_ref, k_ref, v_ref):
    p = mla_ref[...]
    cqn = _rms(p[:, 0:MLA_Q_RANK], gcq_ref[...]).astype(BF16)
    ckvn = _rms(p[:, MLA_Q_RANK:MLA_Q_RANK + MLA_KV_RANK], gckv_ref[...]).astype(BF16)
    qall = _dot(cqn, wuq_ref[...])
    kall = _dot(ckvn, wuk_ref[...])
    vall = _dot(ckvn, wuv_ref[...])
    misc = misc_ref[...]
    lane = lax.broadcasted_iota(jnp.int32, misc.shape, 1)
    in_rope = (lane >= MLA_NOPE) & (lane < MLA_NOPE + MLA_ROPE)
    krope = jnp.where(in_rope, pltpu.roll(misc, MLA_NOPE, axis=1), 0.0)
    rc, rs1, rs2 = rc_ref[...], rs1_ref[...], rs2_ref[...]
    gq, gk = gq_ref[...], gk_ref[...]
    inv_dk = 1.0 / (MLA_NOPE + MLA_ROPE)
    scale = (MLA_NOPE + MLA_ROPE) ** -0.5

    def norm_rope(t, g):
        t = t * lax.rsqrt(jnp.sum(t * t, axis=-1, keepdims=True) * inv_dk + EPS) * g
        half = MLA_ROPE // 2
        return t * rc + pltpu.roll(t, LANES - half, axis=1) * rs1 + pltpu.roll(t, half, axis=1) * rs2

    for h in range(MLA_HEADS):
        qh = norm_rope(qall[:, h * LANES:(h + 1) * LANES], gq)
        q_ref[h] = (qh * scale).astype(BF16)
        kh = norm_rope(kall[:, h * LANES:(h + 1) * LANES] + krope, gk)
        k_ref[h] = kh.astype(BF16)
        v_ref[h] = vall[:, h * MLA_V:(h + 1) * MLA_V].astype(BF16)


def _mla_prep(mla, misc, gcq, wuq, gckv, wuk, wuv, gq, gk, rc, rs1, rs2, seq):
    n = mla.shape[0]
    tm = 256
    ns = seq // tm
    full = lambda a: pl.BlockSpec(a.shape, lambda i: (0,) * a.ndim)
    rope_spec = pl.BlockSpec((tm, LANES), lambda i: (i % ns, 0))
    return pl.pallas_call(
        _mla_prep_kernel,
        grid=(n // tm,),
        in_specs=[pl.BlockSpec((tm, 384), lambda i: (i, 0)),
                  pl.BlockSpec((tm, LANES), lambda i: (i, 0)),
                  full(gcq), full(wuq), full(gckv), full(wuk), full(wuv), full(gq), full(gk),
                  rope_spec, rope_spec, rope_spec],
        out_specs=[pl.BlockSpec((MLA_HEADS, tm, LANES), lambda i: (0, i, 0)),
                   pl.BlockSpec((MLA_HEADS, tm, LANES), lambda i: (0, i, 0)),
                   pl.BlockSpec((MLA_HEADS, tm, MLA_V), lambda i: (0, i, 0))],
        out_shape=[jax.ShapeDtypeStruct((MLA_HEADS, n, LANES), BF16),
                   jax.ShapeDtypeStruct((MLA_HEADS, n, LANES), BF16),
                   jax.ShapeDtypeStruct((MLA_HEADS, n, MLA_V), BF16)],
        compiler_params=_cparams(("parallel",)),
        name="mla_prep",
    )(mla, misc, gcq, wuq, gckv, wuk, wuv, gq, gk, rc, rs1, rs2)


def _mla_attn_kernel(q_ref, k_ref, v_ref, o_ref, m_ref, l_ref, acc_ref, *, tq, tk):
    qi, ki = pl.program_id(1), pl.program_id(2)

    @pl.when(ki == 0)
    def _():
        m_ref[...] = jnp.full(m_ref.shape, NEG_INF, F32)
        l_ref[...] = jnp.zeros(l_ref.shape, F32)
        acc_ref[...] = jnp.zeros(acc_ref.shape, F32)

    def step(masked):
        def head(h, carry):
            s = _dot_nt(q_ref[h], k_ref[h])
            if masked:
                row = lax.broadcasted_iota(jnp.int32, s.shape, 0)
                col = lax.broadcasted_iota(jnp.int32, s.shape, 1)
                s = jnp.where(col <= row, s, NEG_INF)
            m_old = m_ref[h]
            m_new = jnp.maximum(m_old, jnp.max(s, axis=-1, keepdims=True))
            alpha = jnp.exp(m_old - m_new)
            p = jnp.exp(s - m_new)
            l_ref[h] = alpha * l_ref[h] + jnp.sum(p, axis=-1, keepdims=True)
            acc_ref[h] = alpha * acc_ref[h] + _dot(p.astype(BF16), v_ref[h])
            m_ref[h] = m_new
            return carry
        lax.fori_loop(0, MLA_HEADS, head, 0)

    @pl.when(ki < qi)
    def _():
        step(False)

    @pl.when(ki == qi)
    def _():
        step(True)

    @pl.when(ki == pl.num_programs(2) - 1)
    def _():
        for h in range(MLA_HEADS):
            o_ref[:, h * MLA_V:(h + 1) * MLA_V] = acc_ref[h] / l_ref[h]


def _mla_attn(q, k, v, batch, seq):
    n = q.shape[1]
    tq = tk = 512
    nq = seq // tq
    kern = functools.partial(_mla_attn_kernel, tq=tq, tk=tk)
    return pl.pallas_call(
        kern,
        grid=(batch, nq, nq),
        in_specs=[pl.BlockSpec((MLA_HEADS, tq, LANES), lambda b, i, j: (0, b * nq + i, 0)),
                  pl.BlockSpec((MLA_HEADS, tk, LANES), lambda b, i, j: (0, b * nq + jnp.minimum(i, j), 0)),
                  pl.BlockSpec((MLA_HEADS, tk, MLA_V), lambda b, i, j: (0, b * nq + jnp.minimum(i, j), 0))],
        out_specs=pl.BlockSpec((tq, MLA_HEADS * MLA_V), lambda b, i, j: (b * nq + i, 0)),
        out_shape=jax.ShapeDtypeStruct((n, MLA_HEADS * MLA_V), F32),
        scratch_shapes=[pltpu.VMEM((MLA_HEADS, tq, 1), F32),
                        pltpu.VMEM((MLA_HEADS, tq, 1), F32),
                        pltpu.VMEM((MLA_HEADS, tq, MLA_V), F32)],
        compiler_params=_cparams(("parallel", "parallel", "arbitrary")),
        name="mla_attn",
    )(q, k, v)


def _nsa_cmp_kernel(kc_ref, vc_ref, pak_ref, pbk_ref, wak_ref, wbk_ref, w2k_ref,
                    pav_ref, pbv_ref, wav_ref, wbv_ref, w2v_ref, gk_ref, kout_ref, vout_ref):
    def compress(chunks, pa, pb, wa, wb, w2):
        nc = chunks.shape[0]
        ha = _dot((chunks + pa).astype(BF16), wa)
        hb = _dot((chunks + pb).astype(BF16), wb)
        hid = jax.nn.gelu(ha + pltpu.roll(hb, nc - 1, axis=0))
        return [_dot(hid[:, g * CMP_HIDDEN:(g + 1) * CMP_HIDDEN].astype(BF16), w2)
                for g in range(NSA_KV_GROUPS)]

    kc = compress(kc_ref[0], pak_ref[...], pbk_ref[...], wak_ref[...], wbk_ref[...], w2k_ref[...])
    vc = compress(vc_ref[0], pav_ref[...], pbv_ref[...], wav_ref[...], wbv_ref[...], w2v_ref[...])
    for g in range(NSA_KV_GROUPS):
        kout_ref[0, g] = _rms(kc[g], gk_ref[...]).astype(BF16)
        vout_ref[0, g] = vc[g].astype(BF16)


def _nsa_cmp(kc_chunks, vc_chunks, wk, wv, gk):
    b, nc, width = kc_chunks.shape
    full = lambda a: pl.BlockSpec(a.shape, lambda i: (0,) * a.ndim)
    chunk_spec = pl.BlockSpec((1, nc, width), lambda i: (i, 0, 0))
    out_spec = pl.BlockSpec((1, NSA_KV_GROUPS, nc, NSA_HD), lambda i: (i, 0, 0, 0))
    out_sds = jax.ShapeDtypeStruct((b, NSA_KV_GROUPS, nc, NSA_HD), BF16)
    return pl.pallas_call(
        _nsa_cmp_kernel,
        grid=(b,),
        in_specs=[chunk_spec, chunk_spec] + [full(a) for a in wk] + [full(a) for a in wv] + [full(gk)],
        out_specs=[out_spec, out_spec],
        out_shape=[out_sds, out_sds],
        compiler_params=_cparams(("parallel",)),
        name="nsa_cmp",
    )(kc_chunks, vc_chunks, *wk, *wv, gk)


def _nsa_prep_kernel(qn_ref, ks_ref, vs_ref, kw_ref, vw_ref, misc_ref, gq_ref, gk_ref,
                     q_ref, kso_ref, vso_ref, kwo_ref, vwo_ref, gate_ref):
    gq2, gk2 = gq_ref[...], gk_ref[...]
    scale = NSA_HD ** -0.5
    for c in range(NSA_HEADS // 2):
        t = _seg_rms64(qn_ref[:, c * LANES:(c + 1) * LANES], gq2) * scale
        q_ref[2 * c] = t[:, :NSA_HD].astype(BF16)
        q_ref[2 * c + 1] = t[:, NSA_HD:].astype(BF16)
    ksn = _seg_rms64(ks_ref[0], gk2)
    kwn = _seg_rms64(kw_ref[0], gk2)
    vs, vw = vs_ref[0], vw_ref[0]
    for g in range(NSA_KV_GROUPS):
        sl = slice(g * NSA_HD, (g + 1) * NSA_HD)
        kso_ref[g] = ksn[:, sl].astype(BF16)
        kwo_ref[g] = kwn[:, sl].astype(BF16)
        vso_ref[g] = vs[:, sl].astype(BF16)
        vwo_ref[g] = vw[:, sl].astype(BF16)
    sig = jax.nn.sigmoid(misc_ref[...])
    per_group = 3 * NSA_HPG
    for g in range(NSA_KV_GROUPS):
        gate_ref[g] = pltpu.roll(sig, LANES - (MLA_ROPE + per_group * g), axis=1)


def _nsa_prep(qn, kv, misc, gq2, gk2):
    n = qn.shape[0]
    tm = 512
    full = lambda a: pl.BlockSpec(a.shape, lambda i: (0,) * a.ndim)
    kv_spec = lambda j: pl.BlockSpec((1, tm, LANES), lambda i, j=j: (j, i, 0))
    g_spec = pl.BlockSpec((NSA_KV_GROUPS, tm, NSA_HD), lambda i: (0, i, 0))
    g_sds = jax.ShapeDtypeStruct((NSA_KV_GROUPS, n, NSA_HD), BF16)
    return pl.pallas_call(
        _nsa_prep_kernel,
        grid=(n // tm,),
        in_specs=[pl.BlockSpec((tm, 512), lambda i: (i, 0)),
                  kv_spec(2), kv_spec(3), kv_spec(4), kv_spec(5),
                  pl.BlockSpec((tm, LANES), lambda i: (i, 0)), full(gq2), full(gk2)],
        out_specs=[pl.BlockSpec((NSA_HEADS, tm, NSA_HD), lambda i: (0, i, 0)),
                   g_spec, g_spec, g_spec, g_spec,
                   pl.BlockSpec((NSA_KV_GROUPS, tm, LANES), lambda i: (0, i, 0))],
        out_shape=[jax.ShapeDtypeStruct((NSA_HEADS, n, NSA_HD), BF16),
                   g_sds, g_sds, g_sds, g_sds,
                   jax.ShapeDtypeStruct((NSA_KV_GROUPS, n, LANES), F32)],
        compiler_params=_cparams(("parallel",)),
        name="nsa_prep",
    )(qn, kv, kv, kv, kv, misc, gq2, gk2)


def _nsa_attn_kernel(q_ref, kc_ref, vc_ref, ks_ref, vs_ref, kw_ref, vw_ref, gate_ref,
                     ecmp_ref, tn_ref, ovl_ref, efull_ref, o_ref, *, nc, n_sel):
    tq, tk, hpg = NSA_TQ, NSA_TK, NSA_HPG
    i = pl.program_id(2)
    q0 = i * tq
    q4 = q_ref[...].reshape(hpg * tq, NSA_HD)

    def rows_cols(width):
        row = lax.broadcasted_iota(jnp.int32, (tq, width), 0)
        col = lax.broadcasted_iota(jnp.int32, (tq, width), 1)
        return row, col

    def softmax_rows(s3, valid):
        s3 = jnp.where(valid[None], s3, NEG_INF)
        m = jnp.max(s3, axis=-1, keepdims=True)
        e = jnp.exp(s3 - m) * valid[None].astype(F32)
        return e / jnp.maximum(jnp.sum(e, axis=-1, keepdims=True), 1e-30)

    row, col = rows_cols(nc)
    s = _dot_nt(q4, kc_ref[0, 0])
    shift = nc + (CMP_STRIDE // 2) * i
    bias = jnp.stack([pltpu.roll(ecmp_ref[h], shift, axis=1)[:, :nc] for h in range(hpg)])
    valid = (CMP_STRIDE * col + (CMP_LEN - 1)) <= (q0 + row)
    p_cmp = softmax_rows(s.reshape(hpg, tq, nc) + bias, valid)
    o_cmp = _dot(p_cmp.reshape(hpg * tq, nc).astype(BF16), vc_ref[0, 0])

    imp = jnp.dot(jnp.sum(p_cmp, axis=0), ovl_ref[...], preferred_element_type=F32,
                  precision=lax.Precision.HIGHEST)
    row, j = rows_cols(LANES)
    qpos = q0 + row
    cur = qpos // SLC_LEN
    forced = (j == 0) | (j == cur) | (j == cur - 1)
    imp = jnp.where(forced, FORCE_SCORE, imp)
    imp = jnp.where(j * SLC_LEN <= qpos, imp, NEG_INF)
    sel = jnp.zeros((tq, LANES), F32)
    for _ in range(n_sel):
        best = jnp.max(imp, axis=-1, keepdims=True)
        first = jnp.min(jnp.where(imp == best, j, LANES), axis=-1, keepdims=True)
        hit = j == first
        sel = jnp.where(hit, 1.0, sel)
        imp = jnp.where(hit, -jnp.inf, imp)
    sel = sel.astype(BF16)

    row, col = rows_cols(tk)

    def sel_tile(kt, carry):
        m_old, l_old, acc = carry
        k0 = pl.multiple_of(kt * tk, tk)
        s = _dot_nt(q4, ks_ref[0, 0, pl.ds(k0, tk), :])
        start = pl.multiple_of(jnp.maximum(k0 + tk + WIN - q0, 0), LANES)
        bias = tn_ref[:, :, pl.ds(start, tk)]
        in_sel = _dot(sel, efull_ref[:, pl.ds(k0, tk)]) > 0.5
        valid = in_sel & ((k0 + col) <= (q0 + row))
        s3 = jnp.where(valid[None], s.reshape(hpg, tq, tk) + bias, NEG_INF).reshape(hpg * tq, tk)
        m_new = jnp.maximum(m_old, jnp.max(s3, axis=-1, keepdims=True))
        vf = jnp.broadcast_to(valid.astype(F32)[None], (hpg, tq, tk)).reshape(hpg * tq, tk)
        p = jnp.exp(s3 - m_new) * vf
        alpha = jnp.exp(m_old - m_new)
        l_new = alpha * l_old + jnp.sum(p, axis=-1, keepdims=True)
        acc = alpha * acc + _dot(p.astype(BF16), vs_ref[0, 0, pl.ds(k0, tk), :])
        return m_new, l_new, acc

    n_tiles = (q0 + tq + tk - 1) // tk
    init = (jnp.full((hpg * tq, 1), NEG_INF, F32), jnp.zeros((hpg * tq, 1), F32),
            jnp.zeros((hpg * tq, NSA_HD), F32))
    _, l_slc, acc_slc = lax.fori_loop(0, n_tiles, sel_tile, init)
    o_slc = acc_slc / jnp.maximum(l_slc, 1e-30)

    wk = WIN + tq
    row, col = rows_cols(wk)
    s = _dot_nt(q4, kw_ref[0, 0, pl.ds(pl.multiple_of(q0, tq), wk), :])
    dist = row + WIN - col
    valid = (dist >= 0) & (dist < WIN) & (q0 - WIN + col >= 0)
    p_win = softmax_rows(s.reshape(hpg, tq, wk) + tn_ref[:, :, tk:tk + wk], valid)
    o_win = _dot(p_win.reshape(hpg * tq, wk).astype(BF16),
                 vw_ref[0, 0, pl.ds(pl.multiple_of(q0, tq), wk), :])

    gates = gate_ref[0]
    outs = []
    for h in range(hpg):
        rs = slice(h * tq, (h + 1) * tq)
        outs.append(gates[:, 3 * h:3 * h + 1] * o_cmp[rs] + gates[:, 3 * h + 1:3 * h + 2] * o_slc[rs]
                    + gates[:, 3 * h + 2:3 * h + 3] * o_win[rs])
    o_ref[...] = jnp.concatenate(outs, axis=1)


def _nsa_attn(q, kcmp, vcmp, ks, vs, kwp, vwp, gates, ecmp, tn, ovl, efull, batch, seq):
    n = q.shape[1]
    nc = kcmp.shape[2]
    nq = seq // NSA_TQ
    n_sel = min(SLC_TOPK, seq // SLC_LEN)
    g_, hpg = NSA_KV_GROUPS, NSA_HPG
    kern = functools.partial(_nsa_attn_kernel, nc=nc, n_sel=n_sel)
    seq_spec = lambda length: pl.BlockSpec((1, 1, length, NSA_HD), lambda b, g, i: (g, b, 0, 0))
    cmp_spec = pl.BlockSpec((1, 1, nc, NSA_HD), lambda b, g, i: (b, g, 0, 0))
    return pl.pallas_call(
        kern,
        grid=(batch, g_, nq),
        in_specs=[pl.BlockSpec((hpg, NSA_TQ, NSA_HD), lambda b, g, i: (g, b * nq + i, 0)),
                  cmp_spec, cmp_spec,
                  seq_spec(seq), seq_spec(seq), seq_spec(seq + WIN), seq_spec(seq + WIN),
                  pl.BlockSpec((1, NSA_TQ, LANES), lambda b, g, i: (g, b * nq + i, 0)),
                  pl.BlockSpec((hpg, NSA_TQ, 2 * nc), lambda b, g, i: (g, 0, 0)),
                  pl.BlockSpec((hpg, NSA_TQ, 2 * NSA_TK + WIN), lambda b, g, i: (g, 0, 0)),
                  pl.BlockSpec(ovl.shape, lambda b, g, i: (0, 0)),
                  pl.BlockSpec(efull.shape, lambda b, g, i: (0, 0))],
        out_specs=pl.BlockSpec((NSA_TQ, hpg * NSA_HD), lambda b, g, i: (b * nq + i, g)),
        out_shape=jax.ShapeDtypeStruct((n, NSA_HEADS * NSA_HD), F32),
        compiler_params=_cparams(("parallel", "parallel", "arbitrary")),
        name="nsa_attn",
    )(q, kcmp, vcmp, ks, vs, kwp, vwp, gates, ecmp, tn, ovl, efull)


def _mem_kv_kernel(mem_ref, g_ref, w_ref, gk_ref, k_ref, v_ref):
    mn = _rms(mem_ref[0], g_ref[...]).astype(BF16)
    kv = _dot(mn, w_ref[...])
    width = MEM_HEADS * MEM_HD
    for c in range(width // LANES):
        kn = _seg_rms64(kv[:, c * LANES:(c + 1) * LANES], gk_ref[...])
        vv = kv[:, width + c * LANES:width + (c + 1) * LANES]
        for half in range(2):
            sl = slice(half * MEM_HD, (half + 1) * MEM_HD)
            k_ref[0, 2 * c + half] = kn[:, sl].astype(BF16)
            v_ref[0, 2 * c + half] = vv[:, sl].astype(BF16)


def _mem_kv(mem, g, w, gk2):
    b, m, d = mem.shape
    full = lambda a: pl.BlockSpec(a.shape, lambda i: (0,) * a.ndim)
    spec = pl.BlockSpec((1, MEM_HEADS, m, MEM_HD), lambda i: (i, 0, 0, 0))
    sds = jax.ShapeDtypeStruct((b, MEM_HEADS, m, MEM_HD), BF16)
    return pl.pallas_call(
        _mem_kv_kernel,
        grid=(b,),
        in_specs=[pl.BlockSpec((1, m, d), lambda i: (i, 0, 0)), full(g), full(w), full(gk2)],
        out_specs=[spec, spec],
        out_shape=[sds, sds],
        compiler_params=_cparams(("parallel",)),
        name="mem_kv",
    )(mem, g, w, gk2)


def _post_attn_kernel(x_ref, ya_ref, yb_ref, goa_ref, gob_ref, woa_ref, wob_ref, gmn_ref, wmq_ref, gmq_ref,
                      km_ref, vm_ref, wmo_ref, gmoe_ref, rw_ref, rb_ref,
                      x2_ref, h2_ref, topi_ref, gate_ref, cnt_ref):
    first = (pl.program_id(0) == 0) & (pl.program_id(1) == 0)

    @pl.when(first)
    def _():
        cnt_ref[...] = jnp.zeros(cnt_ref.shape, F32)

    mixa = _rms(ya_ref[...], goa_ref[...]).astype(BF16)
    mixb = _rms(yb_ref[...], gob_ref[...]).astype(BF16)
    x1 = x_ref[...] + _dot(mixa, woa_ref[...]) + _dot(mixb, wob_ref[...])

    h = _rms(x1, gmn_ref[...]).astype(BF16)
    q = _dot(h, wmq_ref[...])
    scale = MEM_HD ** -0.5
    outs = []
    for c in range(MEM_HEADS * MEM_HD // LANES):
        qn = _seg_rms64(q[:, c * LANES:(c + 1) * LANES], gmq_ref[...]) * scale
        for half in range(2):
            hd = 2 * c + half
            qh = qn[:, half * MEM_HD:(half + 1) * MEM_HD].astype(BF16)
            s = _dot_nt(qh, km_ref[0, hd])
            e = jnp.exp(s - jnp.max(s, axis=-1, keepdims=True))
            p = e / jnp.sum(e, axis=-1, keepdims=True)
            outs.append(_dot(p.astype(BF16), vm_ref[0, hd]))
    o = jnp.concatenate(outs, axis=1).astype(BF16)
    x2 = x1 + _dot(o, wmo_ref[...])
    x2_ref[...] = x2

    h2 = _rms(x2, gmoe_ref[...])
    h2_ref[...] = h2
    logits = jnp.dot(h2, rw_ref[...], preferred_element_type=F32,
                     precision=lax.Precision.HIGHEST) + rb_ref[...]
    lane = lax.broadcasted_iota(jnp.int32, logits.shape, 1)
    topi = jnp.zeros(logits.shape, jnp.int32)
    topv = jnp.full(logits.shape, NEG_INF, F32)
    onehot = jnp.zeros(logits.shape, F32)
    for k in range(TOP_K):
        best = jnp.max(logits, axis=-1, keepdims=True)
        first_idx = jnp.min(jnp.where(logits == best, lane, LANES), axis=-1, keepdims=True)
        hit = lane == first_idx
        topi = jnp.where(lane == k, first_idx, topi)
        topv = jnp.where(lane == k, best, topv)
        onehot = jnp.where(hit, 1.0, onehot)
        logits = jnp.where(hit, -jnp.inf, logits)
    e = jnp.where(lane < TOP_K, jnp.exp(topv - jnp.max(topv, axis=-1, keepdims=True)), 0.0)
    gate_ref[...] = e / jnp.sum(e, axis=-1, keepdims=True)
    topi_ref[...] = topi
    cnt_ref[...] += jnp.sum(onehot, axis=0, keepdims=True)


def _post_attn(x2d, ya, yb, goa, gob, woa, wob, gmn, wmq, gmq2, km, vm, wmo, gmoe, rw, rb, batch, seq):
    n, d = x2d.shape
    tm = 256
    ns = seq // tm
    full = lambda a: pl.BlockSpec(a.shape, lambda b, i: (0,) * a.ndim)
    tok = lambda width: pl.BlockSpec((tm, width), lambda b, i: (b * ns + i, 0))
    mem_spec = pl.BlockSpec((1,) + km.shape[1:], lambda b, i: (b, 0, 0, 0))
    return pl.pallas_call(
        _post_attn_kernel,
        grid=(batch, ns),
        in_specs=[tok(d), tok(ya.shape[1]), tok(yb.shape[1]), full(goa), full(gob), full(woa), full(wob),
                  full(gmn), full(wmq), full(gmq2), mem_spec, mem_spec, full(wmo), full(gmoe), full(rw), full(rb)],
        out_specs=[tok(d), tok(d), tok(LANES), tok(LANES), pl.BlockSpec((1, LANES), lambda b, i: (0, 0))],
        out_shape=[jax.ShapeDtypeStruct((n, d), F32),
                   jax.ShapeDtypeStruct((n, d), F32),
                   jax.ShapeDtypeStruct((n, LANES), jnp.int32),
                   jax.ShapeDtypeStruct((n, LANES), F32),
                   jax.ShapeDtypeStruct((1, LANES), F32)],
        compiler_params=_cparams(("arbitrary", "arbitrary")),
        name="post_attn",
    )(x2d, ya, yb, goa, gob, woa, wob, gmn, wmq, gmq2, km, vm, wmo, gmoe, rw, rb)


def _moe_pos_kernel(topi_ref, start_ref, tri_ref, pos_ref, carry_ref):
    @pl.when(pl.program_id(0) == 0)
    def _():
        carry_ref[...] = jnp.zeros(carry_ref.shape, F32)

    topi = topi_ref[...]
    lane = lax.broadcasted_iota(jnp.int32, topi.shape, 1)
    hits = [lane == topi[:, k:k + 1] for k in range(TOP_K)]
    onehot = sum(h.astype(F32) for h in hits)
    before = _dot(tri_ref[...], onehot.astype(BF16))
    base = start_ref[...] + carry_ref[...] + before
    pos = jnp.zeros(topi.shape, jnp.int32)
    for k in range(TOP_K):
        pk = jnp.sum(jnp.where(hits[k], base, 0.0), axis=-1, keepdims=True).astype(jnp.int32)
        pos = jnp.where(lane == k, pk, pos)
    pos_ref[...] = pos
    carry_ref[...] += jnp.sum(onehot, axis=0, keepdims=True)


def _moe_pos(topi, pad_start, tri):
    n = topi.shape[0]
    tm = tri.shape[0]
    return pl.pallas_call(
        _moe_pos_kernel,
        grid=(n // tm,),
        in_specs=[pl.BlockSpec((tm, LANES), lambda i: (i, 0)),
                  pl.BlockSpec((1, LANES), lambda i: (0, 0)),
                  pl.BlockSpec((tm, tm), lambda i: (0, 0))],
        out_specs=pl.BlockSpec((tm, LANES), lambda i: (i, 0)),
        out_shape=jax.ShapeDtypeStruct((n, LANES), jnp.int32),
        scratch_shapes=[pltpu.VMEM((1, LANES), F32)],
        compiler_params=_cparams(("arbitrary",)),
        name="moe_pos",
    )(topi, pad_start, tri)


def _moe_scatter_kernel(pos_ref, h_ref, zero_ref, xs_ref, sem, *, tm):
    del zero_ref

    def copy(t, k):
        return pltpu.make_async_copy(h_ref.at[pl.ds(t, 1)],
                                     xs_ref.at[pl.ds(pos_ref[t * TOP_K + k], 1)], sem)

    def issue(t, c):
        for k in range(TOP_K):
            copy(t, k).start()
        return c

    def drain(t, c):
        for k in range(TOP_K):
            copy(t, k).wait()
        return c

    lax.fori_loop(0, tm, issue, 0)
    lax.fori_loop(0, tm, drain, 0)


def _moe_scatter(pos_flat, h2, xs_zero):
    n, d = h2.shape
    tm = 512
    kern = functools.partial(_moe_scatter_kernel, tm=tm)
    return pl.pallas_call(
        kern,
        grid=(n // tm,),
        in_specs=[pl.BlockSpec((tm * TOP_K,), lambda i: (i,), memory_space=pltpu.SMEM),
                  pl.BlockSpec((tm, d), lambda i: (i, 0)),
                  pl.BlockSpec(memory_space=pl.ANY)],
        out_specs=pl.BlockSpec(memory_space=pl.ANY),
        out_shape=jax.ShapeDtypeStruct(xs_zero.shape, xs_zero.dtype),
        scratch_shapes=[pltpu.SemaphoreType.DMA(())],
        input_output_aliases={2: 0},
        compiler_params=_cparams(("arbitrary",)),
        name="moe_scatter",
    )(pos_flat, h2, xs_zero)


GU_GROUP = 2 * LANES


def _moe_wprep_kernel(w_ref, perm_ref, o_ref):
    w = w_ref[0].astype(BF16)
    for c in range(w.shape[1] // GU_GROUP):
        sl = slice(c * GU_GROUP, (c + 1) * GU_GROUP)
        o_ref[0, :, sl] = _dot(w[:, sl], perm_ref[...]).astype(BF16)


def _moe_wprep(w_gate_up, perm):
    e, d, width = w_gate_up.shape
    tk = 512
    return pl.pallas_call(
        _moe_wprep_kernel,
        grid=(e, d // tk),
        in_specs=[pl.BlockSpec((1, tk, width), lambda i, j: (i, j, 0)),
                  pl.BlockSpec(perm.shape, lambda i, j: (0, 0))],
        out_specs=pl.BlockSpec((1, tk, width), lambda i, j: (i, j, 0)),
        out_shape=jax.ShapeDtypeStruct((e, d, width), BF16),
        compiler_params=_cparams(("parallel", "parallel")),
        name="moe_wprep",
    )(w_gate_up, perm)


def _moe_ffn_kernel(blk_e_ref, n_used_ref, x_ref, wgu_ref, bg_ref, bu_ref, wd_ref, bd_ref, y_ref):
    i = pl.program_id(0)

    @pl.when(i < n_used_ref[0])
    def _():
        x = x_ref[...].astype(BF16)
        gu = _dot(x, wgu_ref[0])
        bg, bu = bg_ref[0], bu_ref[0]
        acts = []
        for c in range(gu.shape[1] // GU_GROUP):
            fs = slice(c * LANES, (c + 1) * LANES)
            gate = jnp.minimum(gu[:, c * GU_GROUP:c * GU_GROUP + LANES] + bg[:, fs], SWIGLU_LIMIT)
            up = jnp.clip(gu[:, c * GU_GROUP + LANES:(c + 1) * GU_GROUP] + bu[:, fs], -SWIGLU_LIMIT, SWIGLU_LIMIT)
            acts.append(((up + 1.0) * gate * jax.nn.sigmoid(SWIGLU_ALPHA * gate)).astype(BF16))
        act = jnp.concatenate(acts, axis=1)
        y_ref[...] = _dot(act, wd_ref[0].astype(BF16)) + bd_ref[0]

    @pl.when(i >= n_used_ref[0])
    def _():
        y_ref[...] = jnp.zeros(y_ref.shape, F32)


def _moe_ffn(blk_e, n_used, xs, wgu, bg, bu, wd, bd):
    p, d = xs.shape
    dff = wd.shape[1]
    n_blk = p // MOE_BLK
    w_spec = lambda r, c: pl.BlockSpec((1, r, c), lambda i, be, nu: (be[i], 0, 0))
    grid_spec = pltpu.PrefetchScalarGridSpec(
        num_scalar_prefetch=2,
        grid=(n_blk,),
        in_specs=[pl.BlockSpec((MOE_BLK, d), lambda i, be, nu: (i, 0)),
                  w_spec(d, 2 * dff), w_spec(1, dff), w_spec(1, dff),
                  w_spec(dff, d), w_spec(1, d)],
        out_specs=pl.BlockSpec((MOE_BLK, d), lambda i, be, nu: (i, 0)),
    )
    return pl.pallas_call(
        _moe_ffn_kernel,
        grid_spec=grid_spec,
        out_shape=jax.ShapeDtypeStruct((p, d), F32),
        compiler_params=_cparams(("arbitrary",)),
        name="moe_ffn",
    )(blk_e, n_used, xs, wgu, bg, bu, wd, bd)


def _moe_combine_kernel(pos_ref, gate_ref, x_ref, ys_ref, o_ref, buf_ref, sem, *, tm):
    def copy(t, k):
        return pltpu.make_async_copy(ys_ref.at[pl.ds(pos_ref[t * TOP_K + k], 1)],
                                     buf_ref.at[k, pl.ds(t, 1)], sem)

    def issue(t, c):
        for k in range(TOP_K):
            copy(t, k).start()
        return c

    def drain(t, c):
        for k in range(TOP_K):
            copy(t, k).wait()
        return c

    lax.fori_loop(0, tm, issue, 0)
    lax.fori_loop(0, tm, drain, 0)
    gates = gate_ref[...]
    acc = x_ref[...]
    for k in range(TOP_K):
        acc = acc + gates[:, k:k + 1] * buf_ref[k]
    o_ref[...] = acc


def _moe_combine(pos_flat, gates, x2, ys):
    n, d = x2.shape
    tm = 256
    kern = functools.partial(_moe_combine_kernel, tm=tm)
    return pl.pallas_call(
        kern,
        grid=(n // tm,),
        in_specs=[pl.BlockSpec((tm * TOP_K,), lambda i: (i,), memory_space=pltpu.SMEM),
                  pl.BlockSpec((tm, LANES), lambda i: (i, 0)),
                  pl.BlockSpec((tm, d), lambda i: (i, 0)),
                  pl.BlockSpec(memory_space=pl.ANY)],
        out_specs=pl.BlockSpec((tm, d), lambda i: (i, 0)),
        out_shape=jax.ShapeDtypeStruct((n, d), F32),
        scratch_shapes=[pltpu.VMEM((TOP_K, tm, d), F32), pltpu.SemaphoreType.DMA(())],
        compiler_params=_cparams(("arbitrary",)),
        name="moe_combine",
    )(pos_flat, gates, x2, ys)


def _rel_bucket_np(dist):
    n = np.maximum(dist, 0)
    nf = np.maximum(n, 1).astype(np.float32)
    ratio = np.log(nf / np.float32(REL_MAX_EXACT)) / np.float32(math.log(REL_MAX_DIST / REL_MAX_EXACT))
    large = REL_MAX_EXACT + (ratio * np.float32(REL_BUCKETS - REL_MAX_EXACT)).astype(np.int32)
    large = np.minimum(large, REL_BUCKETS - 1)
    return np.where(n < REL_MAX_EXACT, n, large)


def _bias_tables_kernel(tbl_ref, bc_ref, bt_ref, ec_ref, tn_ref):
    h = pl.program_id(0)
    bc, bt = bc_ref[...], bt_ref[...]
    ec = jnp.zeros(bc.shape, F32)
    tn = jnp.zeros(bt.shape, F32)
    for b in range(REL_BUCKETS):
        v = tbl_ref[b, h]
        ec = jnp.where(bc == b, v, ec)
        tn = jnp.where(bt == b, v, tn)
    ec_ref[0] = ec
    tn_ref[0] = tn


def _bias_tables(rel_table, nc):
    q = np.arange(NSA_TQ)[:, None]
    c = np.arange(2 * nc)[None, :]
    b_cmp = _rel_bucket_np(q - (CMP_LEN - 1) - CMP_STRIDE * (c - nc)).astype(np.int32)
    c = np.arange(2 * NSA_TK + WIN)[None, :]
    b_tok = _rel_bucket_np(q + WIN - (c - NSA_TK)).astype(np.int32)
    heads = rel_table.shape[1]
    return pl.pallas_call(
        _bias_tables_kernel,
        grid=(heads,),
        in_specs=[pl.BlockSpec(memory_space=pltpu.SMEM),
                  pl.BlockSpec(b_cmp.shape, lambda h: (0, 0)),
                  pl.BlockSpec(b_tok.shape, lambda h: (0, 0))],
        out_specs=[pl.BlockSpec((1,) + b_cmp.shape, lambda h: (h, 0, 0)),
                   pl.BlockSpec((1,) + b_tok.shape, lambda h: (h, 0, 0))],
        out_shape=[jax.ShapeDtypeStruct((heads,) + b_cmp.shape, F32),
                   jax.ShapeDtypeStruct((heads,) + b_tok.shape, F32)],
        compiler_params=_cparams(("parallel",)),
        name="bias_tables",
    )(rel_table.astype(F32), jnp.asarray(b_cmp), jnp.asarray(b_tok))


def kernel(x, mem, g_attn_norm, w_in, g_cq, w_uq, g_ckv, w_ukv, g_q_mla, g_k_mla, cmp_k_pos, cmp_k_w1, cmp_k_w2, cmp_v_pos, cmp_v_w1, cmp_v_w2, g_q_nsa, g_k_nsa, rel_table, g_out_mla, g_out_nsa, w_out, g_mem_norm, g_mem_src, w_mq, w_mkv, g_mq, g_mk, w_mo, g_moe_norm, router_w, router_b, w_gate_up, b_gate_up, w_down, b_down):
    batch, seq, d = x.shape
    n = batch * seq
    depth = w_in.shape[0]
    assert seq % 512 == 0 and seq // SLC_LEN <= LANES
    row = lambda v: v.reshape(1, -1).astype(F32)
    tile2 = lambda v: jnp.concatenate([v, v]).reshape(1, -1).astype(F32)

    x2d = x.reshape(n, d)
    for l in range(depth):
        wi = w_in[l]
        w_in_r = jnp.concatenate(
            [wi[:, 0:384], wi[:, 416:928], wi[:, 928:1696], wi[:, 384:416], wi[:, 1696:1720],
             jnp.zeros((d, IN_COLS_PAD - 1720), wi.dtype)], axis=1).astype(BF16)
        dq = MLA_NOPE + MLA_ROPE
        wuq_r = jnp.pad(w_uq[l].reshape(MLA_Q_RANK, MLA_HEADS, dq), ((0, 0), (0, 0), (0, LANES - dq)))
        wuq_r = wuq_r.reshape(MLA_Q_RANK, MLA_HEADS * LANES).astype(BF16)
        wukv = w_ukv[l].reshape(MLA_KV_RANK, MLA_HEADS, MLA_NOPE + MLA_V)
        wuk_r = jnp.pad(wukv[:, :, :MLA_NOPE], ((0, 0), (0, 0), (0, LANES - MLA_NOPE)))
        wuk_r = wuk_r.reshape(MLA_KV_RANK, MLA_HEADS * LANES).astype(BF16)
        wuv_r = wukv[:, :, MLA_NOPE:].reshape(MLA_KV_RANK, MLA_HEADS * MLA_V).astype(BF16)
        gq_pad = jnp.pad(g_q_mla[l], (0, LANES - dq)).reshape(1, LANES)
        gk_pad = jnp.pad(g_k_mla[l], (0, LANES - dq)).reshape(1, LANES)

        half = MLA_ROPE // 2
        inv = ROPE_THETA ** (-jnp.arange(half, dtype=F32) / half)
        ang = jnp.arange(seq, dtype=F32)[:, None] * inv
        cos, sin = jnp.cos(ang), jnp.sin(ang)
        zeros = jnp.zeros((seq, half), F32)
        rc = jnp.concatenate([jnp.ones((seq, MLA_NOPE), F32), cos, cos, jnp.zeros((seq, LANES - dq), F32)], 1)
        rs1 = jnp.concatenate([jnp.zeros((seq, MLA_NOPE), F32), -sin, zeros, jnp.zeros((seq, LANES - dq), F32)], 1)
        rs2 = jnp.concatenate([jnp.zeros((seq, MLA_NOPE), F32), zeros, sin, jnp.zeros((seq, LANES - dq), F32)], 1)

        mla_in, qn, kv6, misc = _in_proj(x2d, row(g_attn_norm[l]), w_in_r)
        q_m, k_m, v_m = _mla_prep(mla_in, misc, row(g_cq[l]), wuq_r, row(g_ckv[l]), wuk_r, wuv_r,
                                  gq_pad, gk_pad, rc, rs1, rs2, seq)
        y_mla = _mla_attn(q_m, k_m, v_m, batch, seq)

        nc = seq // CMP_STRIDE
        half_len = CMP_LEN // 2
        eye_g = jnp.eye(NSA_KV_GROUPS, dtype=F32)

        def cmp_weights(pos, w1, w2):
            out = []
            for part in range(2):
                sl = slice(part * half_len, (part + 1) * half_len)
                out.append(jnp.broadcast_to(pos[sl][:, None, :], (half_len, NSA_KV_GROUPS, NSA_HD))
                           .reshape(1, -1))
            for part in range(2):
                sl = slice(part * half_len, (part + 1) * half_len)
                wexp = jnp.einsum('ldf,gh->lgdhf', w1[sl], eye_g)
                out.append(wexp.reshape(half_len * NSA_KV_GROUPS * NSA_HD, NSA_KV_GROUPS * CMP_HIDDEN).astype(BF16))
            out.append(w2.astype(BF16))
            return out

        chunk_w = CMP_STRIDE * NSA_KV_GROUPS * NSA_HD
        kcmp, vcmp = _nsa_cmp(kv6[0].reshape(batch, nc, chunk_w), kv6[1].reshape(batch, nc, chunk_w),
                              cmp_weights(cmp_k_pos[l], cmp_k_w1[l], cmp_k_w2[l]),
                              cmp_weights(cmp_v_pos[l], cmp_v_w1[l], cmp_v_w2[l]), row(g_k_nsa[l]))
        q_n, ks, vs, kw, vw, gates_n = _nsa_prep(qn, kv6, misc, tile2(g_q_nsa[l]), tile2(g_k_nsa[l]))
        by_batch = lambda t: t.reshape(NSA_KV_GROUPS, batch, seq, NSA_HD)
        front_pad = lambda t: jnp.pad(by_batch(t), ((0, 0), (0, 0), (WIN, 0), (0, 0)))
        ecmp, tn = _bias_tables(rel_table, nc)
        n_idx = np.arange(nc)[:, None]
        j_idx = np.arange(LANES)[None, :]
        ovl = ((CMP_STRIDE * n_idx < SLC_LEN * j_idx + SLC_LEN)
               & (CMP_STRIDE * n_idx + CMP_LEN - 1 >= SLC_LEN * j_idx)
               & (n_idx < nc - 1) & (j_idx < seq // SLC_LEN)).astype(np.float32)
        efull = (np.arange(seq)[None, :] // SLC_LEN == np.arange(LANES)[:, None]).astype(np.float32)
        y_nsa = _nsa_attn(q_n, kcmp, vcmp, by_batch(ks), by_batch(vs), front_pad(kw), front_pad(vw), gates_n,
                          ecmp, tn, jnp.asarray(ovl), jnp.asarray(efull, dtype=BF16), batch, seq)

        k_mem, v_mem = _mem_kv(mem, row(g_mem_src[l]), w_mkv[l].reshape(d, 2 * MEM_HEADS * MEM_HD).astype(BF16),
                               tile2(g_mk[l]))
        rw = jnp.pad(router_w[l], ((0, 0), (0, LANES - N_EXPERTS)))
        rb = jnp.pad(router_b[l], (0, LANES - N_EXPERTS), constant_values=NEG_INF).reshape(1, LANES)
        wo = w_out[l].astype(BF16)
        n_mla = MLA_HEADS * MLA_V
        x2, h2, topi, gates_e, counts = _post_attn(
            x2d, y_mla, y_nsa, row(g_out_mla[l]), row(g_out_nsa[l]), wo[:n_mla], wo[n_mla:],
            row(g_mem_norm[l]), w_mq[l].astype(BF16), tile2(g_mq[l]), k_mem, v_mem, w_mo[l].astype(BF16),
            row(g_moe_norm[l]), rw, rb, batch, seq)

        cnt = counts[0].astype(jnp.int32)
        padded = (cnt + MOE_BLK - 1) // MOE_BLK * MOE_BLK
        pad_end = jnp.cumsum(padded)
        pad_start = (pad_end - padded).astype(F32).reshape(1, LANES)
        p_rows = (n * TOP_K // MOE_BLK + N_EXPERTS) * MOE_BLK
        n_blk = p_rows // MOE_BLK
        blk_first_row = jnp.arange(n_blk, dtype=jnp.int32) * MOE_BLK
        blk_e = jnp.minimum(jnp.sum(pad_end[None, :N_EXPERTS] <= blk_first_row[:, None], axis=1),
                            N_EXPERTS - 1).astype(jnp.int32)
        n_used = (pad_end[N_EXPERTS - 1] // MOE_BLK).astype(jnp.int32).reshape(1)
        tm_pos = 256
        tri = (np.arange(tm_pos)[None, :] < np.arange(tm_pos)[:, None]).astype(np.float32)
        pos = _moe_pos(topi, pad_start, jnp.asarray(tri, dtype=BF16))
        pos_flat = pos[:, :TOP_K].reshape(n * TOP_K)
        xs = _moe_scatter(pos_flat, h2, jnp.zeros((p_rows, d), F32))
        src = np.arange(GU_GROUP)
        perm = (np.arange(GU_GROUP)[:, None] == np.where(src < LANES, 2 * src, 2 * (src - LANES) + 1)[None, :])
        wgu = _moe_wprep(w_gate_up[l], jnp.asarray(perm.astype(np.float32), dtype=BF16))
        bgu = b_gate_up[l]
        ys = _moe_ffn(blk_e, n_used, xs, wgu, bgu[:, None, 0::2], bgu[:, None, 1::2],
                      w_down[l], b_down[l][:, None, :])
        x2d = _moe_combine(pos_flat, gates_e, x2, ys)
    return x2d.reshape(batch, seq, d)
```

```python
import functools
import math

import numpy as np
import jax
import jax.numpy as jnp
from jax import lax
from jax.experimental import pallas as pl
from jax.experimental.pallas import tpu as pltpu

F32 = jnp.float32
BF16 = jnp.bfloat16

EPS = 1e-6
NEG_INF = -1e30
LANES = 128

MLA_HEADS = 8
MLA_NOPE = 64
MLA_ROPE = 32
MLA_V = 64
MLA_Q_RANK = 256
MLA_KV_RANK = 128
ROPE_THETA = 10000.0

NSA_HEADS = 8
NSA_KV_GROUPS = 2
NSA_HPG = NSA_HEADS // NSA_KV_GROUPS
NSA_HD = 64
CMP_LEN = 32
CMP_STRIDE = 16
CMP_HIDDEN = 128
SLC_LEN = 64
SLC_TOPK = 16
WIN = 512
FORCE_SCORE = 1e9
NSA_TQ = 256
NSA_TQS = 512
NSA_TK = 512
CMP_WINDOW_BACK = 40

REL_BUCKETS = 32
REL_MAX_EXACT = 16
REL_MAX_DIST = 512

MEM_HEADS = 4
MEM_HD = 64

N_EXPERTS = 32
TOP_K = 4
SWIGLU_LIMIT = 7.0
SWIGLU_ALPHA = 1.702
MOE_BLK = 256

LOG2E = math.log2(math.e)
MASK_BIG = 2.0 ** 100

VMEM_LIMIT = 56 * 1024 * 1024


def _cparams(sem, vmem=VMEM_LIMIT):
    return pltpu.CompilerParams(dimension_semantics=sem, vmem_limit_bytes=vmem)


def _rms(x, g):
    return x * lax.rsqrt(jnp.mean(x * x, axis=-1, keepdims=True) + EPS) * g


def _seg_rms64(t, g2):
    lane = lax.broadcasted_iota(jnp.int32, t.shape, 1)
    sq = t * t
    lo = jnp.sum(jnp.where(lane < 64, sq, 0.0), axis=-1, keepdims=True)
    hi = jnp.sum(jnp.where(lane >= 64, sq, 0.0), axis=-1, keepdims=True)
    ms = jnp.where(lane < 64, lo, hi) * (1.0 / 64.0)
    return t * lax.rsqrt(ms + EPS) * g2


def _dot(a, b):
    return jnp.dot(a, b, preferred_element_type=F32)


def _with_ones(v):
    lane = lax.broadcasted_iota(jnp.int32, v.shape, 1)
    return jnp.concatenate([v, jnp.where(lane == 0, 1.0, 0.0).astype(v.dtype)], axis=1)


def _dot_nt(a, b):
    return lax.dot_general(a, b, (((1,), (1,)), ((), ())), preferred_element_type=F32)


def _softmax_weights(s, m_old):
    cols = [s[:, c * LANES:(c + 1) * LANES] for c in range(s.shape[1] // LANES)]
    m_new = jnp.maximum(m_old, jnp.max(functools.reduce(jnp.maximum, cols), axis=-1, keepdims=True))
    return m_new, jnp.concatenate([jnp.exp2(c - m_new).astype(BF16) for c in cols], axis=1)


IN_COLS_PAD = 1792


def _in_proj_kernel(x_ref, g_ref, w_ref, mla_ref, qn_ref, kv_ref, misc_ref):
    h = _rms(x_ref[...], g_ref[...])
    p = _dot(h.astype(BF16), w_ref[...])
    mla_ref[...] = p[:, 0:384]
    qn_ref[...] = p[:, 384:896]
    for j in range(6):
        kv_ref[j] = p[:, 896 + 128 * j:1024 + 128 * j]
    misc_ref[...] = p[:, 1664:1792]


def _in_proj(x2d, g, w):
    n, d = x2d.shape
    tm = 512
    return pl.pallas_call(
        _in_proj_kernel,
        grid=(n // tm,),
        in_specs=[pl.BlockSpec((tm, d), lambda i: (i, 0)),
                  pl.BlockSpec((1, d), lambda i: (0, 0)),
                  pl.BlockSpec((d, IN_COLS_PAD), lambda i: (0, 0))],
        out_specs=[pl.BlockSpec((tm, 384), lambda i: (i, 0)),
                   pl.BlockSpec((tm, 512), lambda i: (i, 0)),
                   pl.BlockSpec((6, tm, 128), lambda i: (0, i, 0)),
                   pl.BlockSpec((tm, 128), lambda i: (i, 0))],
        out_shape=[jax.ShapeDtypeStruct((n, 384), F32),
                   jax.ShapeDtypeStruct((n, 512), F32),
                   jax.ShapeDtypeStruct((6, n, 128), F32),
                   jax.ShapeDtypeStruct((n, 128), F32)],
        compiler_params=_cparams(("parallel",)),
        name="in_proj",
    )(x2d, g, w)


def _mla_prep_kernel(mla_ref, misc_ref, gcq_ref, wuq_ref, gckv_ref, wuk_ref, wuv_ref, gq_ref, gk_ref,
                     rc_ref, rs1_ref, rs2_ref, q_ref, k_ref, v_ref):
    p = mla_ref[...]
    cqn = _rms(p[:, 0:MLA_Q_RANK], gcq_ref[...]).astype(BF16)
    ckvn = _rms(p[:, MLA_Q_RANK:MLA_Q_RANK + MLA_KV_RANK], gckv_ref[...]).astype(BF16)
    qall = _dot(cqn, wuq_ref[...])
    kall = _dot(ckvn, wuk_ref[...])
    vall = _dot(ckvn, wuv_ref[...])
    misc = misc_ref[...]
    lane = lax.broadcasted_iota(jnp.int32, misc.shape, 1)
    in_rope = (lane >= MLA_NOPE) & (lane < MLA_NOPE + MLA_ROPE)
    krope = jnp.where(in_rope, pltpu.roll(misc, MLA_NOPE, axis=1), 0.0)
    rc, rs1, rs2 = rc_ref[...], rs1_ref[...], rs2_ref[...]
    gq, gk = gq_ref[...], gk_ref[...]
    inv_dk = 1.0 / (MLA_NOPE + MLA_ROPE)
    scale = (MLA_NOPE + MLA_ROPE) ** -0.5 * LOG2E

    def norm_rope(t, g):
        t = t * lax.rsqrt(jnp.sum(t * t, axis=-1, keepdims=True) * inv_dk + EPS) * g
        half = MLA_ROPE // 2
        return t * rc + pltpu.roll(t, LANES - half, axis=1) * rs1 + pltpu.roll(t, half, axis=1) * rs2

    for h in range(MLA_HEADS):
        qh = norm_rope(qall[:, h * LANES:(h + 1) * LANES], gq)
        q_ref[h] = (qh * scale).astype(BF16)
        kh = norm_rope(kall[:, h * LANES:(h + 1) * LANES] + krope, gk)
        k_ref[h] = kh.astype(BF16)
        v_ref[h] = _with_ones(vall[:, h * MLA_V:(h + 1) * MLA_V]).astype(BF16)


def _mla_prep(mla, misc, gcq, wuq, gckv, wuk, wuv, gq, gk, rc, rs1, rs2, seq):
    n = mla.shape[0]
    tm = 256
    ns = seq // tm
    full = lambda a: pl.BlockSpec(a.shape, lambda i: (0,) * a.ndim)
    rope_spec = pl.BlockSpec((tm, LANES), lambda i: (i % ns, 0))
    head_spec = pl.BlockSpec((MLA_HEADS, tm, LANES), lambda i: (0, i, 0))
    head_sds = jax.ShapeDtypeStruct((MLA_HEADS, n, LANES), BF16)
    return pl.pallas_call(
        _mla_prep_kernel,
        grid=(n // tm,),
        in_specs=[pl.BlockSpec((tm, 384), lambda i: (i, 0)),
                  pl.BlockSpec((tm, LANES), lambda i: (i, 0)),
                  full(gcq), full(wuq), full(gckv), full(wuk), full(wuv), full(gq), full(gk),
                  rope_spec, rope_spec, rope_spec],
        out_specs=[head_spec, head_spec, head_spec],
        out_shape=[head_sds, head_sds, head_sds],
        compiler_params=_cparams(("parallel",)),
        name="mla_prep",
    )(mla, misc, gcq, wuq, gckv, wuk, wuv, gq, gk, rc, rs1, rs2)


def _mla_attn_kernel(q_ref, k_ref, v_ref, o_ref, m_ref, acc_ref):
    qi, ki = pl.program_id(1), pl.program_id(2)

    @pl.when(ki == 0)
    def _():
        m_ref[...] = jnp.full(m_ref.shape, NEG_INF, F32)
        acc_ref[...] = jnp.zeros(acc_ref.shape, F32)

    def step(masked):
        def head(h, carry):
            s = _dot_nt(q_ref[h], k_ref[h])
            if masked:
                row = lax.broadcasted_iota(jnp.int32, s.shape, 0)
                col = lax.broadcasted_iota(jnp.int32, s.shape, 1)
                s = jnp.where(col <= row, s, -jnp.inf)
            m_old = m_ref[h]
            m_new, p = _softmax_weights(s, m_old)
            acc_ref[h] = jnp.exp2(m_old - m_new) * acc_ref[h] + _dot(p, v_ref[h])
            m_ref[h] = m_new
            return carry
        lax.fori_loop(0, MLA_HEADS, head, 0, unroll=4)

    @pl.when(ki < qi)
    def _():
        step(False)

    @pl.when(ki == qi)
    def _():
        step(True)

    @pl.when(ki == pl.num_programs(2) - 1)
    def _():
        for h in range(MLA_HEADS):
            acc = acc_ref[h]
            o_ref[:, h * MLA_V:(h + 1) * MLA_V] = acc[:, :MLA_V] / acc[:, MLA_V:MLA_V + 1]


def _mla_attn(q, k, v, batch, seq):
    n = q.shape[1]
    tq = tk = 512
    nq = seq // tq
    kv_spec = pl.BlockSpec((MLA_HEADS, tk, LANES), lambda b, i, j: (0, b * nq + jnp.minimum(i, j), 0))
    return pl.pallas_call(
        _mla_attn_kernel,
        grid=(batch, nq, nq),
        in_specs=[pl.BlockSpec((MLA_HEADS, tq, LANES), lambda b, i, j: (0, b * nq + i, 0)), kv_spec, kv_spec],
        out_specs=pl.BlockSpec((tq, MLA_HEADS * MLA_V), lambda b, i, j: (b * nq + i, 0)),
        out_shape=jax.ShapeDtypeStruct((n, MLA_HEADS * MLA_V), F32),
        scratch_shapes=[pltpu.VMEM((MLA_HEADS, tq, LANES), F32),
                        pltpu.VMEM((MLA_HEADS, tq, LANES), F32)],
        compiler_params=_cparams(("parallel", "parallel", "arbitrary")),
        name="mla_attn",
    )(q, k, v)


def _nsa_cmp_kernel(kc_ref, vc_ref, pak_ref, pbk_ref, wak_ref, wbk_ref, w2k_ref,
                    pav_ref, pbv_ref, wav_ref, wbv_ref, w2v_ref, gk_ref, kout_ref, vout_ref):
    def compress(chunks, pa, pb, wa, wb, w2):
        nc = chunks.shape[0]
        ha = _dot((chunks + pa).astype(BF16), wa)
        hb = _dot((chunks + pb).astype(BF16), wb)
        hid = jax.nn.gelu(ha + pltpu.roll(hb, nc - 1, axis=0))
        return [_dot(hid[:, g * CMP_HIDDEN:(g + 1) * CMP_HIDDEN].astype(BF16), w2)
                for g in range(NSA_KV_GROUPS)]

    kc = compress(kc_ref[0], pak_ref[...], pbk_ref[...], wak_ref[...], wbk_ref[...], w2k_ref[...])
    vc = compress(vc_ref[0], pav_ref[...], pbv_ref[...], wav_ref[...], wbv_ref[...], w2v_ref[...])
    for g in range(NSA_KV_GROUPS):
        kn = _rms(kc[g], gk_ref[...])
        kout_ref[0, g] = jnp.concatenate([kn, jnp.zeros_like(kn)], axis=1).astype(BF16)
        vout_ref[0, g] = vc[g].astype(BF16)


def _nsa_cmp(kc_chunks, vc_chunks, wk, wv, gk):
    b, nc, width = kc_chunks.shape
    full = lambda a: pl.BlockSpec(a.shape, lambda i: (0,) * a.ndim)
    chunk_spec = pl.BlockSpec((1, nc, width), lambda i: (i, 0, 0))
    out_spec = lambda width: pl.BlockSpec((1, NSA_KV_GROUPS, nc, width), lambda i: (i, 0, 0, 0))
    out_sds = lambda width: jax.ShapeDtypeStruct((b, NSA_KV_GROUPS, nc, width), BF16)
    return pl.pallas_call(
        _nsa_cmp_kernel,
        grid=(b,),
        in_specs=[chunk_spec, chunk_spec] + [full(a) for a in wk] + [full(a) for a in wv] + [full(gk)],
        out_specs=[out_spec(LANES), out_spec(NSA_HD)],
        out_shape=[out_sds(LANES), out_sds(NSA_HD)],
        compiler_params=_cparams(("parallel",)),
        name="nsa_cmp",
    )(kc_chunks, vc_chunks, *wk, *wv, gk)


def _nsa_prep_kernel(qn_ref, ks_ref, vs_ref, kw_ref, vw_ref, misc_ref, gq_ref, gk_ref,
                     q_ref, kso_ref, vso_ref, kwo_ref, vwo_ref, gate_ref, *, seq):
    gq2, gk2 = gq_ref[...], gk_ref[...]
    tm = qn_ref.shape[0]
    scale = NSA_HD ** -0.5 * LOG2E
    zpad = jnp.zeros((tm, NSA_HD), F32)
    for c in range(NSA_HEADS // 2):
        t = _seg_rms64(qn_ref[:, c * LANES:(c + 1) * LANES], gq2) * scale
        q_ref[2 * c] = jnp.concatenate([t[:, :NSA_HD], zpad], axis=1).astype(BF16)
        q_ref[2 * c + 1] = jnp.concatenate([t[:, NSA_HD:], zpad], axis=1).astype(BF16)
    ksn = _seg_rms64(ks_ref[0], gk2)
    kwn = _seg_rms64(kw_ref[0], gk2)
    vs, vw = vs_ref[0], vw_ref[0]
    row = lax.broadcasted_iota(jnp.int32, (tm, LANES), 0)
    lane = lax.broadcasted_iota(jnp.int32, (tm, LANES), 1)
    pos = (pl.program_id(0) * tm) % seq + row
    neg_onehot = jnp.where(lane == pos // SLC_LEN, -1.0, 0.0)
    for g in range(NSA_KV_GROUPS):
        sl = slice(g * NSA_HD, (g + 1) * NSA_HD)
        kso_ref[g] = jnp.concatenate([neg_onehot, ksn[:, sl], zpad], axis=1).astype(BF16)
        kwo_ref[g] = jnp.concatenate([kwn[:, sl], zpad], axis=1).astype(BF16)
        vso_ref[g] = _with_ones(vs[:, sl]).astype(BF16)
        vwo_ref[g] = _with_ones(vw[:, sl]).astype(BF16)
    sig = jax.nn.sigmoid(misc_ref[...])
    per_group = 3 * NSA_HPG
    for g in range(NSA_KV_GROUPS):
        gate_ref[g] = pltpu.roll(sig, LANES - (MLA_ROPE + per_group * g), axis=1)


def _nsa_prep(qn, kv, misc, gq2, gk2, seq):
    n = qn.shape[0]
    tm = 512
    full = lambda a: pl.BlockSpec(a.shape, lambda i: (0,) * a.ndim)
    kv_spec = lambda j: pl.BlockSpec((1, tm, LANES), lambda i, j=j: (j, i, 0))
    g_spec = lambda width: pl.BlockSpec((NSA_KV_GROUPS, tm, width), lambda i: (0, i, 0))
    g_sds = lambda width: jax.ShapeDtypeStruct((NSA_KV_GROUPS, n, width), BF16)
    return pl.pallas_call(
        functools.partial(_nsa_prep_kernel, seq=seq),
        grid=(n // tm,),
        in_specs=[pl.BlockSpec((tm, 512), lambda i: (i, 0)),
                  kv_spec(2), kv_spec(3), kv_spec(4), kv_spec(5),
                  pl.BlockSpec((tm, LANES), lambda i: (i, 0)), full(gq2), full(gk2)],
        out_specs=[pl.BlockSpec((NSA_HEADS, tm, LANES), lambda i: (0, i, 0)),
                   g_spec(2 * LANES), g_spec(LANES), g_spec(LANES), g_spec(LANES),
                   pl.BlockSpec((NSA_KV_GROUPS, tm, LANES), lambda i: (0, i, 0))],
        out_shape=[jax.ShapeDtypeStruct((NSA_HEADS, n, LANES), BF16),
                   g_sds(2 * LANES), g_sds(LANES), g_sds(LANES), g_sds(LANES),
                   jax.ShapeDtypeStruct((NSA_KV_GROUPS, n, LANES), F32)],
        compiler_params=_cparams(("parallel",)),
        name="nsa_prep",
    )(qn, kv, kv, kv, kv, misc, gq2, gk2)


def _nsa_select_kernel(q_ref, kc_ref, vc_ref, tc_ref, ovl_ref, ocmp_ref, sel_ref, s_ref, *, nc, n_sel):
    tq, hpg = NSA_TQS, NSA_HPG
    i = pl.program_id(2)
    q0 = i * tq
    ncb = nc // LANES
    s_ref[...] = _dot_nt(q_ref[...].reshape(hpg * tq, LANES), kc_ref[0, 0])

    first_blk = ((tq // CMP_STRIDE) * i + LANES - CMP_WINDOW_BACK) // LANES - 1
    for half in range(2):
        blk = first_blk + half

        @pl.when((blk >= 0) & (blk < ncb))
        def _():
            c0 = pl.multiple_of(blk * LANES, LANES)
            for h in range(hpg):
                s_ref[h * tq:(h + 1) * tq, pl.ds(c0, LANES)] += tc_ref[0, h, :, half * LANES:(half + 1) * LANES]

    row = lax.broadcasted_iota(jnp.int32, (tq, LANES), 0)
    lane = lax.broadcasted_iota(jnp.int32, (tq, LANES), 1)
    qpos = q0 + row
    valid = [(CMP_STRIDE * (c * LANES + lane) + (CMP_LEN - 1)) <= qpos for c in range(ncb)]
    psum = [jnp.zeros((tq, LANES), F32) for _ in range(ncb)]
    vc = vc_ref[0, 0]
    for h in range(hpg):
        rs = slice(h * tq, (h + 1) * tq)
        cols = [jnp.where(valid[c], s_ref[rs, c * LANES:(c + 1) * LANES], -jnp.inf) for c in range(ncb)]
        m = jnp.maximum(jnp.max(functools.reduce(jnp.maximum, cols), axis=-1, keepdims=True), NEG_INF)
        es = [jnp.exp2(c - m) for c in cols]
        inv = 1.0 / jnp.maximum(jnp.sum(functools.reduce(jnp.add, es), axis=-1, keepdims=True), 1e-30)
        ps = [e * inv for e in es]
        psum = [a + p for a, p in zip(psum, ps)]
        ocmp_ref[:, h * NSA_HD:(h + 1) * NSA_HD] = _dot(
            jnp.concatenate([p.astype(BF16) for p in ps], axis=1), vc)

    pcat = jnp.concatenate(psum, axis=1)
    hi = pcat.astype(BF16)
    rest = pcat - hi.astype(F32)
    mid = rest.astype(BF16)
    lo = (rest - mid.astype(F32)).astype(BF16)
    ovl = ovl_ref[...]
    imp = _dot(hi, ovl) + _dot(mid, ovl) + _dot(lo, ovl)

    jf = lane.astype(F32)
    cur = qpos // SLC_LEN
    forced = (lane == 0) | (lane == cur) | (lane == cur - 1)
    imp = jnp.where(forced, FORCE_SCORE, imp)
    imp = jnp.where(lane * SLC_LEN <= qpos, imp, NEG_INF)
    for _ in range(n_sel):
        best = jnp.max(imp, axis=-1, keepdims=True)
        first = jnp.min(jnp.where(imp == best, jf, float(LANES)), axis=-1, keepdims=True)
        imp = jnp.where(jf == first, -jnp.inf, imp)
    sel_ref[0] = jnp.where(imp == -jnp.inf, 0.0, MASK_BIG).astype(BF16)


def _nsa_select(q, kcmp, vcmp, tc, ovl, batch, seq):
    n = q.shape[1]
    nc = kcmp.shape[2]
    nq = seq // NSA_TQS
    n_res = tc.shape[0]
    n_sel = min(SLC_TOPK, seq // SLC_LEN)
    g_, hpg = NSA_KV_GROUPS, NSA_HPG
    kern = functools.partial(_nsa_select_kernel, nc=nc, n_sel=n_sel)
    cmp_spec = lambda width: pl.BlockSpec((1, 1, nc, width), lambda b, g, i: (b, g, 0, 0))
    return pl.pallas_call(
        kern,
        grid=(batch, g_, nq),
        in_specs=[pl.BlockSpec((hpg, NSA_TQS, LANES), lambda b, g, i: (g, b * nq + i, 0)),
                  cmp_spec(LANES), cmp_spec(NSA_HD),
                  pl.BlockSpec((1, hpg, NSA_TQS, 2 * LANES), lambda b, g, i: (i % n_res, g, 0, 0)),
                  pl.BlockSpec(ovl.shape, lambda b, g, i: (0, 0))],
        out_specs=[pl.BlockSpec((NSA_TQS, hpg * NSA_HD), lambda b, g, i: (b * nq + i, g)),
                   pl.BlockSpec((1, NSA_TQS, LANES), lambda b, g, i: (g, b * nq + i, 0))],
        out_shape=[jax.ShapeDtypeStruct((n, NSA_HEADS * NSA_HD), F32),
                   jax.ShapeDtypeStruct((g_, n, LANES), BF16)],
        scratch_shapes=[pltpu.VMEM((hpg * NSA_TQS, nc), F32)],
        compiler_params=_cparams(("parallel", "parallel", "arbitrary")),
        name="nsa_select",
    )(q, kcmp, vcmp, tc, ovl)


def _nsa_attn_kernel(q_ref, ks_ref, vs_ref, kw_ref, vw_ref, sel_ref, tn_ref, gate_ref, ocmp_ref,
                     o_ref, m_ref, acc_ref):
    tq, tk, hpg = NSA_TQ, NSA_TK, NSA_HPG
    rows = hpg * tq
    i = pl.program_id(2)
    q0 = i * tq
    q4 = q_ref[...].reshape(rows, LANES)
    q_aug = jnp.concatenate([jnp.concatenate([sel_ref[0]] * hpg, axis=0), q4], axis=1)
    row = lax.broadcasted_iota(jnp.int32, (tq, tk), 0)
    col = lax.broadcasted_iota(jnp.int32, (tq, tk), 1)
    m_ref[...] = jnp.full(m_ref.shape, NEG_INF, F32)
    acc_ref[...] = jnp.zeros(acc_ref.shape, F32)

    def tile(kt, near):
        k0 = pl.multiple_of(kt * tk, tk)
        s = _dot_nt(q_aug, ks_ref[0, 0, pl.ds(k0, tk), :])
        if near:
            causal = (k0 + col) <= (q0 + row)
            start = pl.multiple_of(k0 + tk + WIN - q0, LANES)
        ps, alphas = [], []
        for h in range(hpg):
            rs = slice(h * tq, (h + 1) * tq)
            sh = s[rs]
            if near:
                sh = jnp.where(causal, sh + tn_ref[h, :, pl.ds(start, tk)], -jnp.inf)
            m_old = m_ref[rs]
            m_new, p = _softmax_weights(sh, m_old)
            m_ref[rs] = m_new
            alphas.append(jnp.exp2(m_old - m_new))
            ps.append(p)
        acc_ref[...] = (jnp.concatenate(alphas, axis=0) * acc_ref[...]
                        + _dot(jnp.concatenate(ps, axis=0), vs_ref[0, 0, pl.ds(k0, tk), :]))

    last = q0 // tk
    n_far = jnp.maximum(last - 1, 0)
    def far_pair(pair, c):
        tile(2 * pair, False)
        tile(2 * pair + 1, False)
        return c

    lax.fori_loop(0, n_far // 2, far_pair, 0)
    lax.fori_loop(n_far // 2 * 2, n_far, lambda kt, c: (tile(kt, False), c)[1], 0)
    lax.fori_loop(n_far, last + 1, lambda kt, c: (tile(kt, True), c)[1], 0)

    wk = WIN + tq
    w0 = pl.multiple_of(q0, tq)
    s = _dot_nt(q4, kw_ref[0, 0, pl.ds(w0, wk), :])
    roww = lax.broadcasted_iota(jnp.int32, (tq, wk), 0)
    colw = lax.broadcasted_iota(jnp.int32, (tq, wk), 1)
    dist = roww + WIN - colw
    validw = (dist >= 0) & (dist < WIN) & (q0 - WIN + colw >= 0)
    ps = []
    for h in range(hpg):
        sh = s[h * tq:(h + 1) * tq] + tn_ref[h, :, tk:tk + wk]
        ps.append(_softmax_weights(jnp.where(validw, sh, -jnp.inf), NEG_INF)[1])
    accw = _dot(jnp.concatenate(ps, axis=0), vw_ref[0, 0, pl.ds(w0, wk), :])

    gates = gate_ref[0]
    acc = acc_ref[...]
    outs = []
    for h in range(hpg):
        rs = slice(h * tq, (h + 1) * tq)
        o_cmp = ocmp_ref[:, h * NSA_HD:(h + 1) * NSA_HD]
        o_slc = acc[rs, :NSA_HD] / jnp.maximum(acc[rs, NSA_HD:NSA_HD + 1], 1e-30)
        o_win = accw[rs, :NSA_HD] / accw[rs, NSA_HD:NSA_HD + 1]
        outs.append(gates[:, 3 * h:3 * h + 1] * o_cmp + gates[:, 3 * h + 1:3 * h + 2] * o_slc
                    + gates[:, 3 * h + 2:3 * h + 3] * o_win)
    o_ref[...] = jnp.concatenate(outs, axis=1)


def _nsa_attn(q, ks, vs, kwp, vwp, sel, tn, gates, ocmp, batch, seq):
    n = q.shape[1]
    nq = seq // NSA_TQ
    g_, hpg = NSA_KV_GROUPS, NSA_HPG
    seq_spec = lambda length, width: pl.BlockSpec((1, 1, length, width), lambda b, g, i: (g, b, 0, 0))
    tok_spec = pl.BlockSpec((1, NSA_TQ, LANES), lambda b, g, i: (g, b * nq + i, 0))
    out_spec = pl.BlockSpec((NSA_TQ, hpg * NSA_HD), lambda b, g, i: (b * nq + i, g))
    return pl.pallas_call(
        _nsa_attn_kernel,
        grid=(batch, g_, nq),
        in_specs=[pl.BlockSpec((hpg, NSA_TQ, LANES), lambda b, g, i: (g, b * nq + i, 0)),
                  seq_spec(seq, 2 * LANES), seq_spec(seq, LANES), seq_spec(seq + WIN, LANES), seq_spec(seq + WIN, LANES),
                  tok_spec,
                  pl.BlockSpec((hpg, NSA_TQ, 2 * NSA_TK + WIN), lambda b, g, i: (g, 0, 0)),
                  tok_spec, out_spec],
        out_specs=out_spec,
        out_shape=jax.ShapeDtypeStruct((n, NSA_HEADS * NSA_HD), F32),
        scratch_shapes=[pltpu.VMEM((hpg * NSA_TQ, LANES), F32), pltpu.VMEM((hpg * NSA_TQ, LANES), F32)],
        compiler_params=_cparams(("parallel", "parallel", "arbitrary")),
        name="nsa_attn",
    )(q, ks, vs, kwp, vwp, sel, tn, gates, ocmp)


def _mem_kv_kernel(mem_ref, g_ref, w_ref, gk_ref, k_ref, v_ref):
    mn = _rms(mem_ref[0], g_ref[...]).astype(BF16)
    kv = _dot(mn, w_ref[...])
    width = MEM_HEADS * MEM_HD
    for c in range(width // LANES):
        kn = _seg_rms64(kv[:, c * LANES:(c + 1) * LANES], gk_ref[...])
        vv = kv[:, width + c * LANES:width + (c + 1) * LANES]
        for half in range(2):
            sl = slice(half * MEM_HD, (half + 1) * MEM_HD)
            k_ref[0, 2 * c + half] = kn[:, sl].astype(BF16)
            v_ref[0, 2 * c + half] = vv[:, sl].astype(BF16)


def _mem_kv(mem, g, w, gk2):
    b, m, d = mem.shape
    full = lambda a: pl.BlockSpec(a.shape, lambda i: (0,) * a.ndim)
    spec = pl.BlockSpec((1, MEM_HEADS, m, MEM_HD), lambda i: (i, 0, 0, 0))
    sds = jax.ShapeDtypeStruct((b, MEM_HEADS, m, MEM_HD), BF16)
    return pl.pallas_call(
        _mem_kv_kernel,
        grid=(b,),
        in_specs=[pl.BlockSpec((1, m, d), lambda i: (i, 0, 0)), full(g), full(w), full(gk2)],
        out_specs=[spec, spec],
        out_shape=[sds, sds],
        compiler_params=_cparams(("parallel",)),
        name="mem_kv",
    )(mem, g, w, gk2)


def _post_attn_kernel(x_ref, ya_ref, yb_ref, goa_ref, gob_ref, woa_ref, wob_ref, gmn_ref, wmq_ref, gmq_ref,
                      km_ref, vm_ref, wmo_ref, gmoe_ref, rw_ref, rb_ref,
                      x2_ref, h2_ref, topi_ref, gate_ref, cnt_ref):
    first = (pl.program_id(0) == 0) & (pl.program_id(1) == 0)

    @pl.when(first)
    def _():
        cnt_ref[...] = jnp.zeros(cnt_ref.shape, F32)

    mixa = _rms(ya_ref[...], goa_ref[...]).astype(BF16)
    mixb = _rms(yb_ref[...], gob_ref[...]).astype(BF16)
    x1 = x_ref[...] + _dot(mixa, woa_ref[...]) + _dot(mixb, wob_ref[...])

    h = _rms(x1, gmn_ref[...]).astype(BF16)
    q = _dot(h, wmq_ref[...])
    scale = MEM_HD ** -0.5
    outs = []
    for c in range(MEM_HEADS * MEM_HD // LANES):
        qn = _seg_rms64(q[:, c * LANES:(c + 1) * LANES], gmq_ref[...]) * scale
        for half in range(2):
            hd = 2 * c + half
            qh = qn[:, half * MEM_HD:(half + 1) * MEM_HD].astype(BF16)
            s = _dot_nt(qh, km_ref[0, hd])
            e = jnp.exp(s - jnp.max(s, axis=-1, keepdims=True))
            p = e / jnp.sum(e, axis=-1, keepdims=True)
            outs.append(_dot(p.astype(BF16), vm_ref[0, hd]))
    o = jnp.concatenate(outs, axis=1).astype(BF16)
    x2 = x1 + _dot(o, wmo_ref[...])
    x2_ref[...] = x2

    h2 = _rms(x2, gmoe_ref[...])
    h2_ref[...] = h2
    logits = jnp.dot(h2, rw_ref[...], preferred_element_type=F32,
                     precision=lax.Precision.HIGHEST) + rb_ref[...]
    lane = lax.broadcasted_iota(jnp.int32, logits.shape, 1)
    topi = jnp.zeros(logits.shape, jnp.int32)
    topv = jnp.full(logits.shape, NEG_INF, F32)
    onehot = jnp.zeros(logits.shape, F32)
    for k in range(TOP_K):
        best = jnp.max(logits, axis=-1, keepdims=True)
        first_idx = jnp.min(jnp.where(logits == best, lane, LANES), axis=-1, keepdims=True)
        hit = lane == first_idx
        topi = jnp.where(lane == k, first_idx, topi)
        topv = jnp.where(lane == k, best, topv)
        onehot = jnp.where(hit, 1.0, onehot)
        logits = jnp.where(hit, -jnp.inf, logits)
    e = jnp.where(lane < TOP_K, jnp.exp(topv - jnp.max(topv, axis=-1, keepdims=True)), 0.0)
    gate_ref[...] = e / jnp.sum(e, axis=-1, keepdims=True)
    topi_ref[...] = topi
    cnt_ref[...] += jnp.sum(onehot, axis=0, keepdims=True)


def _post_attn(x2d, ya, yb, goa, gob, woa, wob, gmn, wmq, gmq2, km, vm, wmo, gmoe, rw, rb, batch, seq):
    n, d = x2d.shape
    tm = 256
    ns = seq // tm
    full = lambda a: pl.BlockSpec(a.shape, lambda b, i: (0,) * a.ndim)
    tok = lambda width: pl.BlockSpec((tm, width), lambda b, i: (b * ns + i, 0))
    mem_spec = pl.BlockSpec((1,) + km.shape[1:], lambda b, i: (b, 0, 0, 0))
    return pl.pallas_call(
        _post_attn_kernel,
        grid=(batch, ns),
        in_specs=[tok(d), tok(ya.shape[1]), tok(yb.shape[1]), full(goa), full(gob), full(woa), full(wob),
                  full(gmn), full(wmq), full(gmq2), mem_spec, mem_spec, full(wmo), full(gmoe), full(rw), full(rb)],
        out_specs=[tok(d), tok(d), tok(LANES), tok(LANES), pl.BlockSpec((1, LANES), lambda b, i: (0, 0))],
        out_shape=[jax.ShapeDtypeStruct((n, d), F32),
                   jax.ShapeDtypeStruct((n, d), F32),
                   jax.ShapeDtypeStruct((n, LANES), jnp.int32),
                   jax.ShapeDtypeStruct((n, LANES), F32),
                   jax.ShapeDtypeStruct((1, LANES), F32)],
        compiler_params=_cparams(("arbitrary", "arbitrary")),
        name="post_attn",
    )(x2d, ya, yb, goa, gob, woa, wob, gmn, wmq, gmq2, km, vm, wmo, gmoe, rw, rb)


def _moe_pos_kernel(topi_ref, start_ref, tri_ref, pos_ref, carry_ref):
    @pl.when(pl.program_id(0) == 0)
    def _():
        carry_ref[...] = jnp.zeros(carry_ref.shape, F32)

    topi = topi_ref[...]
    lane = lax.broadcasted_iota(jnp.int32, topi.shape, 1)
    hits = [lane == topi[:, k:k + 1] for k in range(TOP_K)]
    onehot = sum(h.astype(F32) for h in hits)
    before = _dot(tri_ref[...], onehot.astype(BF16))
    base = start_ref[...] + carry_ref[...] + before
    pos = jnp.zeros(topi.shape, jnp.int32)
    for k in range(TOP_K):
        pk = jnp.sum(jnp.where(hits[k], base, 0.0), axis=-1, keepdims=True).astype(jnp.int32)
        pos = jnp.where(lane == k, pk, pos)
    pos_ref[...] = pos
    carry_ref[...] += jnp.sum(onehot, axis=0, keepdims=True)


def _moe_pos(topi, pad_start, tri):
    n = topi.shape[0]
    tm = tri.shape[0]
    return pl.pallas_call(
        _moe_pos_kernel,
        grid=(n // tm,),
        in_specs=[pl.BlockSpec((tm, LANES), lambda i: (i, 0)),
                  pl.BlockSpec((1, LANES), lambda i: (0, 0)),
                  pl.BlockSpec((tm, tm), lambda i: (0, 0))],
        out_specs=pl.BlockSpec((tm, LANES), lambda i: (i, 0)),
        out_shape=jax.ShapeDtypeStruct((n, LANES), jnp.int32),
        scratch_shapes=[pltpu.VMEM((1, LANES), F32)],
        compiler_params=_cparams(("arbitrary",)),
        name="moe_pos",
    )(topi, pad_start, tri)


def _moe_scatter_kernel(pos_ref, h_ref, zero_ref, xs_ref, sem, *, tm):
    del zero_ref

    def copy(t, k):
        return pltpu.make_async_copy(h_ref.at[pl.ds(t, 1)],
                                     xs_ref.at[pl.ds(pos_ref[t * TOP_K + k], 1)], sem)

    def issue(t, c):
        for k in range(TOP_K):
            copy(t, k).start(priority=k % 2)
        return c

    def drain(t, c):
        for k in range(TOP_K):
            copy(t, k).wait()
        return c

    lax.fori_loop(0, tm, issue, 0)
    lax.fori_loop(0, tm, drain, 0)


def _moe_scatter(pos_flat, h2, xs_zero):
    n, d = h2.shape
    tm = 512
    kern = functools.partial(_moe_scatter_kernel, tm=tm)
    return pl.pallas_call(
        kern,
        grid=(n // tm,),
        in_specs=[pl.BlockSpec((tm * TOP_K,), lambda i: (i,), memory_space=pltpu.SMEM),
                  pl.BlockSpec((tm, d), lambda i: (i, 0)),
                  pl.BlockSpec(memory_space=pl.ANY)],
        out_specs=pl.BlockSpec(memory_space=pl.ANY),
        out_shape=jax.ShapeDtypeStruct(xs_zero.shape, xs_zero.dtype),
        scratch_shapes=[pltpu.SemaphoreType.DMA(())],
        input_output_aliases={2: 0},
        compiler_params=_cparams(("arbitrary",)),
        name="moe_scatter",
    )(pos_flat, h2, xs_zero)


GU_GROUP = 2 * LANES


def _moe_wprep_kernel(w_ref, perm_ref, o_ref):
    w = w_ref[0].astype(BF16)
    for c in range(w.shape[1] // GU_GROUP):
        sl = slice(c * GU_GROUP, (c + 1) * GU_GROUP)
        o_ref[0, :, sl] = _dot(w[:, sl], perm_ref[...]).astype(BF16)


def _moe_wprep(w_gate_up, perm):
    e, d, width = w_gate_up.shape
    tk = 512
    return pl.pallas_call(
        _moe_wprep_kernel,
        grid=(e, d // tk),
        in_specs=[pl.BlockSpec((1, tk, width), lambda i, j: (i, j, 0)),
                  pl.BlockSpec(perm.shape, lambda i, j: (0, 0))],
        out_specs=pl.BlockSpec((1, tk, width), lambda i, j: (i, j, 0)),
        out_shape=jax.ShapeDtypeStruct((e, d, width), BF16),
        compiler_params=_cparams(("parallel", "parallel")),
        name="moe_wprep",
    )(w_gate_up, perm)


def _moe_ffn_kernel(blk_e_ref, n_used_ref, x_ref, wgu_ref, bg_ref, bu_ref, wd_ref, bd_ref, y_ref):
    i = pl.program_id(0)

    @pl.when(i < n_used_ref[0])
    def _():
        x = x_ref[...].astype(BF16)
        gu = _dot(x, wgu_ref[0])
        bg, bu = bg_ref[0], bu_ref[0]
        acts = []
        for c in range(gu.shape[1] // GU_GROUP):
            fs = slice(c * LANES, (c + 1) * LANES)
            gate = jnp.minimum(gu[:, c * GU_GROUP:c * GU_GROUP + LANES] + bg[:, fs], SWIGLU_LIMIT)
            up = jnp.clip(gu[:, c * GU_GROUP + LANES:(c + 1) * GU_GROUP] + bu[:, fs], -SWIGLU_LIMIT, SWIGLU_LIMIT)
            acts.append(((up + 1.0) * gate * jax.nn.sigmoid(SWIGLU_ALPHA * gate)).astype(BF16))
        act = jnp.concatenate(acts, axis=1)
        y_ref[...] = _dot(act, wd_ref[0].astype(BF16)) + bd_ref[0]

    @pl.when(i >= n_used_ref[0])
    def _():
        y_ref[...] = jnp.zeros(y_ref.shape, F32)


def _moe_ffn(blk_e, n_used, xs, wgu, bg, bu, wd, bd):
    p, d = xs.shape
    dff = wd.shape[1]
    n_blk = p // MOE_BLK
    w_spec = lambda r, c: pl.BlockSpec((1, r, c), lambda i, be, nu: (be[i], 0, 0))
    grid_spec = pltpu.PrefetchScalarGridSpec(
        num_scalar_prefetch=2,
        grid=(n_blk,),
        in_specs=[pl.BlockSpec((MOE_BLK, d), lambda i, be, nu: (i, 0)),
                  w_spec(d, 2 * dff), w_spec(1, dff), w_spec(1, dff),
                  w_spec(dff, d), w_spec(1, d)],
        out_specs=pl.BlockSpec((MOE_BLK, d), lambda i, be, nu: (i, 0)),
    )
    return pl.pallas_call(
        _moe_ffn_kernel,
        grid_spec=grid_spec,
        out_shape=jax.ShapeDtypeStruct((p, d), F32),
        compiler_params=_cparams(("arbitrary",)),
        name="moe_ffn",
    )(blk_e, n_used, xs, wgu, bg, bu, wd, bd)


def _moe_combine_kernel(pos_ref, gate_ref, x_ref, ys_ref, o_ref, buf_ref, sem, *, tm):
    def copy(t, k):
        return pltpu.make_async_copy(ys_ref.at[pl.ds(pos_ref[t * TOP_K + k], 1)],
                                     buf_ref.at[k, pl.ds(t, 1)], sem)

    def issue(t, c):
        for k in range(TOP_K):
            copy(t, k).start(priority=k % 2)
        return c

    def drain(t, c):
        for k in range(TOP_K):
            copy(t, k).wait()
        return c

    lax.fori_loop(0, tm, issue, 0)
    lax.fori_loop(0, tm, drain, 0)
    gates = gate_ref[...]
    acc = x_ref[...]
    for k in range(TOP_K):
        acc = acc + gates[:, k:k + 1] * buf_ref[k]
    o_ref[...] = acc


def _moe_combine(pos_flat, gates, x2, ys):
    n, d = x2.shape
    tm = 256
    kern = functools.partial(_moe_combine_kernel, tm=tm)
    return pl.pallas_call(
        kern,
        grid=(n // tm,),
        in_specs=[pl.BlockSpec((tm * TOP_K,), lambda i: (i,), memory_space=pltpu.SMEM),
                  pl.BlockSpec((tm, LANES), lambda i: (i, 0)),
                  pl.BlockSpec((tm, d), lambda i: (i, 0)),
                  pl.BlockSpec(memory_space=pl.ANY)],
        out_specs=pl.BlockSpec((tm, d), lambda i: (i, 0)),
        out_shape=jax.ShapeDtypeStruct((n, d), F32),
        scratch_shapes=[pltpu.VMEM((TOP_K, tm, d), F32), pltpu.SemaphoreType.DMA(())],
        compiler_params=_cparams(("arbitrary",)),
        name="moe_combine",
    )(pos_flat, gates, x2, ys)


def _rel_bucket_np(dist):
    n = np.maximum(dist, 0)
    nf = np.maximum(n, 1).astype(np.float32)
    ratio = np.log(nf / np.float32(REL_MAX_EXACT)) / np.float32(math.log(REL_MAX_DIST / REL_MAX_EXACT))
    large = REL_MAX_EXACT + (ratio * np.float32(REL_BUCKETS - REL_MAX_EXACT)).astype(np.int32)
    large = np.minimum(large, REL_BUCKETS - 1)
    return np.where(n < REL_MAX_EXACT, n, large)


def _bias_tables_kernel(tbl_ref, bc_ref, bt_ref, tc_ref, tn_ref):
    h = pl.program_id(0)
    bc, bt = bc_ref[...], bt_ref[...]
    tc = jnp.zeros(bc.shape, F32)
    tn = jnp.zeros(bt.shape, F32)
    far = tbl_ref[REL_BUCKETS - 1, h]
    for b in range(REL_BUCKETS - 1):
        v = (tbl_ref[b, h] - far) * LOG2E
        tc = jnp.where(bc == b, v, tc)
        tn = jnp.where(bt == b, v, tn)
    tc_ref[:, 0] = tc
    tn_ref[0] = tn


def _bias_tables(rel_table):
    q = np.arange(NSA_TQS)[None, :, None]
    blocks_per_tile = NSA_TQS // CMP_STRIDE
    n_res = LANES // blocks_per_tile
    res = np.arange(n_res)[:, None, None]
    base_gap = (blocks_per_tile * res - CMP_WINDOW_BACK) % LANES + CMP_WINDOW_BACK
    w = np.arange(2 * LANES)[None, None, :]
    b_cmp = _rel_bucket_np(CMP_STRIDE * base_gap + q - CMP_STRIDE * w - (CMP_LEN - 1)).astype(np.int32)
    c = np.arange(2 * NSA_TK + WIN)[None, :]
    b_tok = _rel_bucket_np(np.arange(NSA_TQ)[:, None] + WIN - (c - NSA_TK)).astype(np.int32)
    heads = rel_table.shape[1]
    return pl.pallas_call(
        _bias_tables_kernel,
        grid=(heads,),
        in_specs=[pl.BlockSpec(memory_space=pltpu.SMEM),
                  pl.BlockSpec(b_cmp.shape, lambda h: (0, 0, 0)),
                  pl.BlockSpec(b_tok.shape, lambda h: (0, 0))],
        out_specs=[pl.BlockSpec((n_res, 1) + b_cmp.shape[1:], lambda h: (0, h, 0, 0)),
                   pl.BlockSpec((1,) + b_tok.shape, lambda h: (h, 0, 0))],
        out_shape=[jax.ShapeDtypeStruct((n_res, heads) + b_cmp.shape[1:], F32),
                   jax.ShapeDtypeStruct((heads,) + b_tok.shape, F32)],
        compiler_params=_cparams(("parallel",)),
        name="bias_tables",
    )(rel_table.astype(F32), jnp.asarray(b_cmp), jnp.asarray(b_tok))


def kernel(x, mem, g_attn_norm, w_in, g_cq, w_uq, g_ckv, w_ukv, g_q_mla, g_k_mla, cmp_k_pos, cmp_k_w1, cmp_k_w2, cmp_v_pos, cmp_v_w1, cmp_v_w2, g_q_nsa, g_k_nsa, rel_table, g_out_mla, g_out_nsa, w_out, g_mem_norm, g_mem_src, w_mq, w_mkv, g_mq, g_mk, w_mo, g_moe_norm, router_w, router_b, w_gate_up, b_gate_up, w_down, b_down):
    batch, seq, d = x.shape
    n = batch * seq
    depth = w_in.shape[0]
    assert seq % 512 == 0 and seq // SLC_LEN <= LANES
    row = lambda v: v.reshape(1, -1).astype(F32)
    tile2 = lambda v: jnp.concatenate([v, v]).reshape(1, -1).astype(F32)

    x2d = x.reshape(n, d)
    for l in range(depth):
        wi = w_in[l]
        w_in_r = jnp.concatenate(
            [wi[:, 0:384], wi[:, 416:928], wi[:, 928:1696], wi[:, 384:416], wi[:, 1696:1720],
             jnp.zeros((d, IN_COLS_PAD - 1720), wi.dtype)], axis=1).astype(BF16)
        dq = MLA_NOPE + MLA_ROPE
        wuq_r = jnp.pad(w_uq[l].reshape(MLA_Q_RANK, MLA_HEADS, dq), ((0, 0), (0, 0), (0, LANES - dq)))
        wuq_r = wuq_r.reshape(MLA_Q_RANK, MLA_HEADS * LANES).astype(BF16)
        wukv = w_ukv[l].reshape(MLA_KV_RANK, MLA_HEADS, MLA_NOPE + MLA_V)
        wuk_r = jnp.pad(wukv[:, :, :MLA_NOPE], ((0, 0), (0, 0), (0, LANES - MLA_NOPE)))
        wuk_r = wuk_r.reshape(MLA_KV_RANK, MLA_HEADS * LANES).astype(BF16)
        wuv_r = wukv[:, :, MLA_NOPE:].reshape(MLA_KV_RANK, MLA_HEADS * MLA_V).astype(BF16)
        gq_pad = jnp.pad(g_q_mla[l], (0, LANES - dq)).reshape(1, LANES)
        gk_pad = jnp.pad(g_k_mla[l], (0, LANES - dq)).reshape(1, LANES)

        half = MLA_ROPE // 2
        inv = ROPE_THETA ** (-jnp.arange(half, dtype=F32) / half)
        ang = jnp.arange(seq, dtype=F32)[:, None] * inv
        cos, sin = jnp.cos(ang), jnp.sin(ang)
        zeros = jnp.zeros((seq, half), F32)
        rc = jnp.concatenate([jnp.ones((seq, MLA_NOPE), F32), cos, cos, jnp.zeros((seq, LANES - dq), F32)], 1)
        rs1 = jnp.concatenate([jnp.zeros((seq, MLA_NOPE), F32), -sin, zeros, jnp.zeros((seq, LANES - dq), F32)], 1)
        rs2 = jnp.concatenate([jnp.zeros((seq, MLA_NOPE), F32), zeros, sin, jnp.zeros((seq, LANES - dq), F32)], 1)

        mla_in, qn, kv6, misc = _in_proj(x2d, row(g_attn_norm[l]), w_in_r)
        q_m, k_m, v_m = _mla_prep(mla_in, misc, row(g_cq[l]), wuq_r, row(g_ckv[l]), wuk_r, wuv_r,
                                  gq_pad, gk_pad, rc, rs1, rs2, seq)
        y_mla = _mla_attn(q_m, k_m, v_m, batch, seq)

        nc = seq // CMP_STRIDE
        half_len = CMP_LEN // 2
        eye_g = jnp.eye(NSA_KV_GROUPS, dtype=F32)

        def cmp_weights(pos, w1, w2):
            out = []
            for part in range(2):
                sl = slice(part * half_len, (part + 1) * half_len)
                out.append(jnp.broadcast_to(pos[sl][:, None, :], (half_len, NSA_KV_GROUPS, NSA_HD))
                           .reshape(1, -1))
            for part in range(2):
                sl = slice(part * half_len, (part + 1) * half_len)
                wexp = jnp.einsum('ldf,gh->lgdhf', w1[sl], eye_g)
                out.append(wexp.reshape(half_len * NSA_KV_GROUPS * NSA_HD, NSA_KV_GROUPS * CMP_HIDDEN).astype(BF16))
            out.append(w2.astype(BF16))
            return out

        chunk_w = CMP_STRIDE * NSA_KV_GROUPS * NSA_HD
        kcmp, vcmp = _nsa_cmp(kv6[0].reshape(batch, nc, chunk_w), kv6[1].reshape(batch, nc, chunk_w),
                              cmp_weights(cmp_k_pos[l], cmp_k_w1[l], cmp_k_w2[l]),
                              cmp_weights(cmp_v_pos[l], cmp_v_w1[l], cmp_v_w2[l]), row(g_k_nsa[l]))
        q_n, ks, vs, kw, vw, gates_n = _nsa_prep(qn, kv6, misc, tile2(g_q_nsa[l]), tile2(g_k_nsa[l]), seq)
        by_batch = lambda t: t.reshape(NSA_KV_GROUPS, batch, seq, t.shape[-1])
        front_pad = lambda t: jnp.pad(by_batch(t), ((0, 0), (0, 0), (WIN, 0), (0, 0)))
        tc, tn = _bias_tables(rel_table)
        n_idx = np.arange(nc)[:, None]
        j_idx = np.arange(LANES)[None, :]
        ovl = ((CMP_STRIDE * n_idx < SLC_LEN * j_idx + SLC_LEN)
               & (CMP_STRIDE * n_idx + CMP_LEN - 1 >= SLC_LEN * j_idx)
               & (n_idx < nc - 1) & (j_idx < seq // SLC_LEN)).astype(np.float32)
        o_cmp, sel = _nsa_select(q_n, kcmp, vcmp, tc, jnp.asarray(ovl, dtype=BF16), batch, seq)
        y_nsa = _nsa_attn(q_n, by_batch(ks), by_batch(vs), front_pad(kw), front_pad(vw), sel,
                          tn, gates_n, o_cmp, batch, seq)

        k_mem, v_mem = _mem_kv(mem, row(g_mem_src[l]), w_mkv[l].reshape(d, 2 * MEM_HEADS * MEM_HD).astype(BF16),
                               tile2(g_mk[l]))
        rw = jnp.pad(router_w[l], ((0, 0), (0, LANES - N_EXPERTS)))
        rb = jnp.pad(router_b[l], (0, LANES - N_EXPERTS), constant_values=NEG_INF).reshape(1, LANES)
        wo = w_out[l].astype(BF16)
        n_mla = MLA_HEADS * MLA_V
        x2, h2, topi, gates_e, counts = _post_attn(
            x2d, y_mla, y_nsa, row(g_out_mla[l]), row(g_out_nsa[l]), wo[:n_mla], wo[n_mla:],
            row(g_mem_norm[l]), w_mq[l].astype(BF16), tile2(g_mq[l]), k_mem, v_mem, w_mo[l].astype(BF16),
            row(g_moe_norm[l]), rw, rb, batch, seq)

        cnt = counts[0].astype(jnp.int32)
        padded = (cnt + MOE_BLK - 1) // MOE_BLK * MOE_BLK
        pad_end = jnp.cumsum(padded)
        pad_start = (pad_end - padded).astype(F32).reshape(1, LANES)
        p_rows = (n * TOP_K // MOE_BLK + N_EXPERTS) * MOE_BLK
        n_blk = p_rows // MOE_BLK
        blk_first_row = jnp.arange(n_blk, dtype=jnp.int32) * MOE_BLK
        blk_e = jnp.minimum(jnp.sum(pad_end[None, :N_EXPERTS] <= blk_first_row[:, None], axis=1),
                            N_EXPERTS - 1).astype(jnp.int32)
        n_used = (pad_end[N_EXPERTS - 1] // MOE_BLK).astype(jnp.int32).reshape(1)
        tm_pos = 256
        tri = (np.arange(tm_pos)[None, :] < np.arange(tm_pos)[:, None]).astype(np.float32)
        pos = _moe_pos(topi, pad_start, jnp.asarray(tri, dtype=BF16))
        pos_flat = pos[:, :TOP_K].reshape(n * TOP_K)
        xs = _moe_scatter(pos_flat, h2, jnp.zeros((p_rows, d), F32))
        src = np.arange(GU_GROUP)
        perm = (np.arange(GU_GROUP)[:, None] == np.where(src < LANES, 2 * src, 2 * (src - LANES) + 1)[None, :])
        wgu = _moe_wprep(w_gate_up[l], jnp.asarray(perm.astype(np.float32), dtype=BF16))
        bgu = b_gate_up[l]
        ys = _moe_ffn(blk_e, n_used, xs, wgu, bgu[:, None, 0::2], bgu[:, None, 1::2],
                      w_down[l], b_down[l][:, None, :])
        x2d = _moe_combine(pos_flat, gates_e, x2, ys)
    return x2d.reshape(batch, seq, d)
```

```python
import functools
import math

import numpy as np
import jax
import jax.numpy as jnp
from jax import lax
from jax.experimental import pallas as pl
from jax.experimental.pallas import tpu as pltpu

F32 = jnp.float32
BF16 = jnp.bfloat16

EPS = 1e-6
NEG_INF = -1e30
LANES = 128

MLA_HEADS = 8
MLA_NOPE = 64
MLA_ROPE = 32
MLA_V = 64
MLA_Q_RANK = 256
MLA_KV_RANK = 128
ROPE_THETA = 10000.0

NSA_HEADS = 8
NSA_KV_GROUPS = 2
NSA_HPG = NSA_HEADS // NSA_KV_GROUPS
NSA_HD = 64
CMP_LEN = 32
CMP_STRIDE = 16
CMP_HIDDEN = 128
SLC_LEN = 64
SLC_TOPK = 16
WIN = 512
FORCE_SCORE = 1e9
NSA_TQ = 256
NSA_TQS = 512
NSA_TK = 512
CMP_WINDOW_BACK = 40

REL_BUCKETS = 32
REL_MAX_EXACT = 16
REL_MAX_DIST = 512

MEM_HEADS = 4
MEM_HD = 64

N_EXPERTS = 32
TOP_K = 4
SWIGLU_LIMIT = 7.0
SWIGLU_ALPHA = 1.702
MOE_BLK = 256
ROW_TILE = 8

LOG2E = math.log2(math.e)
MASK_BIG = 2.0 ** 100

VMEM_LIMIT = 56 * 1024 * 1024


def _cparams(sem, vmem=VMEM_LIMIT):
    return pltpu.CompilerParams(dimension_semantics=sem, vmem_limit_bytes=vmem)


def _rms(x, g):
    return x * lax.rsqrt(jnp.mean(x * x, axis=-1, keepdims=True) + EPS) * g


def _seg_rms64(t, g2):
    lane = lax.broadcasted_iota(jnp.int32, t.shape, 1)
    sq = t * t
    lo = jnp.sum(jnp.where(lane < 64, sq, 0.0), axis=-1, keepdims=True)
    hi = jnp.sum(jnp.where(lane >= 64, sq, 0.0), axis=-1, keepdims=True)
    ms = jnp.where(lane < 64, lo, hi) * (1.0 / 64.0)
    return t * lax.rsqrt(ms + EPS) * g2


def _dot(a, b):
    return jnp.dot(a, b, preferred_element_type=F32)


def _with_ones(v):
    lane = lax.broadcasted_iota(jnp.int32, v.shape, 1)
    return jnp.concatenate([v, jnp.where(lane == 0, 1.0, 0.0).astype(v.dtype)], axis=1)


def _dot_nt(a, b):
    return lax.dot_general(a, b, (((1,), (1,)), ((), ())), preferred_element_type=F32)


def _softmax_weights(s, m_old):
    cols = [s[:, c * LANES:(c + 1) * LANES] for c in range(s.shape[1] // LANES)]
    m_new = jnp.maximum(m_old, jnp.max(functools.reduce(jnp.maximum, cols), axis=-1, keepdims=True))
    return m_new, jnp.concatenate([jnp.exp2(c - m_new).astype(BF16) for c in cols], axis=1)


IN_COLS_PAD = 1792


def _in_proj_kernel(x_ref, g_ref, w_ref, mla_ref, qn_ref, kv_ref, misc_ref):
    h = _rms(x_ref[...], g_ref[...])
    p = _dot(h.astype(BF16), w_ref[...])
    mla_ref[...] = p[:, 0:384]
    qn_ref[...] = p[:, 384:896]
    for j in range(6):
        kv_ref[j] = p[:, 896 + 128 * j:1024 + 128 * j]
    misc_ref[...] = p[:, 1664:1792]


def _in_proj(x2d, g, w):
    n, d = x2d.shape
    tm = 512
    return pl.pallas_call(
        _in_proj_kernel,
        grid=(n // tm,),
        in_specs=[pl.BlockSpec((tm, d), lambda i: (i, 0)),
                  pl.BlockSpec((1, d), lambda i: (0, 0)),
                  pl.BlockSpec((d, IN_COLS_PAD), lambda i: (0, 0))],
        out_specs=[pl.BlockSpec((tm, 384), lambda i: (i, 0)),
                   pl.BlockSpec((tm, 512), lambda i: (i, 0)),
                   pl.BlockSpec((6, tm, 128), lambda i: (0, i, 0)),
                   pl.BlockSpec((tm, 128), lambda i: (i, 0))],
        out_shape=[jax.ShapeDtypeStruct((n, 384), F32),
                   jax.ShapeDtypeStruct((n, 512), F32),
                   jax.ShapeDtypeStruct((6, n, 128), F32),
                   jax.ShapeDtypeStruct((n, 128), F32)],
        compiler_params=_cparams(("parallel",)),
        name="in_proj",
    )(x2d, g, w)


def _mla_prep_kernel(mla_ref, misc_ref, gcq_ref, wuq_ref, gckv_ref, wuk_ref, wuv_ref, gq_ref, gk_ref,
                     rc_ref, rs1_ref, rs2_ref, q_ref, k_ref, v_ref):
    p = mla_ref[...]
    cqn = _rms(p[:, 0:MLA_Q_RANK], gcq_ref[...]).astype(BF16)
    ckvn = _rms(p[:, MLA_Q_RANK:MLA_Q_RANK + MLA_KV_RANK], gckv_ref[...]).astype(BF16)
    qall = _dot(cqn, wuq_ref[...])
    kall = _dot(ckvn, wuk_ref[...])
    vall = _dot(ckvn, wuv_ref[...])
    misc = misc_ref[...]
    lane = lax.broadcasted_iota(jnp.int32, misc.shape, 1)
    in_rope = (lane >= MLA_NOPE) & (lane < MLA_NOPE + MLA_ROPE)
    krope = jnp.where(in_rope, pltpu.roll(misc, MLA_NOPE, axis=1), 0.0)
    rc, rs1, rs2 = rc_ref[...], rs1_ref[...], rs2_ref[...]
    gq, gk = gq_ref[...], gk_ref[...]
    inv_dk = 1.0 / (MLA_NOPE + MLA_ROPE)
    scale = (MLA_NOPE + MLA_ROPE) ** -0.5 * LOG2E

    def norm_rope(t, g):
        t = t * lax.rsqrt(jnp.sum(t * t, axis=-1, keepdims=True) * inv_dk + EPS) * g
        half = MLA_ROPE // 2
        return t * rc + pltpu.roll(t, LANES - half, axis=1) * rs1 + pltpu.roll(t, half, axis=1) * rs2

    for h in range(MLA_HEADS):
        qh = norm_rope(qall[:, h * LANES:(h + 1) * LANES], gq)
        q_ref[h] = (qh * scale).astype(BF16)
        kh = norm_rope(kall[:, h * LANES:(h + 1) * LANES] + krope, gk)
        k_ref[h] = kh.astype(BF16)
        v_ref[h] = _with_ones(vall[:, h * MLA_V:(h + 1) * MLA_V]).astype(BF16)


def _mla_prep(mla, misc, gcq, wuq, gckv, wuk, wuv, gq, gk, rc, rs1, rs2, seq):
    n = mla.shape[0]
    tm = 256
    ns = seq // tm
    full = lambda a: pl.BlockSpec(a.shape, lambda i: (0,) * a.ndim)
    rope_spec = pl.BlockSpec((tm, LANES), lambda i: (i % ns, 0))
    head_spec = pl.BlockSpec((MLA_HEADS, tm, LANES), lambda i: (0, i, 0))
    head_sds = jax.ShapeDtypeStruct((MLA_HEADS, n, LANES), BF16)
    return pl.pallas_call(
        _mla_prep_kernel,
        grid=(n // tm,),
        in_specs=[pl.BlockSpec((tm, 384), lambda i: (i, 0)),
                  pl.BlockSpec((tm, LANES), lambda i: (i, 0)),
                  full(gcq), full(wuq), full(gckv), full(wuk), full(wuv), full(gq), full(gk),
                  rope_spec, rope_spec, rope_spec],
        out_specs=[head_spec, head_spec, head_spec],
        out_shape=[head_sds, head_sds, head_sds],
        compiler_params=_cparams(("parallel",)),
        name="mla_prep",
    )(mla, misc, gcq, wuq, gckv, wuk, wuv, gq, gk, rc, rs1, rs2)


def _mla_attn_kernel(q_ref, k_ref, v_ref, o_ref, m_ref, acc_ref):
    qi, ki = pl.program_id(1), pl.program_id(2)

    @pl.when(ki == 0)
    def _():
        m_ref[...] = jnp.full(m_ref.shape, NEG_INF, F32)
        acc_ref[...] = jnp.zeros(acc_ref.shape, F32)

    def step(masked):
        def head(h, carry):
            s = _dot_nt(q_ref[h], k_ref[h])
            if masked:
                row = lax.broadcasted_iota(jnp.int32, s.shape, 0)
                col = lax.broadcasted_iota(jnp.int32, s.shape, 1)
                s = jnp.where(col <= row, s, -jnp.inf)
            m_old = m_ref[h]
            m_new, p = _softmax_weights(s, m_old)
            acc_ref[h] = jnp.exp2(m_old - m_new) * acc_ref[h] + _dot(p, v_ref[h])
            m_ref[h] = m_new
            return carry
        lax.fori_loop(0, MLA_HEADS, head, 0, unroll=4)

    @pl.when(ki < qi)
    def _():
        step(False)

    @pl.when(ki == qi)
    def _():
        step(True)

    @pl.when(ki == pl.num_programs(2) - 1)
    def _():
        for h in range(MLA_HEADS):
            acc = acc_ref[h]
            o_ref[:, h * MLA_V:(h + 1) * MLA_V] = acc[:, :MLA_V] / acc[:, MLA_V:MLA_V + 1]


def _mla_attn(q, k, v, batch, seq):
    n = q.shape[1]
    tq = tk = 512
    nq = seq // tq
    kv_spec = pl.BlockSpec((MLA_HEADS, tk, LANES), lambda b, i, j: (0, b * nq + jnp.minimum(i, j), 0))
    return pl.pallas_call(
        _mla_attn_kernel,
        grid=(batch, nq, nq),
        in_specs=[pl.BlockSpec((MLA_HEADS, tq, LANES), lambda b, i, j: (0, b * nq + i, 0)), kv_spec, kv_spec],
        out_specs=pl.BlockSpec((tq, MLA_HEADS * MLA_V), lambda b, i, j: (b * nq + i, 0)),
        out_shape=jax.ShapeDtypeStruct((n, MLA_HEADS * MLA_V), F32),
        scratch_shapes=[pltpu.VMEM((MLA_HEADS, tq, LANES), F32),
                        pltpu.VMEM((MLA_HEADS, tq, LANES), F32)],
        compiler_params=_cparams(("parallel", "parallel", "arbitrary")),
        name="mla_attn",
    )(q, k, v)


def _nsa_cmp_kernel(kc_ref, vc_ref, pak_ref, pbk_ref, wak_ref, wbk_ref, w2k_ref,
                    pav_ref, pbv_ref, wav_ref, wbv_ref, w2v_ref, gk_ref, kout_ref, vout_ref):
    def compress(chunks, pa, pb, wa, wb, w2):
        nc = chunks.shape[0]
        ha = _dot((chunks + pa).astype(BF16), wa)
        hb = _dot((chunks + pb).astype(BF16), wb)
        hid = jax.nn.gelu(ha + pltpu.roll(hb, nc - 1, axis=0))
        return [_dot(hid[:, g * CMP_HIDDEN:(g + 1) * CMP_HIDDEN].astype(BF16), w2)
                for g in range(NSA_KV_GROUPS)]

    kc = compress(kc_ref[0], pak_ref[...], pbk_ref[...], wak_ref[...], wbk_ref[...], w2k_ref[...])
    vc = compress(vc_ref[0], pav_ref[...], pbv_ref[...], wav_ref[...], wbv_ref[...], w2v_ref[...])
    for g in range(NSA_KV_GROUPS):
        kn = _rms(kc[g], gk_ref[...])
        kout_ref[0, g] = jnp.concatenate([kn, jnp.zeros_like(kn)], axis=1).astype(BF16)
        vout_ref[0, g] = vc[g].astype(BF16)


def _nsa_cmp(kc_chunks, vc_chunks, wk, wv, gk):
    b, nc, width = kc_chunks.shape
    full = lambda a: pl.BlockSpec(a.shape, lambda i: (0,) * a.ndim)
    chunk_spec = pl.BlockSpec((1, nc, width), lambda i: (i, 0, 0))
    out_spec = lambda width: pl.BlockSpec((1, NSA_KV_GROUPS, nc, width), lambda i: (i, 0, 0, 0))
    out_sds = lambda width: jax.ShapeDtypeStruct((b, NSA_KV_GROUPS, nc, width), BF16)
    return pl.pallas_call(
        _nsa_cmp_kernel,
        grid=(b,),
        in_specs=[chunk_spec, chunk_spec] + [full(a) for a in wk] + [full(a) for a in wv] + [full(gk)],
        out_specs=[out_spec(LANES), out_spec(NSA_HD)],
        out_shape=[out_sds(LANES), out_sds(NSA_HD)],
        compiler_params=_cparams(("parallel",)),
        name="nsa_cmp",
    )(kc_chunks, vc_chunks, *wk, *wv, gk)


def _nsa_prep_kernel(qn_ref, ks_ref, vs_ref, kw_ref, vw_ref, misc_ref, gq_ref, gk_ref,
                     q_ref, kso_ref, vso_ref, kwo_ref, vwo_ref, gate_ref, *, seq):
    gq2, gk2 = gq_ref[...], gk_ref[...]
    tm = qn_ref.shape[0]
    scale = NSA_HD ** -0.5 * LOG2E
    zpad = jnp.zeros((tm, NSA_HD), F32)
    for c in range(NSA_HEADS // 2):
        t = _seg_rms64(qn_ref[:, c * LANES:(c + 1) * LANES], gq2) * scale
        q_ref[2 * c] = jnp.concatenate([t[:, :NSA_HD], zpad], axis=1).astype(BF16)
        q_ref[2 * c + 1] = jnp.concatenate([t[:, NSA_HD:], zpad], axis=1).astype(BF16)
    ksn = _seg_rms64(ks_ref[0], gk2)
    kwn = _seg_rms64(kw_ref[0], gk2)
    vs, vw = vs_ref[0], vw_ref[0]
    row = lax.broadcasted_iota(jnp.int32, (tm, LANES), 0)
    lane = lax.broadcasted_iota(jnp.int32, (tm, LANES), 1)
    pos = (pl.program_id(0) * tm) % seq + row
    neg_onehot = jnp.where(lane == pos // SLC_LEN, -1.0, 0.0)
    for g in range(NSA_KV_GROUPS):
        sl = slice(g * NSA_HD, (g + 1) * NSA_HD)
        kso_ref[g] = jnp.concatenate([neg_onehot, ksn[:, sl], zpad], axis=1).astype(BF16)
        kwo_ref[g] = jnp.concatenate([kwn[:, sl], zpad], axis=1).astype(BF16)
        vso_ref[g] = _with_ones(vs[:, sl]).astype(BF16)
        vwo_ref[g] = _with_ones(vw[:, sl]).astype(BF16)
    sig = jax.nn.sigmoid(misc_ref[...])
    per_group = 3 * NSA_HPG
    for g in range(NSA_KV_GROUPS):
        gate_ref[g] = pltpu.roll(sig, LANES - (MLA_ROPE + per_group * g), axis=1)


def _nsa_prep(qn, kv, misc, gq2, gk2, seq):
    n = qn.shape[0]
    tm = 512
    full = lambda a: pl.BlockSpec(a.shape, lambda i: (0,) * a.ndim)
    kv_spec = lambda j: pl.BlockSpec((1, tm, LANES), lambda i, j=j: (j, i, 0))
    g_spec = lambda width: pl.BlockSpec((NSA_KV_GROUPS, tm, width), lambda i: (0, i, 0))
    g_sds = lambda width: jax.ShapeDtypeStruct((NSA_KV_GROUPS, n, width), BF16)
    return pl.pallas_call(
        functools.partial(_nsa_prep_kernel, seq=seq),
        grid=(n // tm,),
        in_specs=[pl.BlockSpec((tm, 512), lambda i: (i, 0)),
                  kv_spec(2), kv_spec(3), kv_spec(4), kv_spec(5),
                  pl.BlockSpec((tm, LANES), lambda i: (i, 0)), full(gq2), full(gk2)],
        out_specs=[pl.BlockSpec((NSA_HEADS, tm, LANES), lambda i: (0, i, 0)),
                   g_spec(2 * LANES), g_spec(LANES), g_spec(LANES), g_spec(LANES),
                   pl.BlockSpec((NSA_KV_GROUPS, tm, LANES), lambda i: (0, i, 0))],
        out_shape=[jax.ShapeDtypeStruct((NSA_HEADS, n, LANES), BF16),
                   g_sds(2 * LANES), g_sds(LANES), g_sds(LANES), g_sds(LANES),
                   jax.ShapeDtypeStruct((NSA_KV_GROUPS, n, LANES), F32)],
        compiler_params=_cparams(("parallel",)),
        name="nsa_prep",
    )(qn, kv, kv, kv, kv, misc, gq2, gk2)


def _nsa_select_kernel(q_ref, kc_ref, vc_ref, tc_ref, ovl_ref, ocmp_ref, sel_ref, s_ref, *, nc, n_sel):
    tq, hpg = NSA_TQS, NSA_HPG
    i = pl.program_id(2)
    q0 = i * tq
    ncb = nc // LANES
    s_ref[...] = _dot_nt(q_ref[...].reshape(hpg * tq, LANES), kc_ref[0, 0])

    first_blk = ((tq // CMP_STRIDE) * i + LANES - CMP_WINDOW_BACK) // LANES - 1
    for half in range(2):
        blk = first_blk + half

        @pl.when((blk >= 0) & (blk < ncb))
        def _():
            c0 = pl.multiple_of(blk * LANES, LANES)
            for h in range(hpg):
                s_ref[h * tq:(h + 1) * tq, pl.ds(c0, LANES)] += tc_ref[0, h, :, half * LANES:(half + 1) * LANES]

    row = lax.broadcasted_iota(jnp.int32, (tq, LANES), 0)
    lane = lax.broadcasted_iota(jnp.int32, (tq, LANES), 1)
    qpos = q0 + row
    valid = [(CMP_STRIDE * (c * LANES + lane) + (CMP_LEN - 1)) <= qpos for c in range(ncb)]
    psum = [jnp.zeros((tq, LANES), F32) for _ in range(ncb)]
    vc = vc_ref[0, 0]
    for h in range(hpg):
        rs = slice(h * tq, (h + 1) * tq)
        cols = [jnp.where(valid[c], s_ref[rs, c * LANES:(c + 1) * LANES], -jnp.inf) for c in range(ncb)]
        m = jnp.maximum(jnp.max(functools.reduce(jnp.maximum, cols), axis=-1, keepdims=True), NEG_INF)
        es = [jnp.exp2(c - m) for c in cols]
        inv = 1.0 / jnp.maximum(jnp.sum(functools.reduce(jnp.add, es), axis=-1, keepdims=True), 1e-30)
        ps = [e * inv for e in es]
        psum = [a + p for a, p in zip(psum, ps)]
        ocmp_ref[:, h * NSA_HD:(h + 1) * NSA_HD] = _dot(
            jnp.concatenate([p.astype(BF16) for p in ps], axis=1), vc)

    pcat = jnp.concatenate(psum, axis=1)
    hi = pcat.astype(BF16)
    rest = pcat - hi.astype(F32)
    mid = rest.astype(BF16)
    lo = (rest - mid.astype(F32)).astype(BF16)
    ovl = ovl_ref[...]
    imp = _dot(hi, ovl) + _dot(mid, ovl) + _dot(lo, ovl)

    jf = lane.astype(F32)
    cur = qpos // SLC_LEN
    forced = (lane == 0) | (lane == cur) | (lane == cur - 1)
    imp = jnp.where(forced, FORCE_SCORE, imp)
    imp = jnp.where(lane * SLC_LEN <= qpos, imp, NEG_INF)
    for _ in range(n_sel):
        best = jnp.max(imp, axis=-1, keepdims=True)
        first = jnp.min(jnp.where(imp == best, jf, float(LANES)), axis=-1, keepdims=True)
        imp = jnp.where(jf == first, -jnp.inf, imp)
    sel_ref[0] = jnp.where(imp == -jnp.inf, 0.0, MASK_BIG).astype(BF16)


def _nsa_select(q, kcmp, vcmp, tc, ovl, batch, seq):
    n = q.shape[1]
    nc = kcmp.shape[2]
    nq = seq // NSA_TQS
    n_res = tc.shape[0]
    n_sel = min(SLC_TOPK, seq // SLC_LEN)
    g_, hpg = NSA_KV_GROUPS, NSA_HPG
    kern = functools.partial(_nsa_select_kernel, nc=nc, n_sel=n_sel)
    cmp_spec = lambda width: pl.BlockSpec((1, 1, nc, width), lambda b, g, i: (b, g, 0, 0))
    return pl.pallas_call(
        kern,
        grid=(batch, g_, nq),
        in_specs=[pl.BlockSpec((hpg, NSA_TQS, LANES), lambda b, g, i: (g, b * nq + i, 0)),
                  cmp_spec(LANES), cmp_spec(NSA_HD),
                  pl.BlockSpec((1, hpg, NSA_TQS, 2 * LANES), lambda b, g, i: (i % n_res, g, 0, 0)),
                  pl.BlockSpec(ovl.shape, lambda b, g, i: (0, 0))],
        out_specs=[pl.BlockSpec((NSA_TQS, hpg * NSA_HD), lambda b, g, i: (b * nq + i, g)),
                   pl.BlockSpec((1, NSA_TQS, LANES), lambda b, g, i: (g, b * nq + i, 0))],
        out_shape=[jax.ShapeDtypeStruct((n, NSA_HEADS * NSA_HD), F32),
                   jax.ShapeDtypeStruct((g_, n, LANES), BF16)],
        scratch_shapes=[pltpu.VMEM((hpg * NSA_TQS, nc), F32)],
        compiler_params=_cparams(("parallel", "parallel", "arbitrary")),
        name="nsa_select",
    )(q, kcmp, vcmp, tc, ovl)


def _nsa_attn_kernel(q_ref, ks_ref, vs_ref, kw_ref, vw_ref, sel_ref, tn_ref, gate_ref, ocmp_ref,
                     o_ref, m_ref, acc_ref):
    tq, tk, hpg = NSA_TQ, NSA_TK, NSA_HPG
    rows = hpg * tq
    i = pl.program_id(2)
    q0 = i * tq
    q4 = q_ref[...].reshape(rows, LANES)
    q_aug = jnp.concatenate([jnp.concatenate([sel_ref[0]] * hpg, axis=0), q4], axis=1)
    row = lax.broadcasted_iota(jnp.int32, (tq, tk), 0)
    col = lax.broadcasted_iota(jnp.int32, (tq, tk), 1)
    m_ref[...] = jnp.full(m_ref.shape, NEG_INF, F32)
    acc_ref[...] = jnp.zeros(acc_ref.shape, F32)

    def tile(kt, near):
        k0 = pl.multiple_of(kt * tk, tk)
        s = _dot_nt(q_aug, ks_ref[0, 0, pl.ds(k0, tk), :])
        if near:
            causal = (k0 + col) <= (q0 + row)
            start = pl.multiple_of(k0 + tk + WIN - q0, LANES)
        ps, alphas = [], []
        for h in range(hpg):
            rs = slice(h * tq, (h + 1) * tq)
            sh = s[rs]
            if near:
                sh = jnp.where(causal, sh + tn_ref[h, :, pl.ds(start, tk)], -jnp.inf)
            m_old = m_ref[rs]
            m_new, p = _softmax_weights(sh, m_old)
            m_ref[rs] = m_new
            alphas.append(jnp.exp2(m_old - m_new))
            ps.append(p)
        acc_ref[...] = (jnp.concatenate(alphas, axis=0) * acc_ref[...]
                        + _dot(jnp.concatenate(ps, axis=0), vs_ref[0, 0, pl.ds(k0, tk), :]))

    last = q0 // tk
    n_far = jnp.maximum(last - 1, 0)
    def far_pair(pair, c):
        tile(2 * pair, False)
        tile(2 * pair + 1, False)
        return c

    lax.fori_loop(0, n_far // 2, far_pair, 0)
    lax.fori_loop(n_far // 2 * 2, n_far, lambda kt, c: (tile(kt, False), c)[1], 0)
    lax.fori_loop(n_far, last + 1, lambda kt, c: (tile(kt, True), c)[1], 0)

    wk = WIN + tq
    w0 = pl.multiple_of(q0, tq)
    s = _dot_nt(q4, kw_ref[0, 0, pl.ds(w0, wk), :])
    roww = lax.broadcasted_iota(jnp.int32, (tq, wk), 0)
    colw = lax.broadcasted_iota(jnp.int32, (tq, wk), 1)
    dist = roww + WIN - colw
    validw = (dist >= 0) & (dist < WIN) & (q0 - WIN + colw >= 0)
    ps = []
    for h in range(hpg):
        sh = s[h * tq:(h + 1) * tq] + tn_ref[h, :, tk:tk + wk]
        ps.append(_softmax_weights(jnp.where(validw, sh, -jnp.inf), NEG_INF)[1])
    accw = _dot(jnp.concatenate(ps, axis=0), vw_ref[0, 0, pl.ds(w0, wk), :])

    gates = gate_ref[0]
    acc = acc_ref[...]
    outs = []
    for h in range(hpg):
        rs = slice(h * tq, (h + 1) * tq)
        o_cmp = ocmp_ref[:, h * NSA_HD:(h + 1) * NSA_HD]
        o_slc = acc[rs, :NSA_HD] / jnp.maximum(acc[rs, NSA_HD:NSA_HD + 1], 1e-30)
        o_win = accw[rs, :NSA_HD] / accw[rs, NSA_HD:NSA_HD + 1]
        outs.append(gates[:, 3 * h:3 * h + 1] * o_cmp + gates[:, 3 * h + 1:3 * h + 2] * o_slc
                    + gates[:, 3 * h + 2:3 * h + 3] * o_win)
    o_ref[...] = jnp.concatenate(outs, axis=1)


def _nsa_attn(q, ks, vs, kwp, vwp, sel, tn, gates, ocmp, batch, seq):
    n = q.shape[1]
    nq = seq // NSA_TQ
    g_, hpg = NSA_KV_GROUPS, NSA_HPG
    seq_spec = lambda length, width: pl.BlockSpec((1, 1, length, width), lambda b, g, i: (g, b, 0, 0))
    tok_spec = pl.BlockSpec((1, NSA_TQ, LANES), lambda b, g, i: (g, b * nq + i, 0))
    out_spec = pl.BlockSpec((NSA_TQ, hpg * NSA_HD), lambda b, g, i: (b * nq + i, g))
    return pl.pallas_call(
        _nsa_attn_kernel,
        grid=(batch, g_, nq),
        in_specs=[pl.BlockSpec((hpg, NSA_TQ, LANES), lambda b, g, i: (g, b * nq + i, 0)),
                  seq_spec(seq, 2 * LANES), seq_spec(seq, LANES), seq_spec(seq + WIN, LANES), seq_spec(seq + WIN, LANES),
                  tok_spec,
                  pl.BlockSpec((hpg, NSA_TQ, 2 * NSA_TK + WIN), lambda b, g, i: (g, 0, 0)),
                  tok_spec, out_spec],
        out_specs=out_spec,
        out_shape=jax.ShapeDtypeStruct((n, NSA_HEADS * NSA_HD), F32),
        scratch_shapes=[pltpu.VMEM((hpg * NSA_TQ, LANES), F32), pltpu.VMEM((hpg * NSA_TQ, LANES), F32)],
        compiler_params=_cparams(("parallel", "parallel", "arbitrary")),
        name="nsa_attn",
    )(q, ks, vs, kwp, vwp, sel, tn, gates, ocmp)


def _mem_kv_kernel(mem_ref, g_ref, w_ref, gk_ref, k_ref, v_ref):
    mn = _rms(mem_ref[0], g_ref[...]).astype(BF16)
    kv = _dot(mn, w_ref[...])
    width = MEM_HEADS * MEM_HD
    for c in range(width // LANES):
        kn = _seg_rms64(kv[:, c * LANES:(c + 1) * LANES], gk_ref[...])
        vv = kv[:, width + c * LANES:width + (c + 1) * LANES]
        for half in range(2):
            sl = slice(half * MEM_HD, (half + 1) * MEM_HD)
            k_ref[0, 2 * c + half] = kn[:, sl].astype(BF16)
            v_ref[0, 2 * c + half] = vv[:, sl].astype(BF16)


def _mem_kv(mem, g, w, gk2):
    b, m, d = mem.shape
    full = lambda a: pl.BlockSpec(a.shape, lambda i: (0,) * a.ndim)
    spec = pl.BlockSpec((1, MEM_HEADS, m, MEM_HD), lambda i: (i, 0, 0, 0))
    sds = jax.ShapeDtypeStruct((b, MEM_HEADS, m, MEM_HD), BF16)
    return pl.pallas_call(
        _mem_kv_kernel,
        grid=(b,),
        in_specs=[pl.BlockSpec((1, m, d), lambda i: (i, 0, 0)), full(g), full(w), full(gk2)],
        out_specs=[spec, spec],
        out_shape=[sds, sds],
        compiler_params=_cparams(("parallel",)),
        name="mem_kv",
    )(mem, g, w, gk2)


def _post_attn_kernel(x_ref, ya_ref, yb_ref, goa_ref, gob_ref, woa_ref, wob_ref, gmn_ref, wmq_ref, gmq_ref,
                      km_ref, vm_ref, wmo_ref, gmoe_ref, rwh_ref, rwl_ref, rb_ref,
                      x2_ref, h2_ref, topi_ref, gate_ref, cnt_ref):
    first = (pl.program_id(0) == 0) & (pl.program_id(1) == 0)

    @pl.when(first)
    def _():
        cnt_ref[...] = jnp.zeros(cnt_ref.shape, F32)

    mixa = _rms(ya_ref[...], goa_ref[...]).astype(BF16)
    mixb = _rms(yb_ref[...], gob_ref[...]).astype(BF16)
    x1 = x_ref[...] + _dot(mixa, woa_ref[...]) + _dot(mixb, wob_ref[...])

    h = _rms(x1, gmn_ref[...]).astype(BF16)
    q = _dot(h, wmq_ref[...])
    scale = MEM_HD ** -0.5
    outs = []
    for c in range(MEM_HEADS * MEM_HD // LANES):
        qn = _seg_rms64(q[:, c * LANES:(c + 1) * LANES], gmq_ref[...]) * scale
        for half in range(2):
            hd = 2 * c + half
            qh = qn[:, half * MEM_HD:(half + 1) * MEM_HD].astype(BF16)
            s = _dot_nt(qh, km_ref[0, hd])
            e = jnp.exp(s - jnp.max(s, axis=-1, keepdims=True))
            p = e / jnp.sum(e, axis=-1, keepdims=True)
            outs.append(_dot(p.astype(BF16), vm_ref[0, hd]))
    o = jnp.concatenate(outs, axis=1).astype(BF16)
    x2 = x1 + _dot(o, wmo_ref[...])
    x2_ref[...] = x2

    h2 = _rms(x2, gmoe_ref[...])
    for c in range(ROW_TILE):
        h2_ref[pl.ds(c, h2.shape[0], stride=ROW_TILE), :] = h2[:, c * LANES:(c + 1) * LANES]
    h_hi = h2.astype(BF16)
    h_lo = (h2 - h_hi.astype(F32)).astype(BF16)
    logits = _dot(h_hi, rwh_ref[...]) + _dot(h_hi, rwl_ref[...]) + _dot(h_lo, rwh_ref[...]) + rb_ref[...]
    lane = lax.broadcasted_iota(jnp.int32, logits.shape, 1)
    lane_f = lane.astype(F32)
    topi = jnp.zeros(logits.shape, jnp.int32)
    topv = jnp.full(logits.shape, NEG_INF, F32)
    onehot = jnp.zeros(logits.shape, F32)
    for k in range(TOP_K):
        best = jnp.max(logits, axis=-1, keepdims=True)
        first_idx = jnp.min(jnp.where(logits == best, lane_f, float(LANES)), axis=-1, keepdims=True)
        hit = lane_f == first_idx
        topi = jnp.where(lane == k, first_idx.astype(jnp.int32), topi)
        topv = jnp.where(lane == k, best, topv)
        onehot = jnp.where(hit, 1.0, onehot)
        logits = jnp.where(hit, -jnp.inf, logits)
    e = jnp.where(lane < TOP_K, jnp.exp(topv - jnp.max(topv, axis=-1, keepdims=True)), 0.0)
    gate_ref[...] = e / jnp.sum(e, axis=-1, keepdims=True)
    topi_ref[...] = topi
    cnt_ref[...] += jnp.sum(onehot, axis=0, keepdims=True)


def _post_attn(x2d, ya, yb, goa, gob, woa, wob, gmn, wmq, gmq2, km, vm, wmo, gmoe, rwh, rwl, rb, batch, seq):
    n, d = x2d.shape
    tm = 256
    ns = seq // tm
    full = lambda a: pl.BlockSpec(a.shape, lambda b, i: (0,) * a.ndim)
    tok = lambda width: pl.BlockSpec((tm, width), lambda b, i: (b * ns + i, 0))
    mem_spec = pl.BlockSpec((1,) + km.shape[1:], lambda b, i: (b, 0, 0, 0))
    return pl.pallas_call(
        _post_attn_kernel,
        grid=(batch, ns),
        in_specs=[tok(d), tok(ya.shape[1]), tok(yb.shape[1]), full(goa), full(gob), full(woa), full(wob),
                  full(gmn), full(wmq), full(gmq2), mem_spec, mem_spec, full(wmo), full(gmoe),
                  full(rwh), full(rwl), full(rb)],
        out_specs=[tok(d), pl.BlockSpec((tm * ROW_TILE, LANES), lambda b, i: (b * ns + i, 0)),
                   tok(LANES), tok(LANES), pl.BlockSpec((1, LANES), lambda b, i: (0, 0))],
        out_shape=[jax.ShapeDtypeStruct((n, d), F32),
                   jax.ShapeDtypeStruct((n * ROW_TILE, LANES), F32),
                   jax.ShapeDtypeStruct((n, LANES), jnp.int32),
                   jax.ShapeDtypeStruct((n, LANES), F32),
                   jax.ShapeDtypeStruct((1, LANES), F32)],
        compiler_params=_cparams(("arbitrary", "arbitrary")),
        name="post_attn",
    )(x2d, ya, yb, goa, gob, woa, wob, gmn, wmq, gmq2, km, vm, wmo, gmoe, rwh, rwl, rb)


def _moe_pos_kernel(topi_ref, start_ref, tri_ref, pos_ref, carry_ref):
    @pl.when(pl.program_id(0) == 0)
    def _():
        carry_ref[...] = jnp.zeros(carry_ref.shape, F32)

    topi = topi_ref[...]
    lane = lax.broadcasted_iota(jnp.int32, topi.shape, 1)
    hits = [lane == topi[:, k:k + 1] for k in range(TOP_K)]
    onehot = sum(h.astype(F32) for h in hits)
    before = _dot(tri_ref[...], onehot.astype(BF16))
    base = start_ref[...] + carry_ref[...] + before
    pos = jnp.zeros(topi.shape, jnp.int32)
    for k in range(TOP_K):
        pk = jnp.sum(jnp.where(hits[k], base, 0.0), axis=-1, keepdims=True).astype(jnp.int32)
        pos = jnp.where(lane == k, pk, pos)
    pos_ref[...] = pos
    carry_ref[...] += jnp.sum(onehot, axis=0, keepdims=True)


def _moe_pos(topi, pad_start, tri):
    n = topi.shape[0]
    tm = tri.shape[0]
    return pl.pallas_call(
        _moe_pos_kernel,
        grid=(n // tm,),
        in_specs=[pl.BlockSpec((tm, LANES), lambda i: (i, 0)),
                  pl.BlockSpec((1, LANES), lambda i: (0, 0)),
                  pl.BlockSpec((tm, tm), lambda i: (0, 0))],
        out_specs=pl.BlockSpec((tm, LANES), lambda i: (i, 0)),
        out_shape=jax.ShapeDtypeStruct((n, LANES), jnp.int32),
        scratch_shapes=[pltpu.VMEM((1, LANES), F32)],
        compiler_params=_cparams(("arbitrary",)),
        name="moe_pos",
    )(topi, pad_start, tri)


def _moe_scatter_kernel(pos_ref, h_ref, zero_ref, xs_ref, sem, *, tm):
    del zero_ref

    def copy(t, k):
        dst = pl.multiple_of(pos_ref[t * TOP_K + k] * ROW_TILE, ROW_TILE)
        return pltpu.make_async_copy(h_ref.at[pl.ds(pl.multiple_of(t * ROW_TILE, ROW_TILE), ROW_TILE)],
                                     xs_ref.at[pl.ds(dst, ROW_TILE)], sem)

    def issue(t, c):
        for k in range(TOP_K):
            copy(t, k).start(priority=k % 2)
        return c

    def drain(t, c):
        for k in range(TOP_K):
            copy(t, k).wait()
        return c

    lax.fori_loop(0, tm, issue, 0, unroll=4)
    lax.fori_loop(0, tm, drain, 0)


def _moe_scatter(pos_flat, h2, xs_zero):
    n = h2.shape[0] // ROW_TILE
    tm = 512
    kern = functools.partial(_moe_scatter_kernel, tm=tm)
    return pl.pallas_call(
        kern,
        grid=(n // tm,),
        in_specs=[pl.BlockSpec((tm * TOP_K,), lambda i: (i,), memory_space=pltpu.SMEM),
                  pl.BlockSpec((tm * ROW_TILE, LANES), lambda i: (i, 0)),
                  pl.BlockSpec(memory_space=pl.ANY)],
        out_specs=pl.BlockSpec(memory_space=pl.ANY),
        out_shape=jax.ShapeDtypeStruct(xs_zero.shape, xs_zero.dtype),
        scratch_shapes=[pltpu.SemaphoreType.DMA(())],
        input_output_aliases={2: 0},
        compiler_params=_cparams(("arbitrary",)),
        name="moe_scatter",
    )(pos_flat, h2, xs_zero)


GU_GROUP = 2 * LANES


def _moe_wprep_kernel(w_ref, perm_ref, o_ref):
    w = w_ref[0].astype(BF16)
    for c in range(w.shape[1] // GU_GROUP):
        sl = slice(c * GU_GROUP, (c + 1) * GU_GROUP)
        o_ref[0, :, sl] = _dot(w[:, sl], perm_ref[...]).astype(BF16)


def _moe_wprep(w_gate_up, perm):
    e, d, width = w_gate_up.shape
    tk = 512
    return pl.pallas_call(
        _moe_wprep_kernel,
        grid=(e, d // tk),
        in_specs=[pl.BlockSpec((1, tk, width), lambda i, j: (i, j, 0)),
                  pl.BlockSpec(perm.shape, lambda i, j: (0, 0))],
        out_specs=pl.BlockSpec((1, tk, width), lambda i, j: (i, j, 0)),
        out_shape=jax.ShapeDtypeStruct((e, d, width), BF16),
        compiler_params=_cparams(("parallel", "parallel")),
        name="moe_wprep",
    )(w_gate_up, perm)


def _moe_ffn_kernel(blk_e_ref, n_used_ref, x_ref, wgu_ref, bg_ref, bu_ref, wd_ref, bd_ref, y_ref, wd_bf_ref):
    i = pl.program_id(0)
    chunk = lambda c: pl.ds(c, MOE_BLK, stride=ROW_TILE)
    new_expert = (i == 0) | (blk_e_ref[i] != blk_e_ref[jnp.maximum(i - 1, 0)])

    @pl.when(new_expert & (i < n_used_ref[0]))
    def _():
        wd_bf_ref[...] = wd_ref[0].astype(BF16)

    @pl.when(i < n_used_ref[0])
    def _():
        x = jnp.concatenate([x_ref[chunk(c), :] for c in range(ROW_TILE)], axis=1).astype(BF16)
        gu = _dot(x, wgu_ref[0])
        bg, bu = bg_ref[0], bu_ref[0]
        acts = []
        for c in range(gu.shape[1] // GU_GROUP):
            fs = slice(c * LANES, (c + 1) * LANES)
            gate = jnp.minimum(gu[:, c * GU_GROUP:c * GU_GROUP + LANES] + bg[:, fs], SWIGLU_LIMIT)
            up = jnp.clip(gu[:, c * GU_GROUP + LANES:(c + 1) * GU_GROUP] + bu[:, fs], -SWIGLU_LIMIT, SWIGLU_LIMIT)
            acts.append(((up + 1.0) * gate * jax.nn.sigmoid(SWIGLU_ALPHA * gate)).astype(BF16))
        act = jnp.concatenate(acts, axis=1)
        y = _dot(act, wd_bf_ref[...]) + bd_ref[0]
        for c in range(ROW_TILE):
            y_ref[chunk(c), :] = y[:, c * LANES:(c + 1) * LANES]

    @pl.when(i >= n_used_ref[0])
    def _():
        y_ref[...] = jnp.zeros(y_ref.shape, F32)


def _moe_ffn(blk_e, n_used, xs, wgu, bg, bu, wd, bd):
    p = xs.shape[0] // ROW_TILE
    dff, d = wd.shape[1:]
    n_blk = p // MOE_BLK
    w_spec = lambda r, c: pl.BlockSpec((1, r, c), lambda i, be, nu: (be[i], 0, 0))
    row_spec = pl.BlockSpec((MOE_BLK * ROW_TILE, LANES), lambda i, be, nu: (i, 0))
    grid_spec = pltpu.PrefetchScalarGridSpec(
        num_scalar_prefetch=2,
        grid=(n_blk,),
        in_specs=[row_spec,
                  w_spec(d, 2 * dff), w_spec(1, dff), w_spec(1, dff),
                  w_spec(dff, d), w_spec(1, d)],
        out_specs=row_spec,
        scratch_shapes=[pltpu.VMEM((dff, d), BF16)],
    )
    return pl.pallas_call(
        _moe_ffn_kernel,
        grid_spec=grid_spec,
        out_shape=jax.ShapeDtypeStruct(xs.shape, F32),
        compiler_params=_cparams(("arbitrary",)),
        name="moe_ffn",
    )(blk_e, n_used, xs, wgu, bg, bu, wd, bd)


def _moe_combine_kernel(pos_ref, gate_ref, x_ref, ys_ref, o_ref, buf_ref, sem, *, tm):
    def copy(t, k):
        src = pl.multiple_of(pos_ref[t * TOP_K + k] * ROW_TILE, ROW_TILE)
        return pltpu.make_async_copy(ys_ref.at[pl.ds(src, ROW_TILE)],
                                     buf_ref.at[k, pl.ds(pl.multiple_of(t * ROW_TILE, ROW_TILE), ROW_TILE)], sem)

    def issue(t, c):
        for k in range(TOP_K):
            copy(t, k).start(priority=k % 2)
        return c

    def drain(t, c):
        for k in range(TOP_K):
            copy(t, k).wait()
        return c

    lax.fori_loop(0, tm, issue, 0, unroll=4)
    lax.fori_loop(0, tm, drain, 0)
    gates = gate_ref[...]
    for c in range(ROW_TILE):
        cs = slice(c * LANES, (c + 1) * LANES)
        acc = x_ref[:, cs]
        for k in range(TOP_K):
            acc = acc + gates[:, k:k + 1] * buf_ref[k, pl.ds(c, tm, stride=ROW_TILE), :]
        o_ref[:, cs] = acc


def _moe_combine(pos_flat, gates, x2, ys):
    n, d = x2.shape
    tm = 256
    kern = functools.partial(_moe_combine_kernel, tm=tm)
    return pl.pallas_call(
        kern,
        grid=(n // tm,),
        in_specs=[pl.BlockSpec((tm * TOP_K,), lambda i: (i,), memory_space=pltpu.SMEM),
                  pl.BlockSpec((tm, LANES), lambda i: (i, 0)),
                  pl.BlockSpec((tm, d), lambda i: (i, 0)),
                  pl.BlockSpec(memory_space=pl.ANY)],
        out_specs=pl.BlockSpec((tm, d), lambda i: (i, 0)),
        out_shape=jax.ShapeDtypeStruct((n, d), F32),
        scratch_shapes=[pltpu.VMEM((TOP_K, tm * ROW_TILE, LANES), F32), pltpu.SemaphoreType.DMA(())],
        compiler_params=_cparams(("arbitrary",)),
        name="moe_combine",
    )(pos_flat, gates, x2, ys)


def _rel_bucket_np(dist):
    n = np.maximum(dist, 0)
    nf = np.maximum(n, 1).astype(np.float32)
    ratio = np.log(nf / np.float32(REL_MAX_EXACT)) / np.float32(math.log(REL_MAX_DIST / REL_MAX_EXACT))
    large = REL_MAX_EXACT + (ratio * np.float32(REL_BUCKETS - REL_MAX_EXACT)).astype(np.int32)
    large = np.minimum(large, REL_BUCKETS - 1)
    return np.where(n < REL_MAX_EXACT, n, large)


def _bias_tables_kernel(tbl_ref, bc_ref, bt_ref, tc_ref, tn_ref):
    h = pl.program_id(0)
    bc, bt = bc_ref[...], bt_ref[...]
    tc = jnp.zeros(bc.shape, F32)
    tn = jnp.zeros(bt.shape, F32)
    far = tbl_ref[REL_BUCKETS - 1, h]
    for b in range(REL_BUCKETS - 1):
        v = (tbl_ref[b, h] - far) * LOG2E
        tc = jnp.where(bc == b, v, tc)
        tn = jnp.where(bt == b, v, tn)
    tc_ref[:, 0] = tc
    tn_ref[0] = tn


def _bias_tables(rel_table):
    q = np.arange(NSA_TQS)[None, :, None]
    blocks_per_tile = NSA_TQS // CMP_STRIDE
    n_res = LANES // blocks_per_tile
    res = np.arange(n_res)[:, None, None]
    base_gap = (blocks_per_tile * res - CMP_WINDOW_BACK) % LANES + CMP_WINDOW_BACK
    w = np.arange(2 * LANES)[None, None, :]
    b_cmp = _rel_bucket_np(CMP_STRIDE * base_gap + q - CMP_STRIDE * w - (CMP_LEN - 1)).astype(np.int32)
    c = np.arange(2 * NSA_TK + WIN)[None, :]
    b_tok = _rel_bucket_np(np.arange(NSA_TQ)[:, None] + WIN - (c - NSA_TK)).astype(np.int32)
    heads = rel_table.shape[1]
    return pl.pallas_call(
        _bias_tables_kernel,
        grid=(heads,),
        in_specs=[pl.BlockSpec(memory_space=pltpu.SMEM),
                  pl.BlockSpec(b_cmp.shape, lambda h: (0, 0, 0)),
                  pl.BlockSpec(b_tok.shape, lambda h: (0, 0))],
        out_specs=[pl.BlockSpec((n_res, 1) + b_cmp.shape[1:], lambda h: (0, h, 0, 0)),
                   pl.BlockSpec((1,) + b_tok.shape, lambda h: (h, 0, 0))],
        out_shape=[jax.ShapeDtypeStruct((n_res, heads) + b_cmp.shape[1:], F32),
                   jax.ShapeDtypeStruct((heads,) + b_tok.shape, F32)],
        compiler_params=_cparams(("parallel",)),
        name="bias_tables",
    )(rel_table.astype(F32), jnp.asarray(b_cmp), jnp.asarray(b_tok))


def kernel(x, mem, g_attn_norm, w_in, g_cq, w_uq, g_ckv, w_ukv, g_q_mla, g_k_mla, cmp_k_pos, cmp_k_w1, cmp_k_w2, cmp_v_pos, cmp_v_w1, cmp_v_w2, g_q_nsa, g_k_nsa, rel_table, g_out_mla, g_out_nsa, w_out, g_mem_norm, g_mem_src, w_mq, w_mkv, g_mq, g_mk, w_mo, g_moe_norm, router_w, router_b, w_gate_up, b_gate_up, w_down, b_down):
    batch, seq, d = x.shape
    n = batch * seq
    depth = w_in.shape[0]
    assert seq % 512 == 0 and seq // SLC_LEN <= LANES and d == ROW_TILE * LANES
    row = lambda v: v.reshape(1, -1).astype(F32)
    tile2 = lambda v: jnp.concatenate([v, v]).reshape(1, -1).astype(F32)

    x2d = x.reshape(n, d)
    for l in range(depth):
        wi = w_in[l]
        w_in_r = jnp.concatenate(
            [wi[:, 0:384], wi[:, 416:928], wi[:, 928:1696], wi[:, 384:416], wi[:, 1696:1720],
             jnp.zeros((d, IN_COLS_PAD - 1720), wi.dtype)], axis=1).astype(BF16)
        dq = MLA_NOPE + MLA_ROPE
        wuq_r = jnp.pad(w_uq[l].reshape(MLA_Q_RANK, MLA_HEADS, dq), ((0, 0), (0, 0), (0, LANES - dq)))
        wuq_r = wuq_r.reshape(MLA_Q_RANK, MLA_HEADS * LANES).astype(BF16)
        wukv = w_ukv[l].reshape(MLA_KV_RANK, MLA_HEADS, MLA_NOPE + MLA_V)
        wuk_r = jnp.pad(wukv[:, :, :MLA_NOPE], ((0, 0), (0, 0), (0, LANES - MLA_NOPE)))
        wuk_r = wuk_r.reshape(MLA_KV_RANK, MLA_HEADS * LANES).astype(BF16)
        wuv_r = wukv[:, :, MLA_NOPE:].reshape(MLA_KV_RANK, MLA_HEADS * MLA_V).astype(BF16)
        gq_pad = jnp.pad(g_q_mla[l], (0, LANES - dq)).reshape(1, LANES)
        gk_pad = jnp.pad(g_k_mla[l], (0, LANES - dq)).reshape(1, LANES)

        half = MLA_ROPE // 2
        inv = ROPE_THETA ** (-jnp.arange(half, dtype=F32) / half)
        ang = jnp.arange(seq, dtype=F32)[:, None] * inv
        cos, sin = jnp.cos(ang), jnp.sin(ang)
        zeros = jnp.zeros((seq, half), F32)
        rc = jnp.concatenate([jnp.ones((seq, MLA_NOPE), F32), cos, cos, jnp.zeros((seq, LANES - dq), F32)], 1)
        rs1 = jnp.concatenate([jnp.zeros((seq, MLA_NOPE), F32), -sin, zeros, jnp.zeros((seq, LANES - dq), F32)], 1)
        rs2 = jnp.concatenate([jnp.zeros((seq, MLA_NOPE), F32), zeros, sin, jnp.zeros((seq, LANES - dq), F32)], 1)

        mla_in, qn, kv6, misc = _in_proj(x2d, row(g_attn_norm[l]), w_in_r)
        q_m, k_m, v_m = _mla_prep(mla_in, misc, row(g_cq[l]), wuq_r, row(g_ckv[l]), wuk_r, wuv_r,
                                  gq_pad, gk_pad, rc, rs1, rs2, seq)
        y_mla = _mla_attn(q_m, k_m, v_m, batch, seq)

        nc = seq // CMP_STRIDE
        half_len = CMP_LEN // 2
        eye_g = jnp.eye(NSA_KV_GROUPS, dtype=F32)

        def cmp_weights(pos, w1, w2):
            out = []
            for part in range(2):
                sl = slice(part * half_len, (part + 1) * half_len)
                out.append(jnp.broadcast_to(pos[sl][:, None, :], (half_len, NSA_KV_GROUPS, NSA_HD))
                           .reshape(1, -1))
            for part in range(2):
                sl = slice(part * half_len, (part + 1) * half_len)
                wexp = jnp.einsum('ldf,gh->lgdhf', w1[sl], eye_g)
                out.append(wexp.reshape(half_len * NSA_KV_GROUPS * NSA_HD, NSA_KV_GROUPS * CMP_HIDDEN).astype(BF16))
            out.append(w2.astype(BF16))
            return out

        chunk_w = CMP_STRIDE * NSA_KV_GROUPS * NSA_HD
        kcmp, vcmp = _nsa_cmp(kv6[0].reshape(batch, nc, chunk_w), kv6[1].reshape(batch, nc, chunk_w),
                              cmp_weights(cmp_k_pos[l], cmp_k_w1[l], cmp_k_w2[l]),
                              cmp_weights(cmp_v_pos[l], cmp_v_w1[l], cmp_v_w2[l]), row(g_k_nsa[l]))
        q_n, ks, vs, kw, vw, gates_n = _nsa_prep(qn, kv6, misc, tile2(g_q_nsa[l]), tile2(g_k_nsa[l]), seq)
        by_batch = lambda t: t.reshape(NSA_KV_GROUPS, batch, seq, t.shape[-1])
        front_pad = lambda t: jnp.pad(by_batch(t), ((0, 0), (0, 0), (WIN, 0), (0, 0)))
        tc, tn = _bias_tables(rel_table)
        n_idx = np.arange(nc)[:, None]
        j_idx = np.arange(LANES)[None, :]
        ovl = ((CMP_STRIDE * n_idx < SLC_LEN * j_idx + SLC_LEN)
               & (CMP_STRIDE * n_idx + CMP_LEN - 1 >= SLC_LEN * j_idx)
               & (n_idx < nc - 1) & (j_idx < seq // SLC_LEN)).astype(np.float32)
        o_cmp, sel = _nsa_select(q_n, kcmp, vcmp, tc, jnp.asarray(ovl, dtype=BF16), batch, seq)
        y_nsa = _nsa_attn(q_n, by_batch(ks), by_batch(vs), front_pad(kw), front_pad(vw), sel,
                          tn, gates_n, o_cmp, batch, seq)

        k_mem, v_mem = _mem_kv(mem, row(g_mem_src[l]), w_mkv[l].reshape(d, 2 * MEM_HEADS * MEM_HD).astype(BF16),
                               tile2(g_mk[l]))
        rw = jnp.pad(router_w[l], ((0, 0), (0, LANES - N_EXPERTS)))
        rw_hi = rw.astype(BF16)
        rw_lo = (rw - rw_hi.astype(F32)).astype(BF16)
        rb =jnp.pad(router_b[l], (0, LANES - N_EXPERTS), constant_values=NEG_INF).reshape(1, LANES)
        wo = w_out[l].astype(BF16)
        n_mla = MLA_HEADS * MLA_V
        x2, h2, topi, gates_e, counts = _post_attn(
            x2d, y_mla, y_nsa, row(g_out_mla[l]), row(g_out_nsa[l]), wo[:n_mla], wo[n_mla:],
            row(g_mem_norm[l]), w_mq[l].astype(BF16), tile2(g_mq[l]), k_mem, v_mem, w_mo[l].astype(BF16),
            row(g_moe_norm[l]), rw_hi, rw_lo, rb, batch, seq)

        cnt = counts[0].astype(jnp.int32)
        padded = (cnt + MOE_BLK - 1) // MOE_BLK * MOE_BLK
        pad_end = jnp.cumsum(padded)
        pad_start = (pad_end - padded).astype(F32).reshape(1, LANES)
        p_rows = (n * TOP_K // MOE_BLK + N_EXPERTS) * MOE_BLK
        n_blk = p_rows // MOE_BLK
        blk_first_row = jnp.arange(n_blk, dtype=jnp.int32) * MOE_BLK
        blk_e = jnp.minimum(jnp.sum(pad_end[None, :N_EXPERTS] <= blk_first_row[:, None], axis=1),
                            N_EXPERTS - 1).astype(jnp.int32)
        n_used = (pad_end[N_EXPERTS - 1] // MOE_BLK).astype(jnp.int32).reshape(1)
        tm_pos = 256
        tri = (np.arange(tm_pos)[None, :] < np.arange(tm_pos)[:, None]).astype(np.float32)
        pos = _moe_pos(topi, pad_start, jnp.asarray(tri, dtype=BF16))
        pos_flat = pos[:, :TOP_K].reshape(n * TOP_K)
        xs = _moe_scatter(pos_flat, h2, jnp.zeros((p_rows * ROW_TILE, LANES), F32))
        src = np.arange(GU_GROUP)
        perm = (np.arange(GU_GROUP)[:, None] == np.where(src < LANES, 2 * src, 2 * (src - LANES) + 1)[None, :])
        wgu = _moe_wprep(w_gate_up[l], jnp.asarray(perm.astype(np.float32), dtype=BF16))
        bgu = b_gate_up[l]
        ys = _moe_ffn(blk_e, n_used, xs, wgu, bgu[:, None, 0::2], bgu[:, None, 1::2],
                      w_down[l], b_down[l][:, None, :])
        x2d = _moe_combine(pos_flat, gates_e, x2, ys)
    return x2d.reshape(batch, seq, d)
```

```python
import functools
import math

import numpy as np
import jax
import jax.numpy as jnp
from jax import lax
from jax.experimental import pallas as pl
from jax.experimental.pallas import tpu as pltpu

F32 = jnp.float32
BF16 = jnp.bfloat16

EPS = 1e-6
NEG_INF = -1e30
LANES = 128

MLA_HEADS = 8
MLA_NOPE = 64
MLA_ROPE = 32
MLA_V = 64
MLA_Q_RANK = 256
MLA_KV_RANK = 128
ROPE_THETA = 10000.0

NSA_HEADS = 8
NSA_KV_GROUPS = 2
NSA_HPG = NSA_HEADS // NSA_KV_GROUPS
NSA_HD = 64
CMP_LEN = 32
CMP_STRIDE = 16
CMP_HIDDEN = 128
SLC_LEN = 64
SLC_TOPK = 16
WIN = 512
FORCE_SCORE = 1e9
NSA_TQ = 256
NSA_TQS = 512
NSA_TK = 512
CMP_WINDOW_BACK = 40

REL_BUCKETS = 32
REL_MAX_EXACT = 16
REL_MAX_DIST = 512

MEM_HEADS = 4
MEM_HD = 64

N_EXPERTS = 32
TOP_K = 4
SWIGLU_LIMIT = 7.0
SWIGLU_ALPHA = 1.702
MOE_BLK = 256
ROW_TILE = 8

LOG2E = math.log2(math.e)
MASK_BIG = 2.0 ** 100

VMEM_LIMIT = 56 * 1024 * 1024


def _cparams(sem, vmem=VMEM_LIMIT):
    return pltpu.CompilerParams(dimension_semantics=sem, vmem_limit_bytes=vmem)


def _rms(x, g):
    return x * lax.rsqrt(jnp.mean(x * x, axis=-1, keepdims=True) + EPS) * g


def _seg_rms64(t, g2):
    lane = lax.broadcasted_iota(jnp.int32, t.shape, 1)
    sq = t * t
    lo = jnp.sum(jnp.where(lane < 64, sq, 0.0), axis=-1, keepdims=True)
    hi = jnp.sum(jnp.where(lane >= 64, sq, 0.0), axis=-1, keepdims=True)
    ms = jnp.where(lane < 64, lo, hi) * (1.0 / 64.0)
    return t * lax.rsqrt(ms + EPS) * g2


def _dot(a, b):
    return jnp.dot(a, b, preferred_element_type=F32)


def _with_ones(v):
    lane = lax.broadcasted_iota(jnp.int32, v.shape, 1)
    return jnp.concatenate([v, jnp.where(lane == 0, 1.0, 0.0).astype(v.dtype)], axis=1)


def _dot_nt(a, b):
    return lax.dot_general(a, b, (((1,), (1,)), ((), ())), preferred_element_type=F32)


def _softmax_weights(s, m_old):
    cols = [s[:, c * LANES:(c + 1) * LANES] for c in range(s.shape[1] // LANES)]
    m_new = jnp.maximum(m_old, jnp.max(functools.reduce(jnp.maximum, cols), axis=-1, keepdims=True))
    return m_new, jnp.concatenate([jnp.exp2((c - m_new).astype(BF16)) for c in cols], axis=1)


IN_COLS_PAD = 1792


def _in_proj_kernel(x_ref, g_ref, w_ref, mla_ref, qn_ref, kv_ref, misc_ref):
    h = _rms(x_ref[...], g_ref[...])
    p = _dot(h.astype(BF16), w_ref[...])
    mla_ref[...] = p[:, 0:384]
    qn_ref[...] = p[:, 384:896]
    for j in range(6):
        kv_ref[j] = p[:, 896 + 128 * j:1024 + 128 * j]
    misc_ref[...] = p[:, 1664:1792]


def _in_proj(x2d, g, w):
    n, d = x2d.shape
    tm = 512
    return pl.pallas_call(
        _in_proj_kernel,
        grid=(n // tm,),
        in_specs=[pl.BlockSpec((tm, d), lambda i: (i, 0)),
                  pl.BlockSpec((1, d), lambda i: (0, 0)),
                  pl.BlockSpec((d, IN_COLS_PAD), lambda i: (0, 0))],
        out_specs=[pl.BlockSpec((tm, 384), lambda i: (i, 0)),
                   pl.BlockSpec((tm, 512), lambda i: (i, 0)),
                   pl.BlockSpec((6, tm, 128), lambda i: (0, i, 0)),
                   pl.BlockSpec((tm, 128), lambda i: (i, 0))],
        out_shape=[jax.ShapeDtypeStruct((n, 384), F32),
                   jax.ShapeDtypeStruct((n, 512), F32),
                   jax.ShapeDtypeStruct((6, n, 128), F32),
                   jax.ShapeDtypeStruct((n, 128), F32)],
        compiler_params=_cparams(("parallel",)),
        name="in_proj",
    )(x2d, g, w)


def _mla_prep_kernel(mla_ref, misc_ref, gcq_ref, wuq_ref, wuqs_ref, gckv_ref, wuk_ref, wuv_ref,
                     cq0_ref, cq1_ref, ck0_ref, ck1_ref, q_ref, k_ref, v_ref):
    p = mla_ref[...]
    cqn = _rms(p[:, 0:MLA_Q_RANK], gcq_ref[...]).astype(BF16)
    ckvn = _rms(p[:, MLA_Q_RANK:MLA_Q_RANK + MLA_KV_RANK], gckv_ref[...]).astype(BF16)
    qall = _dot(cqn, wuq_ref[...])
    qswp = _dot(cqn, wuqs_ref[...])
    kall = _dot(ckvn, wuk_ref[...])
    vall = _dot(ckvn, wuv_ref[...])
    misc = misc_ref[...]
    lane = lax.broadcasted_iota(jnp.int32, misc.shape, 1)
    half = MLA_ROPE // 2
    in_rope = (lane >= MLA_NOPE) & (lane < MLA_NOPE + MLA_ROPE)
    krope = jnp.where(in_rope, pltpu.roll(misc, MLA_NOPE, axis=1), 0.0)
    kswp = jnp.where(lane < MLA_NOPE + half, pltpu.roll(misc, MLA_NOPE - half, axis=1),
                     pltpu.roll(misc, MLA_NOPE + half, axis=1))
    cq0, cq1, ck0 = cq0_ref[...], cq1_ref[...], ck0_ref[...]
    kswp_term = jnp.where(in_rope, kswp, 0.0) * ck1_ref[...]
    ones_col = jnp.where(lane == MLA_V, 1.0, 0.0)
    inv_dk = 1.0 / (MLA_NOPE + MLA_ROPE)

    def inv_rms(t):
        return lax.rsqrt(jnp.sum(t * t, axis=-1, keepdims=True) * inv_dk + EPS)

    for h in range(MLA_HEADS):
        hs = slice(h * LANES, (h + 1) * LANES)
        tq = qall[:, hs]
        q_ref[h] = (inv_rms(tq) * (tq * cq0 + qswp[:, hs] * cq1)).astype(BF16)
        tk = kall[:, hs] + krope
        k_ref[h] = (inv_rms(tk) * (tk * ck0 + kswp_term)).astype(BF16)
        v_ref[h] = (vall[:, hs] + ones_col).astype(BF16)


def _mla_prep(mla, misc, gcq, wuq, wuqs, gckv, wuk, wuv, cq0, cq1, ck0, ck1, seq):
    n = mla.shape[0]
    tm = 256
    ns = seq // tm
    full = lambda a: pl.BlockSpec(a.shape, lambda i: (0,) * a.ndim)
    rope_spec = pl.BlockSpec((tm, LANES), lambda i: (i % ns, 0))
    head_spec = pl.BlockSpec((MLA_HEADS, tm, LANES), lambda i: (0, i, 0))
    head_sds = jax.ShapeDtypeStruct((MLA_HEADS, n, LANES), BF16)
    return pl.pallas_call(
        _mla_prep_kernel,
        grid=(n // tm,),
        in_specs=[pl.BlockSpec((tm, 384), lambda i: (i, 0)),
                  pl.BlockSpec((tm, LANES), lambda i: (i, 0)),
                  full(gcq), full(wuq), full(wuqs), full(gckv), full(wuk), full(wuv),
                  rope_spec, rope_spec, rope_spec, rope_spec],
        out_specs=[head_spec, head_spec, head_spec],
        out_shape=[head_sds, head_sds, head_sds],
        compiler_params=_cparams(("parallel",)),
        name="mla_prep",
    )(mla, misc, gcq, wuq, wuqs, gckv, wuk, wuv, cq0, cq1, ck0, ck1)


def _mla_attn_kernel(q_ref, k_ref, v_ref, o_ref, m_ref, acc_ref):
    qi, ki = pl.program_id(1), pl.program_id(2)

    @pl.when(ki == 0)
    def _():
        m_ref[...] = jnp.full(m_ref.shape, NEG_INF, F32)
        acc_ref[...] = jnp.zeros(acc_ref.shape, F32)

    def step(masked):
        def head(h, carry):
            s = _dot_nt(q_ref[h], k_ref[h])
            if masked:
                row = lax.broadcasted_iota(jnp.int32, s.shape, 0)
                col = lax.broadcasted_iota(jnp.int32, s.shape, 1)
                s = jnp.where(col <= row, s, -jnp.inf)
            m_old = m_ref[h]
            m_new, p = _softmax_weights(s, m_old)
            acc_ref[h] = jnp.exp2(m_old - m_new) * acc_ref[h] + _dot(p, v_ref[h])
            m_ref[h] = m_new
            return carry
        lax.fori_loop(0, MLA_HEADS, head, 0, unroll=True)

    @pl.when(ki < qi)
    def _():
        step(False)

    @pl.when(ki == qi)
    def _():
        step(True)

    @pl.when(ki == pl.num_programs(2) - 1)
    def _():
        for h in range(MLA_HEADS):
            acc = acc_ref[h]
            o_ref[:, h * MLA_V:(h + 1) * MLA_V] = acc[:, :MLA_V] / acc[:, MLA_V:MLA_V + 1]


def _mla_attn(q, k, v, batch, seq):
    n = q.shape[1]
    tq = tk = 512
    nq = seq // tq
    kv_spec = pl.BlockSpec((MLA_HEADS, tk, LANES), lambda b, i, j: (0, b * nq + jnp.minimum(i, j), 0))
    return pl.pallas_call(
        _mla_attn_kernel,
        grid=(batch, nq, nq),
        in_specs=[pl.BlockSpec((MLA_HEADS, tq, LANES), lambda b, i, j: (0, b * nq + i, 0)), kv_spec, kv_spec],
        out_specs=pl.BlockSpec((tq, MLA_HEADS * MLA_V), lambda b, i, j: (b * nq + i, 0)),
        out_shape=jax.ShapeDtypeStruct((n, MLA_HEADS * MLA_V), F32),
        scratch_shapes=[pltpu.VMEM((MLA_HEADS, tq, LANES), F32),
                        pltpu.VMEM((MLA_HEADS, tq, LANES), F32)],
        compiler_params=_cparams(("parallel", "parallel", "arbitrary")),
        name="mla_attn",
    )(q, k, v)


def _nsa_cmp_kernel(kc_ref, vc_ref, pak_ref, pbk_ref, wak_ref, wbk_ref, w2k_ref,
                    pav_ref, pbv_ref, wav_ref, wbv_ref, w2v_ref, gk_ref, kout_ref, vout_ref):
    def compress(chunks, pa, pb, wa, wb, w2):
        nc = chunks.shape[0]
        ha = _dot((chunks + pa).astype(BF16), wa)
        hb = _dot((chunks + pb).astype(BF16), wb)
        hid = jax.nn.gelu(ha + pltpu.roll(hb, nc - 1, axis=0))
        return [_dot(hid[:, g * CMP_HIDDEN:(g + 1) * CMP_HIDDEN].astype(BF16), w2)
                for g in range(NSA_KV_GROUPS)]

    kc = compress(kc_ref[0], pak_ref[...], pbk_ref[...], wak_ref[...], wbk_ref[...], w2k_ref[...])
    vc = compress(vc_ref[0], pav_ref[...], pbv_ref[...], wav_ref[...], wbv_ref[...], w2v_ref[...])
    for g in range(NSA_KV_GROUPS):
        kn = _rms(kc[g], gk_ref[...])
        kout_ref[0, g] = jnp.concatenate([kn, jnp.zeros_like(kn)], axis=1).astype(BF16)
        vout_ref[0, g] = vc[g].astype(BF16)


def _nsa_cmp(kc_chunks, vc_chunks, wk, wv, gk):
    b, nc, width = kc_chunks.shape
    full = lambda a: pl.BlockSpec(a.shape, lambda i: (0,) * a.ndim)
    chunk_spec = pl.BlockSpec((1, nc, width), lambda i: (i, 0, 0))
    out_spec = lambda width: pl.BlockSpec((1, NSA_KV_GROUPS, nc, width), lambda i: (i, 0, 0, 0))
    out_sds = lambda width: jax.ShapeDtypeStruct((b, NSA_KV_GROUPS, nc, width), BF16)
    return pl.pallas_call(
        _nsa_cmp_kernel,
        grid=(b,),
        in_specs=[chunk_spec, chunk_spec] + [full(a) for a in wk] + [full(a) for a in wv] + [full(gk)],
        out_specs=[out_spec(LANES), out_spec(NSA_HD)],
        out_shape=[out_sds(LANES), out_sds(NSA_HD)],
        compiler_params=_cparams(("parallel",)),
        name="nsa_cmp",
    )(kc_chunks, vc_chunks, *wk, *wv, gk)


def _nsa_prep_kernel(qn_ref, ks_ref, vs_ref, kw_ref, vw_ref, misc_ref, gq_ref, gk_ref,
                     q_ref, kso_ref, vso_ref, kwo_ref, vwo_ref, gate_ref, *, seq):
    gq2, gk2 = gq_ref[...], gk_ref[...]
    tm = qn_ref.shape[0]
    scale = NSA_HD ** -0.5 * LOG2E
    zpad = jnp.zeros((tm, NSA_HD), F32)
    for c in range(NSA_HEADS // 2):
        t = _seg_rms64(qn_ref[:, c * LANES:(c + 1) * LANES], gq2) * scale
        q_ref[2 * c] = jnp.concatenate([t[:, :NSA_HD], zpad], axis=1).astype(BF16)
        q_ref[2 * c + 1] = jnp.concatenate([t[:, NSA_HD:], zpad], axis=1).astype(BF16)
    ksn = _seg_rms64(ks_ref[0], gk2)
    kwn = _seg_rms64(kw_ref[0], gk2)
    vs, vw = vs_ref[0], vw_ref[0]
    row = lax.broadcasted_iota(jnp.int32, (tm, LANES), 0)
    lane = lax.broadcasted_iota(jnp.int32, (tm, LANES), 1)
    pos = (pl.program_id(0) * tm) % seq + row
    neg_onehot = jnp.where(lane == pos // SLC_LEN, -1.0, 0.0)
    for g in range(NSA_KV_GROUPS):
        sl = slice(g * NSA_HD, (g + 1) * NSA_HD)
        kso_ref[g] = jnp.concatenate([neg_onehot, ksn[:, sl], zpad], axis=1).astype(BF16)
        kwo_ref[g] = jnp.concatenate([kwn[:, sl], zpad], axis=1).astype(BF16)
        vso_ref[g] = _with_ones(vs[:, sl]).astype(BF16)
        vwo_ref[g] = _with_ones(vw[:, sl]).astype(BF16)
    sig = jax.nn.sigmoid(misc_ref[...])
    per_group = 3 * NSA_HPG
    for g in range(NSA_KV_GROUPS):
        gate_ref[g] = pltpu.roll(sig, LANES - (MLA_ROPE + per_group * g), axis=1)


def _nsa_prep(qn, kv, misc, gq2, gk2, seq):
    n = qn.shape[0]
    tm = 512
    full = lambda a: pl.BlockSpec(a.shape, lambda i: (0,) * a.ndim)
    kv_spec = lambda j: pl.BlockSpec((1, tm, LANES), lambda i, j=j: (j, i, 0))
    g_spec = lambda width: pl.BlockSpec((NSA_KV_GROUPS, tm, width), lambda i: (0, i, 0))
    g_sds = lambda width: jax.ShapeDtypeStruct((NSA_KV_GROUPS, n, width), BF16)
    return pl.pallas_call(
        functools.partial(_nsa_prep_kernel, seq=seq),
        grid=(n // tm,),
        in_specs=[pl.BlockSpec((tm, 512), lambda i: (i, 0)),
                  kv_spec(2), kv_spec(3), kv_spec(4), kv_spec(5),
                  pl.BlockSpec((tm, LANES), lambda i: (i, 0)), full(gq2), full(gk2)],
        out_specs=[pl.BlockSpec((NSA_HEADS, tm, LANES), lambda i: (0, i, 0)),
                   g_spec(2 * LANES), g_spec(LANES), g_spec(LANES), g_spec(LANES),
                   pl.BlockSpec((NSA_KV_GROUPS, tm, LANES), lambda i: (0, i, 0))],
        out_shape=[jax.ShapeDtypeStruct((NSA_HEADS, n, LANES), BF16),
                   g_sds(2 * LANES), g_sds(LANES), g_sds(LANES), g_sds(LANES),
                   jax.ShapeDtypeStruct((NSA_KV_GROUPS, n, LANES), F32)],
        compiler_params=_cparams(("parallel",)),
        name="nsa_prep",
    )(qn, kv, kv, kv, kv, misc, gq2, gk2)


def _nsa_select_kernel(q_ref, kc_ref, vc_ref, tc_ref, ovl_ref, ocmp_ref, sel_ref, s_ref, *, nc, n_sel):
    tq, hpg = NSA_TQS, NSA_HPG
    i = pl.program_id(2)
    q0 = i * tq
    ncb = nc // LANES
    s_ref[...] = _dot_nt(q_ref[...].reshape(hpg * tq, LANES), kc_ref[0, 0])

    first_blk = ((tq // CMP_STRIDE) * i + LANES - CMP_WINDOW_BACK) // LANES - 1
    for half in range(2):
        blk = first_blk + half

        @pl.when((blk >= 0) & (blk < ncb))
        def _():
            c0 = pl.multiple_of(blk * LANES, LANES)
            for h in range(hpg):
                s_ref[h * tq:(h + 1) * tq, pl.ds(c0, LANES)] += tc_ref[0, h, :, half * LANES:(half + 1) * LANES]

    row = lax.broadcasted_iota(jnp.int32, (tq, LANES), 0)
    lane = lax.broadcasted_iota(jnp.int32, (tq, LANES), 1)
    qpos = q0 + row
    valid = [(CMP_STRIDE * (c * LANES + lane) + (CMP_LEN - 1)) <= qpos for c in range(ncb)]
    psum = [jnp.zeros((tq, LANES), F32) for _ in range(ncb)]
    vc = vc_ref[0, 0]
    for h in range(hpg):
        rs = slice(h * tq, (h + 1) * tq)
        cols = [jnp.where(valid[c], s_ref[rs, c * LANES:(c + 1) * LANES], -jnp.inf) for c in range(ncb)]
        m = jnp.maximum(jnp.max(functools.reduce(jnp.maximum, cols), axis=-1, keepdims=True), NEG_INF)
        es = [jnp.exp2(c - m) for c in cols]
        inv = 1.0 / jnp.maximum(jnp.sum(functools.reduce(jnp.add, es), axis=-1, keepdims=True), 1e-30)
        ps = [e * inv for e in es]
        psum = [a + p for a, p in zip(psum, ps)]
        ocmp_ref[:, h * NSA_HD:(h + 1) * NSA_HD] = _dot(
            jnp.concatenate([p.astype(BF16) for p in ps], axis=1), vc)

    pcat = jnp.concatenate(psum, axis=1)
    hi = pcat.astype(BF16)
    rest = pcat - hi.astype(F32)
    mid = rest.astype(BF16)
    lo = (rest - mid.astype(F32)).astype(BF16)
    ovl = ovl_ref[...]
    imp = _dot(hi, ovl) + _dot(mid, ovl) + _dot(lo, ovl)

    jf = lane.astype(F32)
    cur = qpos // SLC_LEN
    forced = (lane == 0) | (lane == cur) | (lane == cur - 1)
    imp = jnp.where(forced, FORCE_SCORE, imp)
    imp = jnp.where(lane * SLC_LEN <= qpos, imp, NEG_INF)
    for _ in range(n_sel):
        best = jnp.max(imp, axis=-1, keepdims=True)
        first = jnp.min(jnp.where(imp == best, jf, float(LANES)), axis=-1, keepdims=True)
        imp = jnp.where(jf == first, -jnp.inf, imp)
    sel_ref[0] = jnp.where(imp == -jnp.inf, 0.0, MASK_BIG).astype(BF16)


def _nsa_select(q, kcmp, vcmp, tc, ovl, batch, seq):
    n = q.shape[1]
    nc = kcmp.shape[2]
    nq = seq // NSA_TQS
    n_res = tc.shape[0]
    n_sel = min(SLC_TOPK, seq // SLC_LEN)
    g_, hpg = NSA_KV_GROUPS, NSA_HPG
    kern = functools.partial(_nsa_select_kernel, nc=nc, n_sel=n_sel)
    cmp_spec = lambda width: pl.BlockSpec((1, 1, nc, width), lambda b, g, i: (b, g, 0, 0))
    return pl.pallas_call(
        kern,
        grid=(batch, g_, nq),
        in_specs=[pl.BlockSpec((hpg, NSA_TQS, LANES), lambda b, g, i: (g, b * nq + i, 0)),
                  cmp_spec(LANES), cmp_spec(NSA_HD),
                  pl.BlockSpec((1, hpg, NSA_TQS, 2 * LANES), lambda b, g, i: (i % n_res, g, 0, 0)),
                  pl.BlockSpec(ovl.shape, lambda b, g, i: (0, 0))],
        out_specs=[pl.BlockSpec((NSA_TQS, hpg * NSA_HD), lambda b, g, i: (b * nq + i, g)),
                   pl.BlockSpec((1, NSA_TQS, LANES), lambda b, g, i: (g, b * nq + i, 0))],
        out_shape=[jax.ShapeDtypeStruct((n, NSA_HEADS * NSA_HD), F32),
                   jax.ShapeDtypeStruct((g_, n, LANES), BF16)],
        scratch_shapes=[pltpu.VMEM((hpg * NSA_TQS, nc), F32)],
        compiler_params=_cparams(("parallel", "parallel", "arbitrary")),
        name="nsa_select",
    )(q, kcmp, vcmp, tc, ovl)


def _nsa_attn_kernel(q_ref, ks_ref, vs_ref, kw_ref, vw_ref, sel_ref, tn_ref, gate_ref, ocmp_ref,
                     o_ref, m_ref, acc_ref):
    tq, tk, hpg = NSA_TQ, NSA_TK, NSA_HPG
    rows = hpg * tq
    i = pl.program_id(2)
    q0 = i * tq
    q4 = q_ref[...].reshape(rows, LANES)
    q_aug = jnp.concatenate([jnp.concatenate([sel_ref[0]] * hpg, axis=0), q4], axis=1)
    row = lax.broadcasted_iota(jnp.int32, (tq, tk), 0)
    col = lax.broadcasted_iota(jnp.int32, (tq, tk), 1)
    m_ref[...] = jnp.full(m_ref.shape, NEG_INF, F32)
    acc_ref[...] = jnp.zeros(acc_ref.shape, F32)

    def tile(kt, near):
        k0 = pl.multiple_of(kt * tk, tk)
        s = _dot_nt(q_aug, ks_ref[0, 0, pl.ds(k0, tk), :])
        if near:
            causal = (k0 + col) <= (q0 + row)
            start = pl.multiple_of(k0 + tk + WIN - q0, LANES)
        ps, alphas = [], []
        for h in range(hpg):
            rs = slice(h * tq, (h + 1) * tq)
            sh = s[rs]
            if near:
                sh = jnp.where(causal, sh + tn_ref[h, :, pl.ds(start, tk)], -jnp.inf)
            m_old = m_ref[rs]
            m_new, p = _softmax_weights(sh, m_old)
            m_ref[rs] = m_new
            alphas.append(jnp.exp2(m_old - m_new))
            ps.append(p)
        acc_ref[...] = (jnp.concatenate(alphas, axis=0) * acc_ref[...]
                        + _dot(jnp.concatenate(ps, axis=0), vs_ref[0, 0, pl.ds(k0, tk), :]))

    last = q0 // tk
    n_far = jnp.maximum(last - 1, 0)
    def far_pair(pair, c):
        tile(2 * pair, False)
        tile(2 * pair + 1, False)
        return c

    lax.fori_loop(0, n_far // 2, far_pair, 0)
    lax.fori_loop(n_far // 2 * 2, n_far, lambda kt, c: (tile(kt, False), c)[1], 0)
    lax.fori_loop(n_far, last + 1, lambda kt, c: (tile(kt, True), c)[1], 0)

    wk = WIN + tq
    w0 = pl.multiple_of(q0, tq)
    s = _dot_nt(q4, kw_ref[0, 0, pl.ds(w0, wk), :])
    roww = lax.broadcasted_iota(jnp.int32, (tq, wk), 0)
    colw = lax.broadcasted_iota(jnp.int32, (tq, wk), 1)
    dist = roww + WIN - colw
    validw = (dist >= 0) & (dist < WIN) & (q0 - WIN + colw >= 0)
    ps = []
    for h in range(hpg):
        sh = s[h * tq:(h + 1) * tq] + tn_ref[h, :, tk:tk + wk]
        ps.append(_softmax_weights(jnp.where(validw, sh, -jnp.inf), NEG_INF)[1])
    accw = _dot(jnp.concatenate(ps, axis=0), vw_ref[0, 0, pl.ds(w0, wk), :])

    gates = gate_ref[0]
    acc = acc_ref[...]
    outs = []
    for h in range(hpg):
        rs = slice(h * tq, (h + 1) * tq)
        o_cmp = ocmp_ref[:, h * NSA_HD:(h + 1) * NSA_HD]
        o_slc = acc[rs, :NSA_HD] / jnp.maximum(acc[rs, NSA_HD:NSA_HD + 1], 1e-30)
        o_win = accw[rs, :NSA_HD] / accw[rs, NSA_HD:NSA_HD + 1]
        outs.append(gates[:, 3 * h:3 * h + 1] * o_cmp + gates[:, 3 * h + 1:3 * h + 2] * o_slc
                    + gates[:, 3 * h + 2:3 * h + 3] * o_win)
    o_ref[...] = jnp.concatenate(outs, axis=1)


def _nsa_attn(q, ks, vs, kwp, vwp, sel, tn, gates, ocmp, batch, seq):
    n = q.shape[1]
    nq = seq // NSA_TQ
    g_, hpg = NSA_KV_GROUPS, NSA_HPG
    seq_spec = lambda length, width: pl.BlockSpec((1, 1, length, width), lambda b, g, i: (g, b, 0, 0))
    tok_spec = pl.BlockSpec((1, NSA_TQ, LANES), lambda b, g, i: (g, b * nq + i, 0))
    out_spec = pl.BlockSpec((NSA_TQ, hpg * NSA_HD), lambda b, g, i: (b * nq + i, g))
    return pl.pallas_call(
        _nsa_attn_kernel,
        grid=(batch, g_, nq),
        in_specs=[pl.BlockSpec((hpg, NSA_TQ, LANES), lambda b, g, i: (g, b * nq + i, 0)),
                  seq_spec(seq, 2 * LANES), seq_spec(seq, LANES), seq_spec(seq + WIN, LANES), seq_spec(seq + WIN, LANES),
                  tok_spec,
                  pl.BlockSpec((hpg, NSA_TQ, 2 * NSA_TK + WIN), lambda b, g, i: (g, 0, 0)),
                  tok_spec, out_spec],
        out_specs=out_spec,
        out_shape=jax.ShapeDtypeStruct((n, NSA_HEADS * NSA_HD), F32),
        scratch_shapes=[pltpu.VMEM((hpg * NSA_TQ, LANES), F32), pltpu.VMEM((hpg * NSA_TQ, LANES), F32)],
        compiler_params=_cparams(("parallel", "parallel", "arbitrary")),
        name="nsa_attn",
    )(q, ks, vs, kwp, vwp, sel, tn, gates, ocmp)


def _mem_kv_kernel(mem_ref, g_ref, w_ref, gk_ref, k_ref, v_ref):
    mn = _rms(mem_ref[0], g_ref[...]).astype(BF16)
    kv = _dot(mn, w_ref[...])
    width = MEM_HEADS * MEM_HD
    for c in range(width // LANES):
        kn = _seg_rms64(kv[:, c * LANES:(c + 1) * LANES], gk_ref[...])
        vv = kv[:, width + c * LANES:width + (c + 1) * LANES]
        for half in range(2):
            sl = slice(half * MEM_HD, (half + 1) * MEM_HD)
            k_ref[0, 2 * c + half] = kn[:, sl].astype(BF16)
            v_ref[0, 2 * c + half] = vv[:, sl].astype(BF16)


def _mem_kv(mem, g, w, gk2):
    b, m, d = mem.shape
    full = lambda a: pl.BlockSpec(a.shape, lambda i: (0,) * a.ndim)
    spec = pl.BlockSpec((1, MEM_HEADS, m, MEM_HD), lambda i: (i, 0, 0, 0))
    sds = jax.ShapeDtypeStruct((b, MEM_HEADS, m, MEM_HD), BF16)
    return pl.pallas_call(
        _mem_kv_kernel,
        grid=(b,),
        in_specs=[pl.BlockSpec((1, m, d), lambda i: (i, 0, 0)), full(g), full(w), full(gk2)],
        out_specs=[spec, spec],
        out_shape=[sds, sds],
        compiler_params=_cparams(("parallel",)),
        name="mem_kv",
    )(mem, g, w, gk2)


def _post_attn_kernel(x_ref, ya_ref, yb_ref, goa_ref, gob_ref, woa_ref, wob_ref, gmn_ref, wmq_ref, gmq_ref,
                      km_ref, vm_ref, wmo_ref, gmoe_ref, rwh_ref, rwl_ref, rb_ref,
                      x2_ref, h2_ref, topi_ref, gate_ref, cnt_ref):
    first = (pl.program_id(0) == 0) & (pl.program_id(1) == 0)

    @pl.when(first)
    def _():
        cnt_ref[...] = jnp.zeros(cnt_ref.shape, F32)

    mixa = _rms(ya_ref[...], goa_ref[...]).astype(BF16)
    mixb = _rms(yb_ref[...], gob_ref[...]).astype(BF16)
    x1 = x_ref[...] + _dot(mixa, woa_ref[...]) + _dot(mixb, wob_ref[...])

    h = _rms(x1, gmn_ref[...]).astype(BF16)
    q = _dot(h, wmq_ref[...])
    scale = MEM_HD ** -0.5
    outs = []
    for c in range(MEM_HEADS * MEM_HD // LANES):
        qn = _seg_rms64(q[:, c * LANES:(c + 1) * LANES], gmq_ref[...]) * scale
        for half in range(2):
            hd = 2 * c + half
            qh = qn[:, half * MEM_HD:(half + 1) * MEM_HD].astype(BF16)
            s = _dot_nt(qh, km_ref[0, hd])
            e = jnp.exp(s - jnp.max(s, axis=-1, keepdims=True))
            p = e / jnp.sum(e, axis=-1, keepdims=True)
            outs.append(_dot(p.astype(BF16), vm_ref[0, hd]))
    o = jnp.concatenate(outs, axis=1).astype(BF16)
    x2 = x1 + _dot(o, wmo_ref[...])
    x2_ref[...] = x2

    h2 = _rms(x2, gmoe_ref[...])
    for c in range(ROW_TILE):
        h2_ref[pl.ds(c, h2.shape[0], stride=ROW_TILE), :] = h2[:, c * LANES:(c + 1) * LANES]
    h_hi = h2.astype(BF16)
    h_lo = (h2 - h_hi.astype(F32)).astype(BF16)
    logits = _dot(h_hi, rwh_ref[...]) + _dot(h_hi, rwl_ref[...]) + _dot(h_lo, rwh_ref[...]) + rb_ref[...]
    lane = lax.broadcasted_iota(jnp.int32, logits.shape, 1)
    lane_f = lane.astype(F32)
    topi = jnp.zeros(logits.shape, jnp.int32)
    topv = jnp.full(logits.shape, NEG_INF, F32)
    onehot = jnp.zeros(logits.shape, F32)
    for k in range(TOP_K):
        best = jnp.max(logits, axis=-1, keepdims=True)
        first_idx = jnp.min(jnp.where(logits == best, lane_f, float(LANES)), axis=-1, keepdims=True)
        hit = lane_f == first_idx
        topi = jnp.where(lane == k, first_idx.astype(jnp.int32), topi)
        topv = jnp.where(lane == k, best, topv)
        onehot = jnp.where(hit, 1.0, onehot)
        logits = jnp.where(hit, -jnp.inf, logits)
    e = jnp.where(lane < TOP_K, jnp.exp(topv - jnp.max(topv, axis=-1, keepdims=True)), 0.0)
    gate_ref[...] = e / jnp.sum(e, axis=-1, keepdims=True)
    topi_ref[...] = topi
    cnt_ref[...] += jnp.sum(onehot, axis=0, keepdims=True)


def _post_attn(x2d, ya, yb, goa, gob, woa, wob, gmn, wmq, gmq2, km, vm, wmo, gmoe, rwh, rwl, rb, batch, seq):
    n, d = x2d.shape
    tm = 256
    ns = seq // tm
    full = lambda a: pl.BlockSpec(a.shape, lambda b, i: (0,) * a.ndim)
    tok = lambda width: pl.BlockSpec((tm, width), lambda b, i: (b * ns + i, 0))
    mem_spec = pl.BlockSpec((1,) + km.shape[1:], lambda b, i: (b, 0, 0, 0))
    return pl.pallas_call(
        _post_attn_kernel,
        grid=(batch, ns),
        in_specs=[tok(d), tok(ya.shape[1]), tok(yb.shape[1]), full(goa), full(gob), full(woa), full(wob),
                  full(gmn), full(wmq), full(gmq2), mem_spec, mem_spec, full(wmo), full(gmoe),
                  full(rwh), full(rwl), full(rb)],
        out_specs=[tok(d), pl.BlockSpec((tm * ROW_TILE, LANES), lambda b, i: (b * ns + i, 0)),
                   tok(LANES), tok(LANES), pl.BlockSpec((1, LANES), lambda b, i: (0, 0))],
        out_shape=[jax.ShapeDtypeStruct((n, d), F32),
                   jax.ShapeDtypeStruct((n * ROW_TILE, LANES), F32),
                   jax.ShapeDtypeStruct((n, LANES), jnp.int32),
                   jax.ShapeDtypeStruct((n, LANES), F32),
                   jax.ShapeDtypeStruct((1, LANES), F32)],
        compiler_params=_cparams(("arbitrary", "arbitrary")),
        name="post_attn",
    )(x2d, ya, yb, goa, gob, woa, wob, gmn, wmq, gmq2, km, vm, wmo, gmoe, rwh, rwl, rb)


def _moe_pos_kernel(topi_ref, start_ref, tri_ref, pos_ref, carry_ref):
    @pl.when(pl.program_id(0) == 0)
    def _():
        carry_ref[...] = jnp.zeros(carry_ref.shape, F32)

    topi = topi_ref[...]
    lane = lax.broadcasted_iota(jnp.int32, topi.shape, 1)
    hits = [lane == topi[:, k:k + 1] for k in range(TOP_K)]
    onehot = sum(h.astype(F32) for h in hits)
    before = _dot(tri_ref[...], onehot.astype(BF16))
    base = start_ref[...] + carry_ref[...] + before
    pos = jnp.zeros(topi.shape, jnp.int32)
    for k in range(TOP_K):
        pk = jnp.sum(jnp.where(hits[k], base, 0.0), axis=-1, keepdims=True).astype(jnp.int32)
        pos = jnp.where(lane == k, pk, pos)
    pos_ref[...] = pos
    carry_ref[...] += jnp.sum(onehot, axis=0, keepdims=True)


def _moe_pos(topi, pad_start, tri):
    n = topi.shape[0]
    tm = tri.shape[0]
    return pl.pallas_call(
        _moe_pos_kernel,
        grid=(n // tm,),
        in_specs=[pl.BlockSpec((tm, LANES), lambda i: (i, 0)),
                  pl.BlockSpec((1, LANES), lambda i: (0, 0)),
                  pl.BlockSpec((tm, tm), lambda i: (0, 0))],
        out_specs=pl.BlockSpec((tm, LANES), lambda i: (i, 0)),
        out_shape=jax.ShapeDtypeStruct((n, LANES), jnp.int32),
        scratch_shapes=[pltpu.VMEM((1, LANES), F32)],
        compiler_params=_cparams(("arbitrary",)),
        name="moe_pos",
    )(topi, pad_start, tri)


def _moe_scatter_kernel(pos_ref, h_ref, zero_ref, xs_ref, sem, *, tm):
    del zero_ref

    def copy(t, k):
        dst = pl.multiple_of(pos_ref[t * TOP_K + k] * ROW_TILE, ROW_TILE)
        return pltpu.make_async_copy(h_ref.at[pl.ds(pl.multiple_of(t * ROW_TILE, ROW_TILE), ROW_TILE)],
                                     xs_ref.at[pl.ds(dst, ROW_TILE)], sem)

    def issue(t, c):
        for k in range(TOP_K):
            copy(t, k).start(priority=k % 2)
        return c

    def drain(t, c):
        for k in range(TOP_K):
            copy(t, k).wait()
        return c

    lax.fori_loop(0, tm, issue, 0, unroll=4)
    lax.fori_loop(0, tm, drain, 0, unroll=4)


def _moe_scatter(pos_flat, h2, xs_zero):
    n = h2.shape[0] // ROW_TILE
    tm = 512
    kern = functools.partial(_moe_scatter_kernel, tm=tm)
    return pl.pallas_call(
        kern,
        grid=(n // tm,),
        in_specs=[pl.BlockSpec((tm * TOP_K,), lambda i: (i,), memory_space=pltpu.SMEM),
                  pl.BlockSpec((tm * ROW_TILE, LANES), lambda i: (i, 0)),
                  pl.BlockSpec(memory_space=pl.ANY)],
        out_specs=pl.BlockSpec(memory_space=pl.ANY),
        out_shape=jax.ShapeDtypeStruct(xs_zero.shape, xs_zero.dtype),
        scratch_shapes=[pltpu.SemaphoreType.DMA(())],
        input_output_aliases={2: 0},
        compiler_params=_cparams(("arbitrary",)),
        name="moe_scatter",
    )(pos_flat, h2, xs_zero)


GU_GROUP = 2 * LANES


def _moe_wprep_kernel(w_ref, perm_ref, o_ref):
    w = w_ref[0].astype(BF16)
    for c in range(w.shape[1] // GU_GROUP):
        sl = slice(c * GU_GROUP, (c + 1) * GU_GROUP)
        o_ref[0, :, sl] = _dot(w[:, sl], perm_ref[...]).astype(BF16)


def _moe_wprep(w_gate_up, perm):
    e, d, width = w_gate_up.shape
    tk = 512
    return pl.pallas_call(
        _moe_wprep_kernel,
        grid=(e, d // tk),
        in_specs=[pl.BlockSpec((1, tk, width), lambda i, j: (i, j, 0)),
                  pl.BlockSpec(perm.shape, lambda i, j: (0, 0))],
        out_specs=pl.BlockSpec((1, tk, width), lambda i, j: (i, j, 0)),
        out_shape=jax.ShapeDtypeStruct((e, d, width), BF16),
        compiler_params=_cparams(("parallel", "parallel")),
        name="moe_wprep",
    )(w_gate_up, perm)


def _moe_ffn_kernel(blk_e_ref, n_used_ref, x_ref, wgu_ref, bg_ref, bu_ref, wd_ref, bd_ref, y_ref, wd_bf_ref):
    i = pl.program_id(0)
    chunk = lambda c: pl.ds(c, MOE_BLK, stride=ROW_TILE)
    new_expert = (i == 0) | (blk_e_ref[i] != blk_e_ref[jnp.maximum(i - 1, 0)])

    @pl.when(new_expert & (i < n_used_ref[0]))
    def _():
        wd_bf_ref[...] = wd_ref[0].astype(BF16)

    @pl.when(i < n_used_ref[0])
    def _():
        x = jnp.concatenate([x_ref[chunk(c), :] for c in range(ROW_TILE)], axis=1).astype(BF16)
        gu = _dot(x, wgu_ref[0])
        bg, bu = bg_ref[0], bu_ref[0]
        acts = []
        for c in range(gu.shape[1] // GU_GROUP):
            fs = slice(c * LANES, (c + 1) * LANES)
            gate = jnp.minimum(gu[:, c * GU_GROUP:c * GU_GROUP + LANES] + bg[:, fs], SWIGLU_LIMIT)
            up = jnp.clip(gu[:, c * GU_GROUP + LANES:(c + 1) * GU_GROUP] + bu[:, fs], -SWIGLU_LIMIT, SWIGLU_LIMIT)
            acts.append(((up + 1.0) * gate * jax.nn.sigmoid(SWIGLU_ALPHA * gate)).astype(BF16))
        act = jnp.concatenate(acts, axis=1)
        y = _dot(act, wd_bf_ref[...]) + bd_ref[0]
        for c in range(ROW_TILE):
            y_ref[chunk(c), :] = y[:, c * LANES:(c + 1) * LANES]

    @pl.when(i >= n_used_ref[0])
    def _():
        y_ref[...] = jnp.zeros(y_ref.shape, F32)


def _moe_ffn(blk_e, n_used, xs, wgu, bg, bu, wd, bd):
    p = xs.shape[0] // ROW_TILE
    dff, d = wd.shape[1:]
    n_blk = p // MOE_BLK
    w_spec = lambda r, c: pl.BlockSpec((1, r, c), lambda i, be, nu: (be[i], 0, 0))
    row_spec = pl.BlockSpec((MOE_BLK * ROW_TILE, LANES), lambda i, be, nu: (i, 0))
    grid_spec = pltpu.PrefetchScalarGridSpec(
        num_scalar_prefetch=2,
        grid=(n_blk,),
        in_specs=[row_spec,
                  w_spec(d, 2 * dff), w_spec(1, dff), w_spec(1, dff),
                  w_spec(dff, d), w_spec(1, d)],
        out_specs=row_spec,
        scratch_shapes=[pltpu.VMEM((dff, d), BF16)],
    )
    return pl.pallas_call(
        _moe_ffn_kernel,
        grid_spec=grid_spec,
        out_shape=jax.ShapeDtypeStruct(xs.shape, F32),
        compiler_params=_cparams(("arbitrary",)),
        name="moe_ffn",
    )(blk_e, n_used, xs, wgu, bg, bu, wd, bd)


def _moe_combine_kernel(pos_ref, gate_ref, x_ref, ys_ref, o_ref, buf_ref, sem, *, tm):
    def copy(t, k):
        src = pl.multiple_of(pos_ref[t * TOP_K + k] * ROW_TILE, ROW_TILE)
        return pltpu.make_async_copy(ys_ref.at[pl.ds(src, ROW_TILE)],
                                     buf_ref.at[k, pl.ds(pl.multiple_of(t * ROW_TILE, ROW_TILE), ROW_TILE)], sem)

    def issue(t, c):
        for k in range(TOP_K):
            copy(t, k).start(priority=k % 2)
        return c

    def drain(t, c):
        for k in range(TOP_K):
            copy(t, k).wait()
        return c

    lax.fori_loop(0, tm, issue, 0, unroll=4)
    lax.fori_loop(0, tm, drain, 0, unroll=4)
    gates = gate_ref[...]
    for c in range(ROW_TILE):
        cs = slice(c * LANES, (c + 1) * LANES)
        acc = x_ref[:, cs]
        for k in range(TOP_K):
            acc = acc + gates[:, k:k + 1] * buf_ref[k, pl.ds(c, tm, stride=ROW_TILE), :]
        o_ref[:, cs] = acc


def _moe_combine(pos_flat, gates, x2, ys):
    n, d = x2.shape
    tm = 256
    kern = functools.partial(_moe_combine_kernel, tm=tm)
    return pl.pallas_call(
        kern,
        grid=(n // tm,),
        in_specs=[pl.BlockSpec((tm * TOP_K,), lambda i: (i,), memory_space=pltpu.SMEM),
                  pl.BlockSpec((tm, LANES), lambda i: (i, 0)),
                  pl.BlockSpec((tm, d), lambda i: (i, 0)),
                  pl.BlockSpec(memory_space=pl.ANY)],
        out_specs=pl.BlockSpec((tm, d), lambda i: (i, 0)),
        out_shape=jax.ShapeDtypeStruct((n, d), F32),
        scratch_shapes=[pltpu.VMEM((TOP_K, tm * ROW_TILE, LANES), F32), pltpu.SemaphoreType.DMA(())],
        compiler_params=_cparams(("arbitrary",)),
        name="moe_combine",
    )(pos_flat, gates, x2, ys)


def _rel_bucket_np(dist):
    n = np.maximum(dist, 0)
    nf = np.maximum(n, 1).astype(np.float32)
    ratio = np.log(nf / np.float32(REL_MAX_EXACT)) / np.float32(math.log(REL_MAX_DIST / REL_MAX_EXACT))
    large = REL_MAX_EXACT + (ratio * np.float32(REL_BUCKETS - REL_MAX_EXACT)).astype(np.int32)
    large = np.minimum(large, REL_BUCKETS - 1)
    return np.where(n < REL_MAX_EXACT, n, large)


def _bias_tables_kernel(tbl_ref, bc_ref, bt_ref, tc_ref, tn_ref):
    h = pl.program_id(0)
    bc, bt = bc_ref[...], bt_ref[...]
    tc = jnp.zeros(bc.shape, F32)
    tn = jnp.zeros(bt.shape, F32)
    far = tbl_ref[REL_BUCKETS - 1, h]
    for b in range(REL_BUCKETS - 1):
        v = (tbl_ref[b, h] - far) * LOG2E
        tc = jnp.where(bc == b, v, tc)
        tn = jnp.where(bt == b, v, tn)
    tc_ref[:, 0] = tc
    tn_ref[0] = tn


def _bias_tables(rel_table):
    q = np.arange(NSA_TQS)[None, :, None]
    blocks_per_tile = NSA_TQS // CMP_STRIDE
    n_res = LANES // blocks_per_tile
    res = np.arange(n_res)[:, None, None]
    base_gap = (blocks_per_tile * res - CMP_WINDOW_BACK) % LANES + CMP_WINDOW_BACK
    w = np.arange(2 * LANES)[None, None, :]
    b_cmp = _rel_bucket_np(CMP_STRIDE * base_gap + q - CMP_STRIDE * w - (CMP_LEN - 1)).astype(np.int32)
    c = np.arange(2 * NSA_TK + WIN)[None, :]
    b_tok = _rel_bucket_np(np.arange(NSA_TQ)[:, None] + WIN - (c - NSA_TK)).astype(np.int32)
    heads = rel_table.shape[1]
    return pl.pallas_call(
        _bias_tables_kernel,
        grid=(heads,),
        in_specs=[pl.BlockSpec(memory_space=pltpu.SMEM),
                  pl.BlockSpec(b_cmp.shape, lambda h: (0, 0, 0)),
                  pl.BlockSpec(b_tok.shape, lambda h: (0, 0))],
        out_specs=[pl.BlockSpec((n_res, 1) + b_cmp.shape[1:], lambda h: (0, h, 0, 0)),
                   pl.BlockSpec((1,) + b_tok.shape, lambda h: (h, 0, 0))],
        out_shape=[jax.ShapeDtypeStruct((n_res, heads) + b_cmp.shape[1:], F32),
                   jax.ShapeDtypeStruct((heads,) + b_tok.shape, F32)],
        compiler_params=_cparams(("parallel",)),
        name="bias_tables",
    )(rel_table.astype(F32), jnp.asarray(b_cmp), jnp.asarray(b_tok))


def kernel(x, mem, g_attn_norm, w_in, g_cq, w_uq, g_ckv, w_ukv, g_q_mla, g_k_mla, cmp_k_pos, cmp_k_w1, cmp_k_w2, cmp_v_pos, cmp_v_w1, cmp_v_w2, g_q_nsa, g_k_nsa, rel_table, g_out_mla, g_out_nsa, w_out, g_mem_norm, g_mem_src, w_mq, w_mkv, g_mq, g_mk, w_mo, g_moe_norm, router_w, router_b, w_gate_up, b_gate_up, w_down, b_down):
    batch, seq, d = x.shape
    n = batch * seq
    depth = w_in.shape[0]
    assert seq % 512 == 0 and seq // SLC_LEN <= LANES and d == ROW_TILE * LANES
    row = lambda v: v.reshape(1, -1).astype(F32)
    tile2 = lambda v: jnp.concatenate([v, v]).reshape(1, -1).astype(F32)

    x2d = x.reshape(n, d)
    for l in range(depth):
        wi = w_in[l]
        w_in_r = jnp.concatenate(
            [wi[:, 0:384], wi[:, 416:928], wi[:, 928:1696], wi[:, 384:416], wi[:, 1696:1720],
             jnp.zeros((d, IN_COLS_PAD - 1720), wi.dtype)], axis=1).astype(BF16)
        dq = MLA_NOPE + MLA_ROPE
        half = MLA_ROPE // 2
        x1, x2 = slice(MLA_NOPE, MLA_NOPE + half), slice(MLA_NOPE + half, dq)
        head_pad = lambda w, width: jnp.pad(w, ((0, 0), (0, 0), (0, LANES - width))).reshape(
            w.shape[0], MLA_HEADS * LANES).astype(BF16)
        wuq3 = w_uq[l].reshape(MLA_Q_RANK, MLA_HEADS, dq)
        wuq_r = head_pad(wuq3, dq)
        wuq_s = head_pad(jnp.concatenate([jnp.zeros_like(wuq3[:, :, :MLA_NOPE]), wuq3[:, :, x2], wuq3[:, :, x1]], 2), dq)
        wukv = w_ukv[l].reshape(MLA_KV_RANK, MLA_HEADS, MLA_NOPE + MLA_V)
        wuk_r = head_pad(wukv[:, :, :MLA_NOPE], MLA_NOPE)
        wuv_r = head_pad(wukv[:, :, MLA_NOPE:], MLA_V)

        inv = ROPE_THETA ** (-jnp.arange(half, dtype=F32) / half)
        ang = jnp.arange(seq, dtype=F32)[:, None] * inv
        cos, sin = jnp.cos(ang), jnp.sin(ang)
        lane_pad = jnp.zeros((seq, LANES - dq), F32)
        rot_c = jnp.concatenate([jnp.ones((seq, MLA_NOPE), F32), cos, cos, lane_pad], 1)
        rot_s = jnp.concatenate([jnp.zeros((seq, MLA_NOPE), F32), -sin, sin, lane_pad], 1)

        def rope_tables(g, scale):
            g_pad = jnp.pad(g, (0, LANES - dq))
            g_swp = jnp.pad(jnp.concatenate([g[:MLA_NOPE], g[x2], g[x1]]), (0, LANES - dq))
            return rot_c * (g_pad * scale), rot_s * (g_swp * scale)

        cq0, cq1 = rope_tables(g_q_mla[l], dq ** -0.5 * LOG2E)
        ck0, ck1 = rope_tables(g_k_mla[l], 1.0)

        mla_in, qn, kv6, misc = _in_proj(x2d, row(g_attn_norm[l]), w_in_r)
        q_m, k_m, v_m = _mla_prep(mla_in, misc, row(g_cq[l]), wuq_r, wuq_s, row(g_ckv[l]), wuk_r, wuv_r,
                                  cq0, cq1, ck0, ck1, seq)
        y_mla = _mla_attn(q_m, k_m, v_m, batch, seq)

        nc = seq // CMP_STRIDE
        half_len = CMP_LEN // 2
        eye_g = jnp.eye(NSA_KV_GROUPS, dtype=F32)

        def cmp_weights(pos, w1, w2):
            out = []
            for part in range(2):
                sl = slice(part * half_len, (part + 1) * half_len)
                out.append(jnp.broadcast_to(pos[sl][:, None, :], (half_len, NSA_KV_GROUPS, NSA_HD))
                           .reshape(1, -1))
            for part in range(2):
                sl = slice(part * half_len, (part + 1) * half_len)
                wexp = jnp.einsum('ldf,gh->lgdhf', w1[sl], eye_g)
                out.append(wexp.reshape(half_len * NSA_KV_GROUPS * NSA_HD, NSA_KV_GROUPS * CMP_HIDDEN).astype(BF16))
            out.append(w2.astype(BF16))
            return out

        chunk_w = CMP_STRIDE * NSA_KV_GROUPS * NSA_HD
        kcmp, vcmp = _nsa_cmp(kv6[0].reshape(batch, nc, chunk_w), kv6[1].reshape(batch, nc, chunk_w),
                              cmp_weights(cmp_k_pos[l], cmp_k_w1[l], cmp_k_w2[l]),
                              cmp_weights(cmp_v_pos[l], cmp_v_w1[l], cmp_v_w2[l]), row(g_k_nsa[l]))
        q_n, ks, vs, kw, vw, gates_n = _nsa_prep(qn, kv6, misc, tile2(g_q_nsa[l]), tile2(g_k_nsa[l]), seq)
        by_batch = lambda t: t.reshape(NSA_KV_GROUPS, batch, seq, t.shape[-1])
        front_pad = lambda t: jnp.pad(by_batch(t), ((0, 0), (0, 0), (WIN, 0), (0, 0)))
        tc, tn = _bias_tables(rel_table)
        n_idx = np.arange(nc)[:, None]
        j_idx = np.arange(LANES)[None, :]
        ovl = ((CMP_STRIDE * n_idx < SLC_LEN * j_idx + SLC_LEN)
               & (CMP_STRIDE * n_idx + CMP_LEN - 1 >= SLC_LEN * j_idx)
               & (n_idx < nc - 1) & (j_idx < seq // SLC_LEN)).astype(np.float32)
        o_cmp, sel = _nsa_select(q_n, kcmp, vcmp, tc, jnp.asarray(ovl, dtype=BF16), batch, seq)
        y_nsa = _nsa_attn(q_n, by_batch(ks), by_batch(vs), front_pad(kw), front_pad(vw), sel,
                          tn, gates_n, o_cmp, batch, seq)

        k_mem, v_mem = _mem_kv(mem, row(g_mem_src[l]), w_mkv[l].reshape(d, 2 * MEM_HEADS * MEM_HD).astype(BF16),
                               tile2(g_mk[l]))
        rw = jnp.pad(router_w[l], ((0, 0), (0, LANES - N_EXPERTS)))
        rw_hi = rw.astype(BF16)
        rw_lo = (rw - rw_hi.astype(F32)).astype(BF16)
        rb =jnp.pad(router_b[l], (0, LANES - N_EXPERTS), constant_values=NEG_INF).reshape(1, LANES)
        wo = w_out[l].astype(BF16)
        n_mla = MLA_HEADS * MLA_V
        x2, h2, topi, gates_e, counts = _post_attn(
            x2d, y_mla, y_nsa, row(g_out_mla[l]), row(g_out_nsa[l]), wo[:n_mla], wo[n_mla:],
            row(g_mem_norm[l]), w_mq[l].astype(BF16), tile2(g_mq[l]), k_mem, v_mem, w_mo[l].astype(BF16),
            row(g_moe_norm[l]), rw_hi, rw_lo, rb, batch, seq)

        cnt = counts[0].astype(jnp.int32)
        padded = (cnt + MOE_BLK - 1) // MOE_BLK * MOE_BLK
        pad_end = jnp.cumsum(padded)
        pad_start = (pad_end - padded).astype(F32).reshape(1, LANES)
        p_rows = (n * TOP_K // MOE_BLK + N_EXPERTS) * MOE_BLK
        n_blk = p_rows // MOE_BLK
        blk_first_row = jnp.arange(n_blk, dtype=jnp.int32) * MOE_BLK
        blk_e = jnp.minimum(jnp.sum(pad_end[None, :N_EXPERTS] <= blk_first_row[:, None], axis=1),
                            N_EXPERTS - 1).astype(jnp.int32)
        n_used = (pad_end[N_EXPERTS - 1] // MOE_BLK).astype(jnp.int32).reshape(1)
        tm_pos = 256
        tri = (np.arange(tm_pos)[None, :] < np.arange(tm_pos)[:, None]).astype(np.float32)
        pos = _moe_pos(topi, pad_start, jnp.asarray(tri, dtype=BF16))
        pos_flat = pos[:, :TOP_K].reshape(n * TOP_K)
        xs = _moe_scatter(pos_flat, h2, jnp.zeros((p_rows * ROW_TILE, LANES), F32))
        src = np.arange(GU_GROUP)
        perm = (np.arange(GU_GROUP)[:, None] == np.where(src < LANES, 2 * src, 2 * (src - LANES) + 1)[None, :])
        wgu = _moe_wprep(w_gate_up[l], jnp.asarray(perm.astype(np.float32), dtype=BF16))
        bgu = b_gate_up[l]
        ys = _moe_ffn(blk_e, n_used, xs, wgu, bgu[:, None, 0::2], bgu[:, None, 1::2],
                      w_down[l], b_down[l][:, None, :])
        x2d = _moe_combine(pos_flat, gates_e, x2, ys)
    return x2d.reshape(batch, seq, d)
```

```python
import functools
import math

import numpy as np
import jax
import jax.numpy as jnp
from jax import lax
from jax.experimental import pallas as pl
from jax.experimental.pallas import tpu as pltpu

F32 = jnp.float32
BF16 = jnp.bfloat16

EPS = 1e-6
NEG_INF = -1e30
LANES = 128

MLA_HEADS = 8
MLA_NOPE = 64
MLA_ROPE = 32
MLA_V = 64
MLA_Q_RANK = 256
MLA_KV_RANK = 128
ROPE_THETA = 10000.0

NSA_HEADS = 8
NSA_KV_GROUPS = 2
NSA_HPG = NSA_HEADS // NSA_KV_GROUPS
NSA_HD = 64
CMP_LEN = 32
CMP_STRIDE = 16
CMP_HIDDEN = 128
SLC_LEN = 64
SLC_TOPK = 16
WIN = 512
FORCE_SCORE = 1e9
NSA_TQ = 256
NSA_TQS = 512
NSA_TK = 512
CMP_WINDOW_BACK = 40

REL_BUCKETS = 32
REL_MAX_EXACT = 16
REL_MAX_DIST = 512

MEM_HEADS = 4
MEM_HD = 64

N_EXPERTS = 32
TOP_K = 4
SWIGLU_LIMIT = 7.0
SWIGLU_ALPHA = 1.702
MOE_BLK = 512
ROW_TILE = 8

LOG2E = math.log2(math.e)
MASK_BIG = 2.0 ** 100

VMEM_LIMIT = 56 * 1024 * 1024


def _cparams(sem, vmem=VMEM_LIMIT):
    return pltpu.CompilerParams(dimension_semantics=sem, vmem_limit_bytes=vmem)


def _rms(x, g):
    return x * lax.rsqrt(jnp.mean(x * x, axis=-1, keepdims=True) + EPS) * g


def _seg_rms64(t, g2):
    lane = lax.broadcasted_iota(jnp.int32, t.shape, 1)
    sq = t * t
    lo = jnp.sum(jnp.where(lane < 64, sq, 0.0), axis=-1, keepdims=True)
    hi = jnp.sum(jnp.where(lane >= 64, sq, 0.0), axis=-1, keepdims=True)
    ms = jnp.where(lane < 64, lo, hi) * (1.0 / 64.0)
    return t * lax.rsqrt(ms + EPS) * g2


def _dot(a, b):
    return jnp.dot(a, b, preferred_element_type=F32)


def _with_ones(v):
    lane = lax.broadcasted_iota(jnp.int32, v.shape, 1)
    return jnp.concatenate([v, jnp.where(lane == 0, 1.0, 0.0).astype(v.dtype)], axis=1)


def _dot_nt(a, b):
    return lax.dot_general(a, b, (((1,), (1,)), ((), ())), preferred_element_type=F32)


def _softmax_weights(s, m_old):
    cols = [s[:, c * LANES:(c + 1) * LANES] for c in range(s.shape[1] // LANES)]
    m_new = jnp.maximum(m_old, jnp.max(functools.reduce(jnp.maximum, cols), axis=-1, keepdims=True))
    return m_new, jnp.concatenate([jnp.exp2((c - m_new).astype(BF16)) for c in cols], axis=1)


IN_COLS_PAD = 1792


def _in_proj_kernel(x_ref, g_ref, w_ref, mla_ref, qn_ref, kv_ref, misc_ref):
    h = _rms(x_ref[...], g_ref[...])
    p = _dot(h.astype(BF16), w_ref[...])
    mla_ref[...] = p[:, 0:384]
    qn_ref[...] = p[:, 384:896]
    for j in range(6):
        kv_ref[j] = p[:, 896 + 128 * j:1024 + 128 * j]
    misc_ref[...] = p[:, 1664:1792]


def _in_proj(x2d, g, w):
    n, d = x2d.shape
    tm = 512
    return pl.pallas_call(
        _in_proj_kernel,
        grid=(n // tm,),
        in_specs=[pl.BlockSpec((tm, d), lambda i: (i, 0)),
                  pl.BlockSpec((1, d), lambda i: (0, 0)),
                  pl.BlockSpec((d, IN_COLS_PAD), lambda i: (0, 0))],
        out_specs=[pl.BlockSpec((tm, 384), lambda i: (i, 0)),
                   pl.BlockSpec((tm, 512), lambda i: (i, 0)),
                   pl.BlockSpec((6, tm, 128), lambda i: (0, i, 0)),
                   pl.BlockSpec((tm, 128), lambda i: (i, 0))],
        out_shape=[jax.ShapeDtypeStruct((n, 384), F32),
                   jax.ShapeDtypeStruct((n, 512), F32),
                   jax.ShapeDtypeStruct((6, n, 128), F32),
                   jax.ShapeDtypeStruct((n, 128), F32)],
        compiler_params=_cparams(("parallel",)),
        name="in_proj",
    )(x2d, g, w)


def _mla_prep_kernel(mla_ref, misc_ref, gcq_ref, wuq_ref, wuqs_ref, gckv_ref, wuk_ref, wuv_ref,
                     cq0_ref, cq1_ref, ck0_ref, ck1_ref, q_ref, k_ref, v_ref):
    p = mla_ref[...]
    cqn = _rms(p[:, 0:MLA_Q_RANK], gcq_ref[...]).astype(BF16)
    ckvn = _rms(p[:, MLA_Q_RANK:MLA_Q_RANK + MLA_KV_RANK], gckv_ref[...]).astype(BF16)
    qall = _dot(cqn, wuq_ref[...])
    qswp = _dot(cqn, wuqs_ref[...])
    kall = _dot(ckvn, wuk_ref[...])
    vall = _dot(ckvn, wuv_ref[...])
    misc = misc_ref[...]
    lane = lax.broadcasted_iota(jnp.int32, misc.shape, 1)
    half = MLA_ROPE // 2
    in_rope = (lane >= MLA_NOPE) & (lane < MLA_NOPE + MLA_ROPE)
    krope = jnp.where(in_rope, pltpu.roll(misc, MLA_NOPE, axis=1), 0.0)
    kswp = jnp.where(lane < MLA_NOPE + half, pltpu.roll(misc, MLA_NOPE - half, axis=1),
                     pltpu.roll(misc, MLA_NOPE + half, axis=1))
    cq0, cq1, ck0 = cq0_ref[...], cq1_ref[...], ck0_ref[...]
    kswp_term = jnp.where(in_rope, kswp, 0.0) * ck1_ref[...]
    ones_col = jnp.where(lane == MLA_V, 1.0, 0.0)
    inv_dk = 1.0 / (MLA_NOPE + MLA_ROPE)

    def inv_rms(t):
        return lax.rsqrt(jnp.sum(t * t, axis=-1, keepdims=True) * inv_dk + EPS)

    for h in range(MLA_HEADS):
        hs = slice(h * LANES, (h + 1) * LANES)
        tq = qall[:, hs]
        q_ref[h] = (inv_rms(tq) * (tq * cq0 + qswp[:, hs] * cq1)).astype(BF16)
        tk = kall[:, hs] + krope
        k_ref[h] = (inv_rms(tk) * (tk * ck0 + kswp_term)).astype(BF16)
        v_ref[h] = (vall[:, hs] + ones_col).astype(BF16)


def _mla_prep(mla, misc, gcq, wuq, wuqs, gckv, wuk, wuv, cq0, cq1, ck0, ck1, seq):
    n = mla.shape[0]
    tm = 256
    ns = seq // tm
    full = lambda a: pl.BlockSpec(a.shape, lambda i: (0,) * a.ndim)
    rope_spec = pl.BlockSpec((tm, LANES), lambda i: (i % ns, 0))
    head_spec = pl.BlockSpec((MLA_HEADS, tm, LANES), lambda i: (0, i, 0))
    head_sds = jax.ShapeDtypeStruct((MLA_HEADS, n, LANES), BF16)
    return pl.pallas_call(
        _mla_prep_kernel,
        grid=(n // tm,),
        in_specs=[pl.BlockSpec((tm, 384), lambda i: (i, 0)),
                  pl.BlockSpec((tm, LANES), lambda i: (i, 0)),
                  full(gcq), full(wuq), full(wuqs), full(gckv), full(wuk), full(wuv),
                  rope_spec, rope_spec, rope_spec, rope_spec],
        out_specs=[head_spec, head_spec, head_spec],
        out_shape=[head_sds, head_sds, head_sds],
        compiler_params=_cparams(("parallel",)),
        name="mla_prep",
    )(mla, misc, gcq, wuq, wuqs, gckv, wuk, wuv, cq0, cq1, ck0, ck1)


def _mla_attn_kernel(q_ref, k_ref, v_ref, o_ref, m_ref, acc_ref):
    qi, ki = pl.program_id(1), pl.program_id(2)

    @pl.when(ki == 0)
    def _():
        m_ref[...] = jnp.full(m_ref.shape, NEG_INF, F32)
        acc_ref[...] = jnp.zeros(acc_ref.shape, F32)

    def step(masked):
        def head(h, carry):
            s = _dot_nt(q_ref[h], k_ref[h])
            if masked:
                row = lax.broadcasted_iota(jnp.int32, s.shape, 0)
                col = lax.broadcasted_iota(jnp.int32, s.shape, 1)
                s = jnp.where(col <= row, s, -jnp.inf)
            m_old = m_ref[h]
            m_new, p = _softmax_weights(s, m_old)
            acc_ref[h] = jnp.exp2(m_old - m_new) * acc_ref[h] + _dot(p, v_ref[h])
            m_ref[h] = m_new
            return carry
        lax.fori_loop(0, MLA_HEADS, head, 0, unroll=True)

    @pl.when(ki < qi)
    def _():
        step(False)

    @pl.when(ki == qi)
    def _():
        step(True)

    @pl.when(ki == pl.num_programs(2) - 1)
    def _():
        for h in range(MLA_HEADS):
            acc = acc_ref[h]
            o_ref[:, h * MLA_V:(h + 1) * MLA_V] = acc[:, :MLA_V] / acc[:, MLA_V:MLA_V + 1]


def _mla_attn(q, k, v, batch, seq):
    n = q.shape[1]
    tq = tk = 512
    nq = seq // tq
    kv_spec = pl.BlockSpec((MLA_HEADS, tk, LANES), lambda b, i, j: (0, b * nq + jnp.minimum(i, j), 0))
    return pl.pallas_call(
        _mla_attn_kernel,
        grid=(batch, nq, nq),
        in_specs=[pl.BlockSpec((MLA_HEADS, tq, LANES), lambda b, i, j: (0, b * nq + i, 0)), kv_spec, kv_spec],
        out_specs=pl.BlockSpec((tq, MLA_HEADS * MLA_V), lambda b, i, j: (b * nq + i, 0)),
        out_shape=jax.ShapeDtypeStruct((n, MLA_HEADS * MLA_V), F32),
        scratch_shapes=[pltpu.VMEM((MLA_HEADS, tq, LANES), F32),
                        pltpu.VMEM((MLA_HEADS, tq, LANES), F32)],
        compiler_params=_cparams(("parallel", "parallel", "arbitrary")),
        name="mla_attn",
    )(q, k, v)


def _nsa_cmp_kernel(kc_ref, vc_ref, pak_ref, pbk_ref, wak_ref, wbk_ref, w2k_ref,
                    pav_ref, pbv_ref, wav_ref, wbv_ref, w2v_ref, gk_ref, kout_ref, vout_ref):
    def compress(chunks, pa, pb, wa, wb, w2):
        nc = chunks.shape[0]
        ha = _dot((chunks + pa).astype(BF16), wa)
        hb = _dot((chunks + pb).astype(BF16), wb)
        hid = jax.nn.gelu(ha + pltpu.roll(hb, nc - 1, axis=0))
        return [_dot(hid[:, g * CMP_HIDDEN:(g + 1) * CMP_HIDDEN].astype(BF16), w2)
                for g in range(NSA_KV_GROUPS)]

    kc = compress(kc_ref[0], pak_ref[...], pbk_ref[...], wak_ref[...], wbk_ref[...], w2k_ref[...])
    vc = compress(vc_ref[0], pav_ref[...], pbv_ref[...], wav_ref[...], wbv_ref[...], w2v_ref[...])
    for g in range(NSA_KV_GROUPS):
        kn = _rms(kc[g], gk_ref[...])
        kout_ref[0, g] = jnp.concatenate([kn, jnp.zeros_like(kn)], axis=1).astype(BF16)
        vout_ref[0, g] = vc[g].astype(BF16)


def _nsa_cmp(kc_chunks, vc_chunks, wk, wv, gk):
    b, nc, width = kc_chunks.shape
    full = lambda a: pl.BlockSpec(a.shape, lambda i: (0,) * a.ndim)
    chunk_spec = pl.BlockSpec((1, nc, width), lambda i: (i, 0, 0))
    out_spec = lambda width: pl.BlockSpec((1, NSA_KV_GROUPS, nc, width), lambda i: (i, 0, 0, 0))
    out_sds = lambda width: jax.ShapeDtypeStruct((b, NSA_KV_GROUPS, nc, width), BF16)
    return pl.pallas_call(
        _nsa_cmp_kernel,
        grid=(b,),
        in_specs=[chunk_spec, chunk_spec] + [full(a) for a in wk] + [full(a) for a in wv] + [full(gk)],
        out_specs=[out_spec(LANES), out_spec(NSA_HD)],
        out_shape=[out_sds(LANES), out_sds(NSA_HD)],
        compiler_params=_cparams(("parallel",)),
        name="nsa_cmp",
    )(kc_chunks, vc_chunks, *wk, *wv, gk)


def _nsa_prep_kernel(qn_ref, ks_ref, vs_ref, kw_ref, vw_ref, misc_ref, gq_ref, gk_ref,
                     q_ref, kso_ref, vso_ref, kwo_ref, vwo_ref, gate_ref, *, seq):
    gq2, gk2 = gq_ref[...], gk_ref[...]
    tm = qn_ref.shape[0]
    scale = NSA_HD ** -0.5 * LOG2E
    zpad = jnp.zeros((tm, NSA_HD), F32)
    for c in range(NSA_HEADS // 2):
        t = _seg_rms64(qn_ref[:, c * LANES:(c + 1) * LANES], gq2) * scale
        q_ref[2 * c] = jnp.concatenate([t[:, :NSA_HD], zpad], axis=1).astype(BF16)
        q_ref[2 * c + 1] = jnp.concatenate([t[:, NSA_HD:], zpad], axis=1).astype(BF16)
    ksn = _seg_rms64(ks_ref[0], gk2)
    kwn = _seg_rms64(kw_ref[0], gk2)
    vs, vw = vs_ref[0], vw_ref[0]
    row = lax.broadcasted_iota(jnp.int32, (tm, LANES), 0)
    lane = lax.broadcasted_iota(jnp.int32, (tm, LANES), 1)
    pos = (pl.program_id(0) * tm) % seq + row
    neg_onehot = jnp.where(lane == pos // SLC_LEN, -1.0, 0.0)
    for g in range(NSA_KV_GROUPS):
        sl = slice(g * NSA_HD, (g + 1) * NSA_HD)
        kso_ref[g] = jnp.concatenate([neg_onehot, ksn[:, sl], zpad], axis=1).astype(BF16)
        kwo_ref[g] = jnp.concatenate([kwn[:, sl], zpad], axis=1).astype(BF16)
        vso_ref[g] = _with_ones(vs[:, sl]).astype(BF16)
        vwo_ref[g] = _with_ones(vw[:, sl]).astype(BF16)
    sig = jax.nn.sigmoid(misc_ref[...])
    per_group = 3 * NSA_HPG
    for g in range(NSA_KV_GROUPS):
        gate_ref[g] = pltpu.roll(sig, LANES - (MLA_ROPE + per_group * g), axis=1)


def _nsa_prep(qn, kv, misc, gq2, gk2, seq):
    n = qn.shape[0]
    tm = 512
    full = lambda a: pl.BlockSpec(a.shape, lambda i: (0,) * a.ndim)
    kv_spec = lambda j: pl.BlockSpec((1, tm, LANES), lambda i, j=j: (j, i, 0))
    g_spec = lambda width: pl.BlockSpec((NSA_KV_GROUPS, tm, width), lambda i: (0, i, 0))
    g_sds = lambda width: jax.ShapeDtypeStruct((NSA_KV_GROUPS, n, width), BF16)
    return pl.pallas_call(
        functools.partial(_nsa_prep_kernel, seq=seq),
        grid=(n // tm,),
        in_specs=[pl.BlockSpec((tm, 512), lambda i: (i, 0)),
                  kv_spec(2), kv_spec(3), kv_spec(4), kv_spec(5),
                  pl.BlockSpec((tm, LANES), lambda i: (i, 0)), full(gq2), full(gk2)],
        out_specs=[pl.BlockSpec((NSA_HEADS, tm, LANES), lambda i: (0, i, 0)),
                   g_spec(2 * LANES), g_spec(LANES), g_spec(LANES), g_spec(LANES),
                   pl.BlockSpec((NSA_KV_GROUPS, tm, LANES), lambda i: (0, i, 0))],
        out_shape=[jax.ShapeDtypeStruct((NSA_HEADS, n, LANES), BF16),
                   g_sds(2 * LANES), g_sds(LANES), g_sds(LANES), g_sds(LANES),
                   jax.ShapeDtypeStruct((NSA_KV_GROUPS, n, LANES), F32)],
        compiler_params=_cparams(("parallel",)),
        name="nsa_prep",
    )(qn, kv, kv, kv, kv, misc, gq2, gk2)


def _nsa_select_kernel(q_ref, kc_ref, vct_ref, tct_ref, ovlt_ref, ocmpt_ref, sel_ref, s_ref, *, nc, n_sel):
    tq, hpg = NSA_TQS, NSA_HPG
    i = pl.program_id(2)
    q0 = i * tq
    n_forced = 3
    n_live = jnp.minimum((q0 + tq - CMP_LEN) // (CMP_STRIDE * LANES) + 1, nc // LANES)
    for ncb in range(1, nc // LANES + 1):
        pl.when(n_live == ncb)(functools.partial(
            _nsa_select_body, q_ref, kc_ref, vct_ref, tct_ref, ovlt_ref, ocmpt_ref, sel_ref, s_ref,
            ncb=ncb, n_sel=n_sel, n_forced=n_forced))


def _nsa_select_body(q_ref, kc_ref, vct_ref, tct_ref, ovlt_ref, ocmpt_ref, sel_ref, s_ref, *, ncb, n_sel, n_forced):
    tq, hpg = NSA_TQS, NSA_HPG
    i = pl.program_id(2)
    q0 = i * tq
    width = ncb * LANES
    s_ref[:width, :] = _dot_nt(kc_ref[0, 0, :width, :], q_ref[...].reshape(hpg * tq, LANES))

    first_blk = ((tq // CMP_STRIDE) * i + LANES - CMP_WINDOW_BACK) // LANES - 1
    for half in range(2):
        blk = first_blk + half

        @pl.when((blk >= 0) & (blk < ncb))
        def _():
            r0 = pl.multiple_of(blk * LANES, LANES)
            for h in range(hpg):
                s_ref[pl.ds(r0, LANES), h * tq:(h + 1) * tq] += tct_ref[0, h, half * LANES:(half + 1) * LANES, :]

    blk_row = lax.broadcasted_iota(jnp.int32, (width, tq), 0)
    qpos = q0 + lax.broadcasted_iota(jnp.int32, (width, tq), 1)
    valid = (CMP_STRIDE * blk_row + (CMP_LEN - 1)) <= qpos
    vct = vct_ref[0, 0, :, :width]
    psum = jnp.zeros((width, tq), F32)
    for h in range(hpg):
        sh = jnp.where(valid, s_ref[:width, h * tq:(h + 1) * tq], -jnp.inf)
        m = jnp.maximum(jnp.max(sh, axis=0, keepdims=True), NEG_INF)
        e = jnp.exp2(sh - m)
        p = e * (1.0 / jnp.maximum(jnp.sum(e, axis=0, keepdims=True), 1e-30))
        psum = psum + p
        ocmpt_ref[h * NSA_HD:(h + 1) * NSA_HD, :] = _dot(vct, p.astype(BF16))

    hi = psum.astype(BF16)
    rest = psum - hi.astype(F32)
    mid = rest.astype(BF16)
    lo = (rest - mid.astype(F32)).astype(BF16)
    ovlt = ovlt_ref[:, :width]
    imp = _dot(ovlt, hi) + _dot(ovlt, mid) + _dot(ovlt, lo)

    jrow = lax.broadcasted_iota(jnp.int32, (LANES, tq), 0)
    qpos = q0 + lax.broadcasted_iota(jnp.int32, (LANES, tq), 1)
    jf = jrow.astype(F32)
    cur = qpos // SLC_LEN
    forced = (jrow == 0) | (jrow == cur) | (jrow == cur - 1)
    imp = jnp.where(jrow * SLC_LEN <= qpos, imp, NEG_INF)
    if n_sel >= n_forced:
        imp, rounds = jnp.where(forced, -jnp.inf, imp), n_sel - n_forced
    else:
        imp, rounds = jnp.where(forced, FORCE_SCORE, imp), n_sel
    for _ in range(rounds):
        best = jnp.max(imp, axis=0, keepdims=True)
        first = jnp.min(jnp.where(imp == best, jf, float(LANES)), axis=0, keepdims=True)
        imp = jnp.where(jf == first, -jnp.inf, imp)
    sel_ref[0] = jnp.where(imp == -jnp.inf, 0.0, MASK_BIG).T.astype(BF16)


def _nsa_select(q, kcmp, vcmp_t, tct, ovl_t, batch, seq):
    n = q.shape[1]
    nc = kcmp.shape[2]
    nq = seq // NSA_TQS
    n_res = tct.shape[0]
    n_sel = min(SLC_TOPK, seq // SLC_LEN)
    g_, hpg = NSA_KV_GROUPS, NSA_HPG
    kern = functools.partial(_nsa_select_kernel, nc=nc, n_sel=n_sel)
    cmp_spec = lambda shape: pl.BlockSpec((1, 1) + shape, lambda b, g, i: (b, g, 0, 0))
    return pl.pallas_call(
        kern,
        grid=(batch, g_, nq),
        in_specs=[pl.BlockSpec((hpg, NSA_TQS, LANES), lambda b, g, i: (g, b * nq + i, 0)),
                  cmp_spec((nc, LANES)), cmp_spec((NSA_HD, nc)),
                  pl.BlockSpec((1, hpg, 2 * LANES, NSA_TQS), lambda b, g, i: (i % n_res, g, 0, 0)),
                  pl.BlockSpec(ovl_t.shape, lambda b, g, i: (0, 0))],
        out_specs=[pl.BlockSpec((hpg * NSA_HD, NSA_TQS), lambda b, g, i: (g, b * nq + i)),
                   pl.BlockSpec((1, NSA_TQS, LANES), lambda b, g, i: (g, b * nq + i, 0))],
        out_shape=[jax.ShapeDtypeStruct((NSA_HEADS * NSA_HD, n), F32),
                   jax.ShapeDtypeStruct((g_, n, LANES), BF16)],
        scratch_shapes=[pltpu.VMEM((nc, hpg * NSA_TQS), F32)],
        compiler_params=_cparams(("parallel", "parallel", "arbitrary")),
        name="nsa_select",
    )(q, kcmp, vcmp_t, tct, ovl_t)


def _nsa_attn_kernel(q_ref, ks_ref, vs_ref, kw_ref, vw_ref, sel_ref, tn_ref, gate_ref, ocmpt_ref,
                     o_ref, m_ref, acc_ref):
    tq, tk, hpg = NSA_TQ, NSA_TK, NSA_HPG
    rows = hpg * tq
    i = pl.program_id(2)
    q0 = i * tq
    q4 = q_ref[...].reshape(rows, LANES)
    q_aug = jnp.concatenate([jnp.concatenate([sel_ref[0]] * hpg, axis=0), q4], axis=1)
    row = lax.broadcasted_iota(jnp.int32, (tq, tk), 0)
    col = lax.broadcasted_iota(jnp.int32, (tq, tk), 1)
    m_ref[...] = jnp.full(m_ref.shape, NEG_INF, F32)
    acc_ref[...] = jnp.zeros(acc_ref.shape, F32)

    def tile(kt, near):
        k0 = pl.multiple_of(kt * tk, tk)
        s = _dot_nt(q_aug, ks_ref[0, 0, pl.ds(k0, tk), :])
        if near:
            causal = (k0 + col) <= (q0 + row)
            start = pl.multiple_of(k0 + tk + WIN - q0, LANES)
        ps, alphas = [], []
        for h in range(hpg):
            rs = slice(h * tq, (h + 1) * tq)
            sh = s[rs]
            if near:
                sh = jnp.where(causal, sh + tn_ref[h, :, pl.ds(start, tk)], -jnp.inf)
            m_old = m_ref[rs]
            m_new, p = _softmax_weights(sh, m_old)
            m_ref[rs] = m_new
            alphas.append(jnp.exp2(m_old - m_new))
            ps.append(p)
        acc_ref[...] = (jnp.concatenate(alphas, axis=0) * acc_ref[...]
                        + _dot(jnp.concatenate(ps, axis=0), vs_ref[0, 0, pl.ds(k0, tk), :]))

    last = q0 // tk
    n_far = jnp.maximum(last - 1, 0)
    def far_pair(pair, c):
        tile(2 * pair, False)
        tile(2 * pair + 1, False)
        return c

    lax.fori_loop(0, n_far // 2, far_pair, 0)
    lax.fori_loop(n_far // 2 * 2, n_far, lambda kt, c: (tile(kt, False), c)[1], 0)
    lax.fori_loop(n_far, last + 1, lambda kt, c: (tile(kt, True), c)[1], 0)

    wk = WIN + tq
    w0 = pl.multiple_of(q0, tq)
    s = _dot_nt(q4, kw_ref[0, 0, pl.ds(w0, wk), :])
    roww = lax.broadcasted_iota(jnp.int32, (tq, wk), 0)
    colw = lax.broadcasted_iota(jnp.int32, (tq, wk), 1)
    dist = roww + WIN - colw
    validw = (dist >= 0) & (dist < WIN) & (q0 - WIN + colw >= 0)
    ps = []
    for h in range(hpg):
        sh = s[h * tq:(h + 1) * tq] + tn_ref[h, :, tk:tk + wk]
        ps.append(_softmax_weights(jnp.where(validw, sh, -jnp.inf), NEG_INF)[1])
    accw = _dot(jnp.concatenate(ps, axis=0), vw_ref[0, 0, pl.ds(w0, wk), :])

    gates = gate_ref[0]
    acc = acc_ref[...]
    outs = []
    for h in range(hpg):
        rs = slice(h * tq, (h + 1) * tq)
        o_cmp = ocmpt_ref[h * NSA_HD:(h + 1) * NSA_HD, :].T
        o_slc = acc[rs, :NSA_HD] / jnp.maximum(acc[rs, NSA_HD:NSA_HD + 1], 1e-30)
        o_win = accw[rs, :NSA_HD] / accw[rs, NSA_HD:NSA_HD + 1]
        outs.append(gates[:, 3 * h:3 * h + 1] * o_cmp + gates[:, 3 * h + 1:3 * h + 2] * o_slc
                    + gates[:, 3 * h + 2:3 * h + 3] * o_win)
    o_ref[...] = jnp.concatenate(outs, axis=1)


def _nsa_attn(q, ks, vs, kwp, vwp, sel, tn, gates, ocmp_t, batch, seq):
    n = q.shape[1]
    nq = seq // NSA_TQ
    g_, hpg = NSA_KV_GROUPS, NSA_HPG
    seq_spec = lambda length, width: pl.BlockSpec((1, 1, length, width), lambda b, g, i: (g, b, 0, 0))
    tok_spec = pl.BlockSpec((1, NSA_TQ, LANES), lambda b, g, i: (g, b * nq + i, 0))
    out_spec = pl.BlockSpec((NSA_TQ, hpg * NSA_HD), lambda b, g, i: (b * nq + i, g))
    return pl.pallas_call(
        _nsa_attn_kernel,
        grid=(batch, g_, nq),
        in_specs=[pl.BlockSpec((hpg, NSA_TQ, LANES), lambda b, g, i: (g, b * nq + i, 0)),
                  seq_spec(seq, 2 * LANES), seq_spec(seq, LANES), seq_spec(seq + WIN, LANES), seq_spec(seq + WIN, LANES),
                  tok_spec,
                  pl.BlockSpec((hpg, NSA_TQ, 2 * NSA_TK + WIN), lambda b, g, i: (g, 0, 0)),
                  tok_spec, pl.BlockSpec((hpg * NSA_HD, NSA_TQ), lambda b, g, i: (g, b * nq + i))],
        out_specs=out_spec,
        out_shape=jax.ShapeDtypeStruct((n, NSA_HEADS * NSA_HD), F32),
        scratch_shapes=[pltpu.VMEM((hpg * NSA_TQ, LANES), F32), pltpu.VMEM((hpg * NSA_TQ, LANES), F32)],
        compiler_params=_cparams(("parallel", "parallel", "arbitrary")),
        name="nsa_attn",
    )(q, ks, vs, kwp, vwp, sel, tn, gates, ocmp_t)


def _mem_kv_kernel(mem_ref, g_ref, w_ref, gk_ref, k_ref, v_ref):
    mn = _rms(mem_ref[0], g_ref[...]).astype(BF16)
    kv = _dot(mn, w_ref[...])
    width = MEM_HEADS * MEM_HD
    for c in range(width // LANES):
        kn = _seg_rms64(kv[:, c * LANES:(c + 1) * LANES], gk_ref[...])
        vv = kv[:, width + c * LANES:width + (c + 1) * LANES]
        for half in range(2):
            sl = slice(half * MEM_HD, (half + 1) * MEM_HD)
            k_ref[0, 2 * c + half] = kn[:, sl].astype(BF16)
            v_ref[0, 2 * c + half] = vv[:, sl].astype(BF16)


def _mem_kv(mem, g, w, gk2):
    b, m, d = mem.shape
    full = lambda a: pl.BlockSpec(a.shape, lambda i: (0,) * a.ndim)
    spec = pl.BlockSpec((1, MEM_HEADS, m, MEM_HD), lambda i: (i, 0, 0, 0))
    sds = jax.ShapeDtypeStruct((b, MEM_HEADS, m, MEM_HD), BF16)
    return pl.pallas_call(
        _mem_kv_kernel,
        grid=(b,),
        in_specs=[pl.BlockSpec((1, m, d), lambda i: (i, 0, 0)), full(g), full(w), full(gk2)],
        out_specs=[spec, spec],
        out_shape=[sds, sds],
        compiler_params=_cparams(("parallel",)),
        name="mem_kv",
    )(mem, g, w, gk2)


def _post_attn_kernel(x_ref, ya_ref, yb_ref, goa_ref, gob_ref, woa_ref, wob_ref, gmn_ref, wmq_ref, gmq_ref,
                      km_ref, vm_ref, wmo_ref, gmoe_ref, rwh_ref, rwl_ref, rb_ref,
                      x2_ref, h2_ref, topi_ref, gate_ref, cnt_ref):
    first = (pl.program_id(0) == 0) & (pl.program_id(1) == 0)

    @pl.when(first)
    def _():
        cnt_ref[...] = jnp.zeros(cnt_ref.shape, F32)

    mixa = _rms(ya_ref[...], goa_ref[...]).astype(BF16)
    mixb = _rms(yb_ref[...], gob_ref[...]).astype(BF16)
    x1 = x_ref[...] + _dot(mixa, woa_ref[...]) + _dot(mixb, wob_ref[...])

    h = _rms(x1, gmn_ref[...]).astype(BF16)
    q = _dot(h, wmq_ref[...])
    scale = MEM_HD ** -0.5
    outs = []
    for c in range(MEM_HEADS * MEM_HD // LANES):
        qn = _seg_rms64(q[:, c * LANES:(c + 1) * LANES], gmq_ref[...]) * scale
        for half in range(2):
            hd = 2 * c + half
            qh = qn[:, half * MEM_HD:(half + 1) * MEM_HD].astype(BF16)
            s = _dot_nt(qh, km_ref[0, hd])
            e = jnp.exp(s - jnp.max(s, axis=-1, keepdims=True))
            p = e / jnp.sum(e, axis=-1, keepdims=True)
            outs.append(_dot(p.astype(BF16), vm_ref[0, hd]))
    o = jnp.concatenate(outs, axis=1).astype(BF16)
    x2 = x1 + _dot(o, wmo_ref[...])
    x2_ref[...] = x2

    h2 = _rms(x2, gmoe_ref[...])
    for c in range(ROW_TILE):
        h2_ref[pl.ds(c, h2.shape[0], stride=ROW_TILE), :] = h2[:, c * LANES:(c + 1) * LANES]
    h_hi = h2.astype(BF16)
    h_lo = (h2 - h_hi.astype(F32)).astype(BF16)
    logits = _dot(h_hi, rwh_ref[...]) + _dot(h_hi, rwl_ref[...]) + _dot(h_lo, rwh_ref[...]) + rb_ref[...]
    lane = lax.broadcasted_iota(jnp.int32, logits.shape, 1)
    lane_f = lane.astype(F32)
    topi = jnp.zeros(logits.shape, jnp.int32)
    topv = jnp.full(logits.shape, NEG_INF, F32)
    onehot = jnp.zeros(logits.shape, F32)
    for k in range(TOP_K):
        best = jnp.max(logits, axis=-1, keepdims=True)
        first_idx = jnp.min(jnp.where(logits == best, lane_f, float(LANES)), axis=-1, keepdims=True)
        hit = lane_f == first_idx
        topi = jnp.where(lane == k, first_idx.astype(jnp.int32), topi)
        topv = jnp.where(lane == k, best, topv)
        onehot = jnp.where(hit, 1.0, onehot)
        logits = jnp.where(hit, -jnp.inf, logits)
    e = jnp.where(lane < TOP_K, jnp.exp(topv - jnp.max(topv, axis=-1, keepdims=True)), 0.0)
    gate_ref[...] = e / jnp.sum(e, axis=-1, keepdims=True)
    topi_ref[...] = topi
    cnt_ref[...] += jnp.sum(onehot, axis=0, keepdims=True)


def _post_attn(x2d, ya, yb, goa, gob, woa, wob, gmn, wmq, gmq2, km, vm, wmo, gmoe, rwh, rwl, rb, batch, seq):
    n, d = x2d.shape
    tm = 256
    ns = seq // tm
    full = lambda a: pl.BlockSpec(a.shape, lambda b, i: (0,) * a.ndim)
    tok = lambda width: pl.BlockSpec((tm, width), lambda b, i: (b * ns + i, 0))
    mem_spec = pl.BlockSpec((1,) + km.shape[1:], lambda b, i: (b, 0, 0, 0))
    return pl.pallas_call(
        _post_attn_kernel,
        grid=(batch, ns),
        in_specs=[tok(d), tok(ya.shape[1]), tok(yb.shape[1]), full(goa), full(gob), full(woa), full(wob),
                  full(gmn), full(wmq), full(gmq2), mem_spec, mem_spec, full(wmo), full(gmoe),
                  full(rwh), full(rwl), full(rb)],
        out_specs=[tok(d), pl.BlockSpec((tm * ROW_TILE, LANES), lambda b, i: (b * ns + i, 0)),
                   tok(LANES), tok(LANES), pl.BlockSpec((1, LANES), lambda b, i: (0, 0))],
        out_shape=[jax.ShapeDtypeStruct((n, d), F32),
                   jax.ShapeDtypeStruct((n * ROW_TILE, LANES), F32),
                   jax.ShapeDtypeStruct((n, LANES), jnp.int32),
                   jax.ShapeDtypeStruct((n, LANES), F32),
                   jax.ShapeDtypeStruct((1, LANES), F32)],
        compiler_params=_cparams(("arbitrary", "arbitrary")),
        name="post_attn",
    )(x2d, ya, yb, goa, gob, woa, wob, gmn, wmq, gmq2, km, vm, wmo, gmoe, rwh, rwl, rb)


def _moe_pos_kernel(topi_ref, start_ref, tri_ref, pos_ref, carry_ref):
    @pl.when(pl.program_id(0) == 0)
    def _():
        carry_ref[...] = jnp.zeros(carry_ref.shape, F32)

    topi = topi_ref[...]
    lane = lax.broadcasted_iota(jnp.int32, topi.shape, 1)
    hits = [lane == topi[:, k:k + 1] for k in range(TOP_K)]
    onehot = sum(h.astype(F32) for h in hits)
    before = _dot(tri_ref[...], onehot.astype(BF16))
    base = start_ref[...] + carry_ref[...] + before
    pos = jnp.zeros(topi.shape, jnp.int32)
    for k in range(TOP_K):
        pk = jnp.sum(jnp.where(hits[k], base, 0.0), axis=-1, keepdims=True).astype(jnp.int32)
        pos = jnp.where(lane == k, pk, pos)
    pos_ref[...] = pos
    carry_ref[...] += jnp.sum(onehot, axis=0, keepdims=True)


def _moe_pos(topi, pad_start, tri):
    n = topi.shape[0]
    tm = tri.shape[0]
    return pl.pallas_call(
        _moe_pos_kernel,
        grid=(n // tm,),
        in_specs=[pl.BlockSpec((tm, LANES), lambda i: (i, 0)),
                  pl.BlockSpec((1, LANES), lambda i: (0, 0)),
                  pl.BlockSpec((tm, tm), lambda i: (0, 0))],
        out_specs=pl.BlockSpec((tm, LANES), lambda i: (i, 0)),
        out_shape=jax.ShapeDtypeStruct((n, LANES), jnp.int32),
        scratch_shapes=[pltpu.VMEM((1, LANES), F32)],
        compiler_params=_cparams(("arbitrary",)),
        name="moe_pos",
    )(topi, pad_start, tri)


def _moe_scatter_kernel(pos_ref, h_ref, zero_ref, xs_ref, sem, *, tm):
    del zero_ref

    def copy(t, k):
        dst = pl.multiple_of(pos_ref[t * TOP_K + k] * ROW_TILE, ROW_TILE)
        return pltpu.make_async_copy(h_ref.at[pl.ds(pl.multiple_of(t * ROW_TILE, ROW_TILE), ROW_TILE)],
                                     xs_ref.at[pl.ds(dst, ROW_TILE)], sem)

    def issue(t, c):
        for k in range(TOP_K):
            copy(t, k).start(priority=k % 2)
        return c

    def drain(t, c):
        for k in range(TOP_K):
            copy(t, k).wait()
        return c

    lax.fori_loop(0, tm, issue, 0, unroll=4)
    lax.fori_loop(0, tm, drain, 0, unroll=4)


def _moe_scatter(pos_flat, h2, xs_zero):
    n = h2.shape[0] // ROW_TILE
    tm = 512
    kern = functools.partial(_moe_scatter_kernel, tm=tm)
    return pl.pallas_call(
        kern,
        grid=(n // tm,),
        in_specs=[pl.BlockSpec((tm * TOP_K,), lambda i: (i,), memory_space=pltpu.SMEM),
                  pl.BlockSpec((tm * ROW_TILE, LANES), lambda i: (i, 0)),
                  pl.BlockSpec(memory_space=pl.ANY)],
        out_specs=pl.BlockSpec(memory_space=pl.ANY),
        out_shape=jax.ShapeDtypeStruct(xs_zero.shape, xs_zero.dtype),
        scratch_shapes=[pltpu.SemaphoreType.DMA(())],
        input_output_aliases={2: 0},
        compiler_params=_cparams(("arbitrary",)),
        name="moe_scatter",
    )(pos_flat, h2, xs_zero)


GU_GROUP = 2 * LANES


def _moe_wprep_kernel(w_ref, perm_ref, o_ref):
    w = w_ref[0].astype(BF16)
    for c in range(w.shape[1] // GU_GROUP):
        sl = slice(c * GU_GROUP, (c + 1) * GU_GROUP)
        o_ref[0, :, sl] = _dot(w[:, sl], perm_ref[...]).astype(BF16)


def _moe_wprep(w_gate_up, perm):
    e, d, width = w_gate_up.shape
    tk = 512
    return pl.pallas_call(
        _moe_wprep_kernel,
        grid=(e, d // tk),
        in_specs=[pl.BlockSpec((1, tk, width), lambda i, j: (i, j, 0)),
                  pl.BlockSpec(perm.shape, lambda i, j: (0, 0))],
        out_specs=pl.BlockSpec((1, tk, width), lambda i, j: (i, j, 0)),
        out_shape=jax.ShapeDtypeStruct((e, d, width), BF16),
        compiler_params=_cparams(("parallel", "parallel")),
        name="moe_wprep",
    )(w_gate_up, perm)


def _moe_ffn_kernel(blk_e_ref, n_used_ref, x_ref, wgu_ref, bg_ref, bu_ref, wd_ref, bd_ref, y_ref, wd_bf_ref):
    i = pl.program_id(0)
    chunk = lambda c: pl.ds(c, MOE_BLK, stride=ROW_TILE)
    new_expert = (i == 0) | (blk_e_ref[i] != blk_e_ref[jnp.maximum(i - 1, 0)])

    @pl.when(new_expert & (i < n_used_ref[0]))
    def _():
        wd_bf_ref[...] = wd_ref[0].astype(BF16)

    @pl.when(i < n_used_ref[0])
    def _():
        x = jnp.concatenate([x_ref[chunk(c), :] for c in range(ROW_TILE)], axis=1).astype(BF16)
        gu = _dot(x, wgu_ref[0])
        bg, bu = bg_ref[0], bu_ref[0]
        acts = []
        for c in range(gu.shape[1] // GU_GROUP):
            fs = slice(c * LANES, (c + 1) * LANES)
            gate = jnp.minimum(gu[:, c * GU_GROUP:c * GU_GROUP + LANES] + bg[:, fs], SWIGLU_LIMIT)
            up = jnp.clip(gu[:, c * GU_GROUP + LANES:(c + 1) * GU_GROUP] + bu[:, fs], -SWIGLU_LIMIT, SWIGLU_LIMIT)
            acts.append(((up + 1.0) * gate * jax.nn.sigmoid(SWIGLU_ALPHA * gate)).astype(BF16))
        act = jnp.concatenate(acts, axis=1)
        y = _dot(act, wd_bf_ref[...]) + bd_ref[0]
        for c in range(ROW_TILE):
            y_ref[chunk(c), :] = y[:, c * LANES:(c + 1) * LANES]

    @pl.when(i >= n_used_ref[0])
    def _():
        y_ref[...] = jnp.zeros(y_ref.shape, F32)


def _moe_ffn(blk_e, n_used, xs, wgu, bg, bu, wd, bd):
    p = xs.shape[0] // ROW_TILE
    dff, d = wd.shape[1:]
    n_blk = p // MOE_BLK
    w_spec = lambda r, c: pl.BlockSpec((1, r, c), lambda i, be, nu: (be[i], 0, 0))
    row_spec = pl.BlockSpec((MOE_BLK * ROW_TILE, LANES), lambda i, be, nu: (i, 0))
    grid_spec = pltpu.PrefetchScalarGridSpec(
        num_scalar_prefetch=2,
        grid=(n_blk,),
        in_specs=[row_spec,
                  w_spec(d, 2 * dff), w_spec(1, dff), w_spec(1, dff),
                  w_spec(dff, d), w_spec(1, d)],
        out_specs=row_spec,
        scratch_shapes=[pltpu.VMEM((dff, d), BF16)],
    )
    return pl.pallas_call(
        _moe_ffn_kernel,
        grid_spec=grid_spec,
        out_shape=jax.ShapeDtypeStruct(xs.shape, F32),
        compiler_params=_cparams(("arbitrary",)),
        name="moe_ffn",
    )(blk_e, n_used, xs, wgu, bg, bu, wd, bd)


def _moe_combine_kernel(pos_ref, gate_ref, x_ref, ys_ref, o_ref, buf_ref, sem, *, tm):
    def copy(t, k):
        src = pl.multiple_of(pos_ref[t * TOP_K + k] * ROW_TILE, ROW_TILE)
        return pltpu.make_async_copy(ys_ref.at[pl.ds(src, ROW_TILE)],
                                     buf_ref.at[k, pl.ds(pl.multiple_of(t * ROW_TILE, ROW_TILE), ROW_TILE)], sem)

    def issue(t, c):
        for k in range(TOP_K):
            copy(t, k).start(priority=k % 2)
        return c

    def drain(t, c):
        for k in range(TOP_K):
            copy(t, k).wait()
        return c

    lax.fori_loop(0, tm, issue, 0, unroll=4)
    lax.fori_loop(0, tm, drain, 0, unroll=4)
    gates = gate_ref[...]
    for c in range(ROW_TILE):
        cs = slice(c * LANES, (c + 1) * LANES)
        acc = x_ref[:, cs]
        for k in range(TOP_K):
            acc = acc + gates[:, k:k + 1] * buf_ref[k, pl.ds(c, tm, stride=ROW_TILE), :]
        o_ref[:, cs] = acc


def _moe_combine(pos_flat, gates, x2, ys):
    n, d = x2.shape
    tm = 256
    kern = functools.partial(_moe_combine_kernel, tm=tm)
    return pl.pallas_call(
        kern,
        grid=(n // tm,),
        in_specs=[pl.BlockSpec((tm * TOP_K,), lambda i: (i,), memory_space=pltpu.SMEM),
                  pl.BlockSpec((tm, LANES), lambda i: (i, 0)),
                  pl.BlockSpec((tm, d), lambda i: (i, 0)),
                  pl.BlockSpec(memory_space=pl.ANY)],
        out_specs=pl.BlockSpec((tm, d), lambda i: (i, 0)),
        out_shape=jax.ShapeDtypeStruct((n, d), F32),
        scratch_shapes=[pltpu.VMEM((TOP_K, tm * ROW_TILE, LANES), F32), pltpu.SemaphoreType.DMA(())],
        compiler_params=_cparams(("arbitrary",)),
        name="moe_combine",
    )(pos_flat, gates, x2, ys)


def _rel_bucket_np(dist):
    n = np.maximum(dist, 0)
    nf = np.maximum(n, 1).astype(np.float32)
    ratio = np.log(nf / np.float32(REL_MAX_EXACT)) / np.float32(math.log(REL_MAX_DIST / REL_MAX_EXACT))
    large = REL_MAX_EXACT + (ratio * np.float32(REL_BUCKETS - REL_MAX_EXACT)).astype(np.int32)
    large = np.minimum(large, REL_BUCKETS - 1)
    return np.where(n < REL_MAX_EXACT, n, large)


def _bias_tables_kernel(tbl_ref, bc_ref, bt_ref, tc_ref, tn_ref):
    h = pl.program_id(0)
    bc, bt = bc_ref[...], bt_ref[...]
    tc = jnp.zeros(bc.shape, F32)
    tn = jnp.zeros(bt.shape, F32)
    far = tbl_ref[REL_BUCKETS - 1, h]
    for b in range(REL_BUCKETS - 1):
        v = (tbl_ref[b, h] - far) * LOG2E
        tc = jnp.where(bc == b, v, tc)
        tn = jnp.where(bt == b, v, tn)
    tc_ref[:, 0] = tc
    tn_ref[0] = tn


def _bias_tables(rel_table):
    q = np.arange(NSA_TQS)[None, :, None]
    blocks_per_tile = NSA_TQS // CMP_STRIDE
    n_res = LANES // blocks_per_tile
    res = np.arange(n_res)[:, None, None]
    base_gap = (blocks_per_tile * res - CMP_WINDOW_BACK) % LANES + CMP_WINDOW_BACK
    w = np.arange(2 * LANES)[None, None, :]
    b_cmp = _rel_bucket_np(CMP_STRIDE * base_gap + q - CMP_STRIDE * w - (CMP_LEN - 1)).astype(np.int32)
    b_cmp = np.ascontiguousarray(b_cmp.transpose(0, 2, 1))
    c = np.arange(2 * NSA_TK + WIN)[None, :]
    b_tok = _rel_bucket_np(np.arange(NSA_TQ)[:, None] + WIN - (c - NSA_TK)).astype(np.int32)
    heads = rel_table.shape[1]
    return pl.pallas_call(
        _bias_tables_kernel,
        grid=(heads,),
        in_specs=[pl.BlockSpec(memory_space=pltpu.SMEM),
                  pl.BlockSpec(b_cmp.shape, lambda h: (0, 0, 0)),
                  pl.BlockSpec(b_tok.shape, lambda h: (0, 0))],
        out_specs=[pl.BlockSpec((n_res, 1) + b_cmp.shape[1:], lambda h: (0, h, 0, 0)),
                   pl.BlockSpec((1,) + b_tok.shape, lambda h: (h, 0, 0))],
        out_shape=[jax.ShapeDtypeStruct((n_res, heads) + b_cmp.shape[1:], F32),
                   jax.ShapeDtypeStruct((heads,) + b_tok.shape, F32)],
        compiler_params=_cparams(("parallel",)),
        name="bias_tables",
    )(rel_table.astype(F32), jnp.asarray(b_cmp), jnp.asarray(b_tok))


def kernel(x, mem, g_attn_norm, w_in, g_cq, w_uq, g_ckv, w_ukv, g_q_mla, g_k_mla, cmp_k_pos, cmp_k_w1, cmp_k_w2, cmp_v_pos, cmp_v_w1, cmp_v_w2, g_q_nsa, g_k_nsa, rel_table, g_out_mla, g_out_nsa, w_out, g_mem_norm, g_mem_src, w_mq, w_mkv, g_mq, g_mk, w_mo, g_moe_norm, router_w, router_b, w_gate_up, b_gate_up, w_down, b_down):
    batch, seq, d = x.shape
    n = batch * seq
    depth = w_in.shape[0]
    assert seq % 512 == 0 and seq // SLC_LEN <= LANES and d == ROW_TILE * LANES
    row = lambda v: v.reshape(1, -1).astype(F32)
    tile2 = lambda v: jnp.concatenate([v, v]).reshape(1, -1).astype(F32)

    x2d = x.reshape(n, d)
    for l in range(depth):
        wi = w_in[l]
        w_in_r = jnp.concatenate(
            [wi[:, 0:384], wi[:, 416:928], wi[:, 928:1696], wi[:, 384:416], wi[:, 1696:1720],
             jnp.zeros((d, IN_COLS_PAD - 1720), wi.dtype)], axis=1).astype(BF16)
        dq = MLA_NOPE + MLA_ROPE
        half = MLA_ROPE // 2
        x1, x2 = slice(MLA_NOPE, MLA_NOPE + half), slice(MLA_NOPE + half, dq)
        head_pad = lambda w, width: jnp.pad(w, ((0, 0), (0, 0), (0, LANES - width))).reshape(
            w.shape[0], MLA_HEADS * LANES).astype(BF16)
        wuq3 = w_uq[l].reshape(MLA_Q_RANK, MLA_HEADS, dq)
        wuq_r = head_pad(wuq3, dq)
        wuq_s = head_pad(jnp.concatenate([jnp.zeros_like(wuq3[:, :, :MLA_NOPE]), wuq3[:, :, x2], wuq3[:, :, x1]], 2), dq)
        wukv = w_ukv[l].reshape(MLA_KV_RANK, MLA_HEADS, MLA_NOPE + MLA_V)
        wuk_r = head_pad(wukv[:, :, :MLA_NOPE], MLA_NOPE)
        wuv_r = head_pad(wukv[:, :, MLA_NOPE:], MLA_V)

        inv = ROPE_THETA ** (-jnp.arange(half, dtype=F32) / half)
        ang = jnp.arange(seq, dtype=F32)[:, None] * inv
        cos, sin = jnp.cos(ang), jnp.sin(ang)
        lane_pad = jnp.zeros((seq, LANES - dq), F32)
        rot_c = jnp.concatenate([jnp.ones((seq, MLA_NOPE), F32), cos, cos, lane_pad], 1)
        rot_s = jnp.concatenate([jnp.zeros((seq, MLA_NOPE), F32), -sin, sin, lane_pad], 1)

        def rope_tables(g, scale):
            g_pad = jnp.pad(g, (0, LANES - dq))
            g_swp = jnp.pad(jnp.concatenate([g[:MLA_NOPE], g[x2], g[x1]]), (0, LANES - dq))
            return rot_c * (g_pad * scale), rot_s * (g_swp * scale)

        cq0, cq1 = rope_tables(g_q_mla[l], dq ** -0.5 * LOG2E)
        ck0, ck1 = rope_tables(g_k_mla[l], 1.0)

        mla_in, qn, kv6, misc = _in_proj(x2d, row(g_attn_norm[l]), w_in_r)
        q_m, k_m, v_m = _mla_prep(mla_in, misc, row(g_cq[l]), wuq_r, wuq_s, row(g_ckv[l]), wuk_r, wuv_r,
                                  cq0, cq1, ck0, ck1, seq)
        y_mla = _mla_attn(q_m, k_m, v_m, batch, seq)

        nc = seq // CMP_STRIDE
        half_len = CMP_LEN // 2
        eye_g = jnp.eye(NSA_KV_GROUPS, dtype=F32)

        def cmp_weights(pos, w1, w2):
            out = []
            for part in range(2):
                sl = slice(part * half_len, (part + 1) * half_len)
                out.append(jnp.broadcast_to(pos[sl][:, None, :], (half_len, NSA_KV_GROUPS, NSA_HD))
                           .reshape(1, -1))
            for part in range(2):
                sl = slice(part * half_len, (part + 1) * half_len)
                wexp = jnp.einsum('ldf,gh->lgdhf', w1[sl], eye_g)
                out.append(wexp.reshape(half_len * NSA_KV_GROUPS * NSA_HD, NSA_KV_GROUPS * CMP_HIDDEN).astype(BF16))
            out.append(w2.astype(BF16))
            return out

        chunk_w = CMP_STRIDE * NSA_KV_GROUPS * NSA_HD
        kcmp, vcmp = _nsa_cmp(kv6[0].reshape(batch, nc, chunk_w), kv6[1].reshape(batch, nc, chunk_w),
                              cmp_weights(cmp_k_pos[l], cmp_k_w1[l], cmp_k_w2[l]),
                              cmp_weights(cmp_v_pos[l], cmp_v_w1[l], cmp_v_w2[l]), row(g_k_nsa[l]))
        q_n, ks, vs, kw, vw, gates_n = _nsa_prep(qn, kv6, misc, tile2(g_q_nsa[l]), tile2(g_k_nsa[l]), seq)
        by_batch = lambda t: t.reshape(NSA_KV_GROUPS, batch, seq, t.shape[-1])
        front_pad = lambda t: jnp.pad(by_batch(t), ((0, 0), (0, 0), (WIN, 0), (0, 0)))
        tc, tn = _bias_tables(rel_table)
        n_idx = np.arange(nc)[:, None]
        j_idx = np.arange(LANES)[None, :]
        ovl = ((CMP_STRIDE * n_idx < SLC_LEN * j_idx + SLC_LEN)
               & (CMP_STRIDE * n_idx + CMP_LEN - 1 >= SLC_LEN * j_idx)
               & (n_idx < nc - 1) & (j_idx < seq // SLC_LEN)).astype(np.float32)
        o_cmp_t, sel = _nsa_select(q_n, kcmp, jnp.swapaxes(vcmp, 2, 3), tc, jnp.asarray(ovl.T, dtype=BF16),
                                   batch, seq)
        y_nsa = _nsa_attn(q_n, by_batch(ks), by_batch(vs), front_pad(kw), front_pad(vw), sel,
                          tn, gates_n, o_cmp_t, batch, seq)

        k_mem, v_mem = _mem_kv(mem, row(g_mem_src[l]), w_mkv[l].reshape(d, 2 * MEM_HEADS * MEM_HD).astype(BF16),
                               tile2(g_mk[l]))
        rw = jnp.pad(router_w[l], ((0, 0), (0, LANES - N_EXPERTS)))
        rw_hi = rw.astype(BF16)
        rw_lo = (rw - rw_hi.astype(F32)).astype(BF16)
        rb =jnp.pad(router_b[l], (0, LANES - N_EXPERTS), constant_values=NEG_INF).reshape(1, LANES)
        wo = w_out[l].astype(BF16)
        n_mla = MLA_HEADS * MLA_V
        x2, h2, topi, gates_e, counts = _post_attn(
            x2d, y_mla, y_nsa, row(g_out_mla[l]), row(g_out_nsa[l]), wo[:n_mla], wo[n_mla:],
            row(g_mem_norm[l]), w_mq[l].astype(BF16), tile2(g_mq[l]), k_mem, v_mem, w_mo[l].astype(BF16),
            row(g_moe_norm[l]), rw_hi, rw_lo, rb, batch, seq)

        cnt = counts[0].astype(jnp.int32)
        padded = (cnt + MOE_BLK - 1) // MOE_BLK * MOE_BLK
        pad_end = jnp.cumsum(padded)
        pad_start = (pad_end - padded).astype(F32).reshape(1, LANES)
        p_rows = (n * TOP_K // MOE_BLK + N_EXPERTS) * MOE_BLK
        n_blk = p_rows // MOE_BLK
        blk_first_row = jnp.arange(n_blk, dtype=jnp.int32) * MOE_BLK
        blk_e = jnp.minimum(jnp.sum(pad_end[None, :N_EXPERTS] <= blk_first_row[:, None], axis=1),
                            N_EXPERTS - 1).astype(jnp.int32)
        n_used = (pad_end[N_EXPERTS - 1] // MOE_BLK).astype(jnp.int32).reshape(1)
        tm_pos = 256
        tri = (np.arange(tm_pos)[None, :] < np.arange(tm_pos)[:, None]).astype(np.float32)
        pos = _moe_pos(topi, pad_start, jnp.asarray(tri, dtype=BF16))
        pos_flat = pos[:, :TOP_K].reshape(n * TOP_K)
        xs = _moe_scatter(pos_flat, h2, jnp.zeros((p_rows * ROW_TILE, LANES), F32))
        src = np.arange(GU_GROUP)
        perm = (np.arange(GU_GROUP)[:, None] == np.where(src < LANES, 2 * src, 2 * (src - LANES) + 1)[None, :])
        wgu = _moe_wprep(w_gate_up[l], jnp.asarray(perm.astype(np.float32), dtype=BF16))
        bgu = b_gate_up[l]
        ys = _moe_ffn(blk_e, n_used, xs, wgu, bgu[:, None, 0::2], bgu[:, None, 1::2],
                      w_down[l], b_down[l][:, None, :])
        x2d = _moe_combine(pos_flat, gates_e, x2, ys)
    return x2d.reshape(batch, seq, d)
```

```python
import functools
import math

import numpy as np
import jax
import jax.numpy as jnp
from jax import lax
from jax.experimental import pallas as pl
from jax.experimental.pallas import tpu as pltpu

F32 = jnp.float32
BF16 = jnp.bfloat16

EPS = 1e-6
NEG_INF = -1e30
LANES = 128

MLA_HEADS = 8
MLA_NOPE = 64
MLA_ROPE = 32
MLA_V = 64
MLA_Q_RANK = 256
MLA_KV_RANK = 128
ROPE_THETA = 10000.0

NSA_HEADS = 8
NSA_KV_GROUPS = 2
NSA_HPG = NSA_HEADS // NSA_KV_GROUPS
NSA_HD = 64
CMP_LEN = 32
CMP_STRIDE = 16
CMP_HIDDEN = 128
SLC_LEN = 64
SLC_TOPK = 16
WIN = 512
FORCE_SCORE = 1e9
NSA_TQ = 256
NSA_TQS = 512
NSA_TK = 512
CMP_WINDOW_BACK = 40

REL_BUCKETS = 32
REL_MAX_EXACT = 16
REL_MAX_DIST = 512

MEM_HEADS = 4
MEM_HD = 64

N_EXPERTS = 32
TOP_K = 4
SWIGLU_LIMIT = 7.0
SWIGLU_ALPHA = 1.702
MOE_BLK = 512
ROW_TILE = 8

LOG2E = math.log2(math.e)
MASK_BIG = 2.0 ** 100

VMEM_LIMIT = 56 * 1024 * 1024


def _cparams(sem, vmem=VMEM_LIMIT):
    return pltpu.CompilerParams(dimension_semantics=sem, vmem_limit_bytes=vmem)


def _rms(x, g):
    return x * lax.rsqrt(jnp.mean(x * x, axis=-1, keepdims=True) + EPS) * g


def _seg_rms64(t, g2):
    lane = lax.broadcasted_iota(jnp.int32, t.shape, 1)
    sq = t * t
    lo = jnp.sum(jnp.where(lane < 64, sq, 0.0), axis=-1, keepdims=True)
    hi = jnp.sum(jnp.where(lane >= 64, sq, 0.0), axis=-1, keepdims=True)
    ms = jnp.where(lane < 64, lo, hi) * (1.0 / 64.0)
    return t * lax.rsqrt(ms + EPS) * g2


def _dot(a, b):
    return jnp.dot(a, b, preferred_element_type=F32)


def _with_ones(v):
    lane = lax.broadcasted_iota(jnp.int32, v.shape, 1)
    return jnp.concatenate([v, jnp.where(lane == 0, 1.0, 0.0).astype(v.dtype)], axis=1)


def _dot_nt(a, b):
    return lax.dot_general(a, b, (((1,), (1,)), ((), ())), preferred_element_type=F32)


def _softmax_weights(s, m_old):
    cols = [s[:, c * LANES:(c + 1) * LANES] for c in range(s.shape[1] // LANES)]
    m_new = jnp.maximum(m_old, jnp.max(functools.reduce(jnp.maximum, cols), axis=-1, keepdims=True))
    return m_new, jnp.concatenate([jnp.exp2((c - m_new).astype(BF16)) for c in cols], axis=1)


IN_COLS_PAD = 1792


def _in_proj_kernel(x_ref, g_ref, w_ref, mla_ref, qn_ref, kv_ref, misc_ref):
    h = _rms(x_ref[...], g_ref[...])
    p = _dot(h.astype(BF16), w_ref[...])
    mla_ref[...] = p[:, 0:384]
    qn_ref[...] = p[:, 384:896]
    for j in range(6):
        kv_ref[j] = p[:, 896 + 128 * j:1024 + 128 * j]
    misc_ref[...] = p[:, 1664:1792]


def _in_proj(x2d, g, w):
    n, d = x2d.shape
    tm = 512
    return pl.pallas_call(
        _in_proj_kernel,
        grid=(n // tm,),
        in_specs=[pl.BlockSpec((tm, d), lambda i: (i, 0)),
                  pl.BlockSpec((1, d), lambda i: (0, 0)),
                  pl.BlockSpec((d, IN_COLS_PAD), lambda i: (0, 0))],
        out_specs=[pl.BlockSpec((tm, 384), lambda i: (i, 0)),
                   pl.BlockSpec((tm, 512), lambda i: (i, 0)),
                   pl.BlockSpec((6, tm, 128), lambda i: (0, i, 0)),
                   pl.BlockSpec((tm, 128), lambda i: (i, 0))],
        out_shape=[jax.ShapeDtypeStruct((n, 384), F32),
                   jax.ShapeDtypeStruct((n, 512), F32),
                   jax.ShapeDtypeStruct((6, n, 128), F32),
                   jax.ShapeDtypeStruct((n, 128), F32)],
        compiler_params=_cparams(("parallel",)),
        name="in_proj",
    )(x2d, g, w)


def _mla_prep_kernel(mla_ref, misc_ref, gcq_ref, wuq_ref, wuqs_ref, gckv_ref, wuk_ref, wuv_ref,
                     cq0_ref, cq1_ref, ck0_ref, ck1_ref, q_ref, k_ref, v_ref):
    p = mla_ref[...]
    cqn = _rms(p[:, 0:MLA_Q_RANK], gcq_ref[...]).astype(BF16)
    ckvn = _rms(p[:, MLA_Q_RANK:MLA_Q_RANK + MLA_KV_RANK], gckv_ref[...]).astype(BF16)
    qall = _dot(cqn, wuq_ref[...])
    qswp = _dot(cqn, wuqs_ref[...])
    kall = _dot(ckvn, wuk_ref[...])
    vall = _dot(ckvn, wuv_ref[...])
    misc = misc_ref[...]
    lane = lax.broadcasted_iota(jnp.int32, misc.shape, 1)
    half = MLA_ROPE // 2
    in_rope = (lane >= MLA_NOPE) & (lane < MLA_NOPE + MLA_ROPE)
    krope = jnp.where(in_rope, pltpu.roll(misc, MLA_NOPE, axis=1), 0.0)
    kswp = jnp.where(lane < MLA_NOPE + half, pltpu.roll(misc, MLA_NOPE - half, axis=1),
                     pltpu.roll(misc, MLA_NOPE + half, axis=1))
    cq0, cq1, ck0 = cq0_ref[...], cq1_ref[...], ck0_ref[...]
    kswp_term = jnp.where(in_rope, kswp, 0.0) * ck1_ref[...]
    ones_col = jnp.where(lane == MLA_V, 1.0, 0.0)
    inv_dk = 1.0 / (MLA_NOPE + MLA_ROPE)

    def inv_rms(t):
        return lax.rsqrt(jnp.sum(t * t, axis=-1, keepdims=True) * inv_dk + EPS)

    for h in range(MLA_HEADS):
        hs = slice(h * LANES, (h + 1) * LANES)
        tq = qall[:, hs]
        q_ref[h] = (inv_rms(tq) * (tq * cq0 + qswp[:, hs] * cq1)).astype(BF16)
        tk = kall[:, hs] + krope
        k_ref[h] = (inv_rms(tk) * (tk * ck0 + kswp_term)).astype(BF16)
        v_ref[h] = (vall[:, hs] + ones_col).astype(BF16)


def _mla_prep(mla, misc, gcq, wuq, wuqs, gckv, wuk, wuv, cq0, cq1, ck0, ck1, seq):
    n = mla.shape[0]
    tm = 256
    ns = seq // tm
    full = lambda a: pl.BlockSpec(a.shape, lambda i: (0,) * a.ndim)
    rope_spec = pl.BlockSpec((tm, LANES), lambda i: (i % ns, 0))
    head_spec = pl.BlockSpec((MLA_HEADS, tm, LANES), lambda i: (0, i, 0))
    head_sds = jax.ShapeDtypeStruct((MLA_HEADS, n, LANES), BF16)
    return pl.pallas_call(
        _mla_prep_kernel,
        grid=(n // tm,),
        in_specs=[pl.BlockSpec((tm, 384), lambda i: (i, 0)),
                  pl.BlockSpec((tm, LANES), lambda i: (i, 0)),
                  full(gcq), full(wuq), full(wuqs), full(gckv), full(wuk), full(wuv),
                  rope_spec, rope_spec, rope_spec, rope_spec],
        out_specs=[head_spec, head_spec, head_spec],
        out_shape=[head_sds, head_sds, head_sds],
        compiler_params=_cparams(("parallel",)),
        name="mla_prep",
    )(mla, misc, gcq, wuq, wuqs, gckv, wuk, wuv, cq0, cq1, ck0, ck1)


def _mla_attn_kernel(q_ref, k_ref, v_ref, o_ref, m_ref, acc_ref):
    qi, ki = pl.program_id(1), pl.program_id(2)

    @pl.when(ki == 0)
    def _():
        m_ref[...] = jnp.full(m_ref.shape, NEG_INF, F32)
        acc_ref[...] = jnp.zeros(acc_ref.shape, F32)

    def step(masked):
        def head(h, carry):
            s = _dot_nt(q_ref[h], k_ref[h])
            if masked:
                row = lax.broadcasted_iota(jnp.int32, s.shape, 0)
                col = lax.broadcasted_iota(jnp.int32, s.shape, 1)
                s = jnp.where(col <= row, s, -jnp.inf)
            m_old = m_ref[h]
            m_new, p = _softmax_weights(s, m_old)
            acc_ref[h] = jnp.exp2(m_old - m_new) * acc_ref[h] + _dot(p, v_ref[h])
            m_ref[h] = m_new
            return carry
        lax.fori_loop(0, MLA_HEADS, head, 0, unroll=True)

    @pl.when(ki < qi)
    def _():
        step(False)

    @pl.when(ki == qi)
    def _():
        step(True)

    @pl.when(ki == pl.num_programs(2) - 1)
    def _():
        for h in range(MLA_HEADS):
            acc = acc_ref[h]
            o_ref[:, h * MLA_V:(h + 1) * MLA_V] = acc[:, :MLA_V] / acc[:, MLA_V:MLA_V + 1]


def _mla_attn(q, k, v, batch, seq):
    n = q.shape[1]
    tq = tk = 512
    nq = seq // tq
    kv_spec = pl.BlockSpec((MLA_HEADS, tk, LANES), lambda b, i, j: (0, b * nq + jnp.minimum(i, j), 0))
    return pl.pallas_call(
        _mla_attn_kernel,
        grid=(batch, nq, nq),
        in_specs=[pl.BlockSpec((MLA_HEADS, tq, LANES), lambda b, i, j: (0, b * nq + i, 0)), kv_spec, kv_spec],
        out_specs=pl.BlockSpec((tq, MLA_HEADS * MLA_V), lambda b, i, j: (b * nq + i, 0)),
        out_shape=jax.ShapeDtypeStruct((n, MLA_HEADS * MLA_V), F32),
        scratch_shapes=[pltpu.VMEM((MLA_HEADS, tq, LANES), F32),
                        pltpu.VMEM((MLA_HEADS, tq, LANES), F32)],
        compiler_params=_cparams(("parallel", "parallel", "arbitrary")),
        name="mla_attn",
    )(q, k, v)


def _nsa_cmp_kernel(kc_ref, vc_ref, pak_ref, pbk_ref, wak_ref, wbk_ref, w2k_ref,
                    pav_ref, pbv_ref, wav_ref, wbv_ref, w2v_ref, gk_ref, kout_ref, vout_ref):
    def compress(chunks, pa, pb, wa, wb, w2):
        nc = chunks.shape[0]
        ha = _dot((chunks + pa).astype(BF16), wa)
        hb = _dot((chunks + pb).astype(BF16), wb)
        hid = jax.nn.gelu(ha + pltpu.roll(hb, nc - 1, axis=0))
        return [_dot(hid[:, g * CMP_HIDDEN:(g + 1) * CMP_HIDDEN].astype(BF16), w2)
                for g in range(NSA_KV_GROUPS)]

    kc = compress(kc_ref[0], pak_ref[...], pbk_ref[...], wak_ref[...], wbk_ref[...], w2k_ref[...])
    vc = compress(vc_ref[0], pav_ref[...], pbv_ref[...], wav_ref[...], wbv_ref[...], w2v_ref[...])
    for g in range(NSA_KV_GROUPS):
        kn = _rms(kc[g], gk_ref[...])
        kout_ref[0, g] = jnp.concatenate([kn, jnp.zeros_like(kn)], axis=1).astype(BF16)
        vout_ref[0, g] = vc[g].astype(BF16)


def _nsa_cmp(kc_chunks, vc_chunks, wk, wv, gk):
    b, nc, width = kc_chunks.shape
    full = lambda a: pl.BlockSpec(a.shape, lambda i: (0,) * a.ndim)
    chunk_spec = pl.BlockSpec((1, nc, width), lambda i: (i, 0, 0))
    out_spec = lambda width: pl.BlockSpec((1, NSA_KV_GROUPS, nc, width), lambda i: (i, 0, 0, 0))
    out_sds = lambda width: jax.ShapeDtypeStruct((b, NSA_KV_GROUPS, nc, width), BF16)
    return pl.pallas_call(
        _nsa_cmp_kernel,
        grid=(b,),
        in_specs=[chunk_spec, chunk_spec] + [full(a) for a in wk] + [full(a) for a in wv] + [full(gk)],
        out_specs=[out_spec(LANES), out_spec(NSA_HD)],
        out_shape=[out_sds(LANES), out_sds(NSA_HD)],
        compiler_params=_cparams(("parallel",)),
        name="nsa_cmp",
    )(kc_chunks, vc_chunks, *wk, *wv, gk)


def _nsa_prep_kernel(qn_ref, ks_ref, vs_ref, kw_ref, vw_ref, misc_ref, gq_ref, gk_ref,
                     q_ref, kso_ref, vso_ref, kwo_ref, vwo_ref, gate_ref, *, seq):
    gq2, gk2 = gq_ref[...], gk_ref[...]
    tm = qn_ref.shape[0]
    scale = NSA_HD ** -0.5 * LOG2E
    zpad = jnp.zeros((tm, NSA_HD), F32)
    for c in range(NSA_HEADS // 2):
        t = _seg_rms64(qn_ref[:, c * LANES:(c + 1) * LANES], gq2) * scale
        q_ref[2 * c] = jnp.concatenate([t[:, :NSA_HD], zpad], axis=1).astype(BF16)
        q_ref[2 * c + 1] = jnp.concatenate([t[:, NSA_HD:], zpad], axis=1).astype(BF16)
    ksn = _seg_rms64(ks_ref[0], gk2)
    kwn = _seg_rms64(kw_ref[0], gk2)
    vs, vw = vs_ref[0], vw_ref[0]
    row = lax.broadcasted_iota(jnp.int32, (tm, LANES), 0)
    lane = lax.broadcasted_iota(jnp.int32, (tm, LANES), 1)
    pos = (pl.program_id(0) * tm) % seq + row
    neg_onehot = jnp.where(lane == pos // SLC_LEN, -1.0, 0.0)
    for g in range(NSA_KV_GROUPS):
        sl = slice(g * NSA_HD, (g + 1) * NSA_HD)
        kso_ref[g] = jnp.concatenate([neg_onehot, ksn[:, sl], zpad], axis=1).astype(BF16)
        kwo_ref[g] = jnp.concatenate([kwn[:, sl], zpad], axis=1).astype(BF16)
        vso_ref[g] = _with_ones(vs[:, sl]).astype(BF16)
        vwo_ref[g] = _with_ones(vw[:, sl]).astype(BF16)
    sig = jax.nn.sigmoid(misc_ref[...])
    per_group = 3 * NSA_HPG
    for g in range(NSA_KV_GROUPS):
        gate_ref[g] = pltpu.roll(sig, LANES - (MLA_ROPE + per_group * g), axis=1)


def _nsa_prep(qn, kv, misc, gq2, gk2, seq):
    n = qn.shape[0]
    tm = 512
    full = lambda a: pl.BlockSpec(a.shape, lambda i: (0,) * a.ndim)
    kv_spec = lambda j: pl.BlockSpec((1, tm, LANES), lambda i, j=j: (j, i, 0))
    g_spec = lambda width: pl.BlockSpec((NSA_KV_GROUPS, tm, width), lambda i: (0, i, 0))
    g_sds = lambda width: jax.ShapeDtypeStruct((NSA_KV_GROUPS, n, width), BF16)
    return pl.pallas_call(
        functools.partial(_nsa_prep_kernel, seq=seq),
        grid=(n // tm,),
        in_specs=[pl.BlockSpec((tm, 512), lambda i: (i, 0)),
                  kv_spec(2), kv_spec(3), kv_spec(4), kv_spec(5),
                  pl.BlockSpec((tm, LANES), lambda i: (i, 0)), full(gq2), full(gk2)],
        out_specs=[pl.BlockSpec((NSA_HEADS, tm, LANES), lambda i: (0, i, 0)),
                   g_spec(2 * LANES), g_spec(LANES), g_spec(LANES), g_spec(LANES),
                   pl.BlockSpec((NSA_KV_GROUPS, tm, LANES), lambda i: (0, i, 0))],
        out_shape=[jax.ShapeDtypeStruct((NSA_HEADS, n, LANES), BF16),
                   g_sds(2 * LANES), g_sds(LANES), g_sds(LANES), g_sds(LANES),
                   jax.ShapeDtypeStruct((NSA_KV_GROUPS, n, LANES), F32)],
        compiler_params=_cparams(("parallel",)),
        name="nsa_prep",
    )(qn, kv, kv, kv, kv, misc, gq2, gk2)


def _nsa_select_kernel(q_ref, kc_ref, vct_ref, tct_ref, ovlt_ref, ocmpt_ref, sel_ref, s_ref, *, nc, n_sel):
    tq, hpg = NSA_TQS, NSA_HPG
    i = pl.program_id(2)
    q0 = i * tq
    n_forced = 3
    n_live = jnp.minimum((q0 + tq - CMP_LEN) // (CMP_STRIDE * LANES) + 1, nc // LANES)
    for ncb in range(1, nc // LANES + 1):
        pl.when(n_live == ncb)(functools.partial(
            _nsa_select_body, q_ref, kc_ref, vct_ref, tct_ref, ovlt_ref, ocmpt_ref, sel_ref, s_ref,
            ncb=ncb, n_sel=n_sel, n_forced=n_forced))


def _nsa_select_body(q_ref, kc_ref, vct_ref, tct_ref, ovlt_ref, ocmpt_ref, sel_ref, s_ref, *, ncb, n_sel, n_forced):
    tq, hpg = NSA_TQS, NSA_HPG
    i = pl.program_id(2)
    q0 = i * tq
    width = ncb * LANES
    s_ref[:width, :] = _dot_nt(kc_ref[0, 0, :width, :], q_ref[...].reshape(hpg * tq, LANES))

    first_blk = ((tq // CMP_STRIDE) * i + LANES - CMP_WINDOW_BACK) // LANES - 1
    for half in range(2):
        blk = first_blk + half

        @pl.when((blk >= 0) & (blk < ncb))
        def _():
            r0 = pl.multiple_of(blk * LANES, LANES)
            for h in range(hpg):
                s_ref[pl.ds(r0, LANES), h * tq:(h + 1) * tq] += tct_ref[0, h, half * LANES:(half + 1) * LANES, :]

    blk_row = lax.broadcasted_iota(jnp.int32, (width, tq), 0)
    qpos = q0 + lax.broadcasted_iota(jnp.int32, (width, tq), 1)
    valid = (CMP_STRIDE * blk_row + (CMP_LEN - 1)) <= qpos
    vct = vct_ref[0, 0, :, :width]
    psum = jnp.zeros((width, tq), F32)
    for h in range(hpg):
        sh = jnp.where(valid, s_ref[:width, h * tq:(h + 1) * tq], -jnp.inf)
        m = jnp.maximum(jnp.max(sh, axis=0, keepdims=True), NEG_INF)
        e = jnp.exp2(sh - m)
        p = e * (1.0 / jnp.maximum(jnp.sum(e, axis=0, keepdims=True), 1e-30))
        psum = psum + p
        ocmpt_ref[h * NSA_HD:(h + 1) * NSA_HD, :] = _dot(vct, p.astype(BF16))

    hi = psum.astype(BF16)
    rest = psum - hi.astype(F32)
    mid = rest.astype(BF16)
    lo = (rest - mid.astype(F32)).astype(BF16)
    ovlt = ovlt_ref[:, :width]
    imp = _dot(ovlt, hi) + _dot(ovlt, mid) + _dot(ovlt, lo)

    jrow = lax.broadcasted_iota(jnp.int32, (LANES, tq), 0)
    qpos = q0 + lax.broadcasted_iota(jnp.int32, (LANES, tq), 1)
    jf = jrow.astype(F32)
    cur = qpos // SLC_LEN
    forced = (jrow == 0) | (jrow == cur) | (jrow == cur - 1)
    imp = jnp.where(jrow * SLC_LEN <= qpos, imp, NEG_INF)
    if n_sel >= n_forced:
        imp, rounds = jnp.where(forced, -jnp.inf, imp), n_sel - n_forced
    else:
        imp, rounds = jnp.where(forced, FORCE_SCORE, imp), n_sel
    for _ in range(rounds):
        best = jnp.max(imp, axis=0, keepdims=True)
        first = jnp.min(jnp.where(imp == best, jf, float(LANES)), axis=0, keepdims=True)
        imp = jnp.where(jf == first, -jnp.inf, imp)
    sel_ref[0] = jnp.where(imp == -jnp.inf, 0.0, MASK_BIG).T.astype(BF16)


def _nsa_select(q, kcmp, vcmp_t, tct, ovl_t, batch, seq):
    n = q.shape[1]
    nc = kcmp.shape[2]
    nq = seq // NSA_TQS
    n_res = tct.shape[0]
    n_sel = min(SLC_TOPK, seq // SLC_LEN)
    g_, hpg = NSA_KV_GROUPS, NSA_HPG
    kern = functools.partial(_nsa_select_kernel, nc=nc, n_sel=n_sel)
    cmp_spec = lambda shape: pl.BlockSpec((1, 1) + shape, lambda b, g, i: (b, g, 0, 0))
    return pl.pallas_call(
        kern,
        grid=(batch, g_, nq),
        in_specs=[pl.BlockSpec((hpg, NSA_TQS, LANES), lambda b, g, i: (g, b * nq + i, 0)),
                  cmp_spec((nc, LANES)), cmp_spec((NSA_HD, nc)),
                  pl.BlockSpec((1, hpg, 2 * LANES, NSA_TQS), lambda b, g, i: (i % n_res, g, 0, 0)),
                  pl.BlockSpec(ovl_t.shape, lambda b, g, i: (0, 0))],
        out_specs=[pl.BlockSpec((hpg * NSA_HD, NSA_TQS), lambda b, g, i: (g, b * nq + i)),
                   pl.BlockSpec((1, NSA_TQS, LANES), lambda b, g, i: (g, b * nq + i, 0))],
        out_shape=[jax.ShapeDtypeStruct((NSA_HEADS * NSA_HD, n), F32),
                   jax.ShapeDtypeStruct((g_, n, LANES), BF16)],
        scratch_shapes=[pltpu.VMEM((nc, hpg * NSA_TQS), F32)],
        compiler_params=_cparams(("parallel", "parallel", "arbitrary")),
        name="nsa_select",
    )(q, kcmp, vcmp_t, tct, ovl_t)


def _nsa_attn_kernel(q_ref, ks_ref, vs_ref, kw_ref, vw_ref, sel_ref, tn_ref, gate_ref, ocmpt_ref,
                     o_ref, m_ref, acc_ref):
    tq, tk, hpg = NSA_TQ, NSA_TK, NSA_HPG
    rows = hpg * tq
    i = pl.program_id(2)
    q0 = i * tq
    q4 = q_ref[...].reshape(rows, LANES)
    q_aug = jnp.concatenate([jnp.concatenate([sel_ref[0]] * hpg, axis=0), q4], axis=1)
    row = lax.broadcasted_iota(jnp.int32, (tq, tk), 0)
    col = lax.broadcasted_iota(jnp.int32, (tq, tk), 1)
    m_ref[...] = jnp.full(m_ref.shape, NEG_INF, F32)
    acc_ref[...] = jnp.zeros(acc_ref.shape, F32)

    def tile(kt, near):
        k0 = pl.multiple_of(kt * tk, tk)
        s = _dot_nt(q_aug, ks_ref[0, 0, pl.ds(k0, tk), :])
        if near:
            causal = (k0 + col) <= (q0 + row)
            start = pl.multiple_of(k0 + tk + WIN - q0, LANES)
        ps, alphas = [], []
        for h in range(hpg):
            rs = slice(h * tq, (h + 1) * tq)
            sh = s[rs]
            if near:
                sh = jnp.where(causal, sh + tn_ref[h, :, pl.ds(start, tk)], -jnp.inf)
            m_old = m_ref[rs]
            m_new, p = _softmax_weights(sh, m_old)
            m_ref[rs] = m_new
            alphas.append(jnp.exp2(m_old - m_new))
            ps.append(p)
        acc_ref[...] = (jnp.concatenate(alphas, axis=0) * acc_ref[...]
                        + _dot(jnp.concatenate(ps, axis=0), vs_ref[0, 0, pl.ds(k0, tk), :]))

    last = q0 // tk
    n_far = jnp.maximum(last - 1, 0)
    def far_pair(pair, c):
        tile(2 * pair, False)
        tile(2 * pair + 1, False)
        return c

    lax.fori_loop(0, n_far // 2, far_pair, 0)
    lax.fori_loop(n_far // 2 * 2, n_far, lambda kt, c: (tile(kt, False), c)[1], 0)
    lax.fori_loop(n_far, last + 1, lambda kt, c: (tile(kt, True), c)[1], 0)

    wk = WIN + tq
    w0 = pl.multiple_of(q0, tq)
    s = _dot_nt(q4, kw_ref[0, 0, pl.ds(w0, wk), :])
    roww = lax.broadcasted_iota(jnp.int32, (tq, wk), 0)
    colw = lax.broadcasted_iota(jnp.int32, (tq, wk), 1)
    dist = roww + WIN - colw
    validw = (dist >= 0) & (dist < WIN) & (q0 - WIN + colw >= 0)
    ps = []
    for h in range(hpg):
        sh = s[h * tq:(h + 1) * tq] + tn_ref[h, :, tk:tk + wk]
        ps.append(_softmax_weights(jnp.where(validw, sh, -jnp.inf), NEG_INF)[1])
    accw = _dot(jnp.concatenate(ps, axis=0), vw_ref[0, 0, pl.ds(w0, wk), :])

    gates = gate_ref[0]
    acc = acc_ref[...]
    outs = []
    for h in range(hpg):
        rs = slice(h * tq, (h + 1) * tq)
        o_cmp = ocmpt_ref[h * NSA_HD:(h + 1) * NSA_HD, :].T
        o_slc = acc[rs, :NSA_HD] / jnp.maximum(acc[rs, NSA_HD:NSA_HD + 1], 1e-30)
        o_win = accw[rs, :NSA_HD] / accw[rs, NSA_HD:NSA_HD + 1]
        outs.append(gates[:, 3 * h:3 * h + 1] * o_cmp + gates[:, 3 * h + 1:3 * h + 2] * o_slc
                    + gates[:, 3 * h + 2:3 * h + 3] * o_win)
    o_ref[...] = jnp.concatenate(outs, axis=1)


def _nsa_attn(q, ks, vs, kwp, vwp, sel, tn, gates, ocmp_t, batch, seq):
    n = q.shape[1]
    nq = seq // NSA_TQ
    g_, hpg = NSA_KV_GROUPS, NSA_HPG
    seq_spec = lambda length, width: pl.BlockSpec((1, 1, length, width), lambda b, g, i: (g, b, 0, 0))
    tok_spec = pl.BlockSpec((1, NSA_TQ, LANES), lambda b, g, i: (g, b * nq + i, 0))
    out_spec = pl.BlockSpec((NSA_TQ, hpg * NSA_HD), lambda b, g, i: (b * nq + i, g))
    return pl.pallas_call(
        _nsa_attn_kernel,
        grid=(batch, g_, nq),
        in_specs=[pl.BlockSpec((hpg, NSA_TQ, LANES), lambda b, g, i: (g, b * nq + i, 0)),
                  seq_spec(seq, 2 * LANES), seq_spec(seq, LANES), seq_spec(seq + WIN, LANES), seq_spec(seq + WIN, LANES),
                  tok_spec,
                  pl.BlockSpec((hpg, NSA_TQ, 2 * NSA_TK + WIN), lambda b, g, i: (g, 0, 0)),
                  tok_spec, pl.BlockSpec((hpg * NSA_HD, NSA_TQ), lambda b, g, i: (g, b * nq + i))],
        out_specs=out_spec,
        out_shape=jax.ShapeDtypeStruct((n, NSA_HEADS * NSA_HD), F32),
        scratch_shapes=[pltpu.VMEM((hpg * NSA_TQ, LANES), F32), pltpu.VMEM((hpg * NSA_TQ, LANES), F32)],
        compiler_params=_cparams(("parallel", "parallel", "arbitrary")),
        name="nsa_attn",
    )(q, ks, vs, kwp, vwp, sel, tn, gates, ocmp_t)


def _mem_kv_kernel(mem_ref, g_ref, w_ref, gk_ref, k_ref, v_ref):
    mn = _rms(mem_ref[0], g_ref[...]).astype(BF16)
    kv = _dot(mn, w_ref[...])
    width = MEM_HEADS * MEM_HD
    for c in range(width // LANES):
        kn = _seg_rms64(kv[:, c * LANES:(c + 1) * LANES], gk_ref[...])
        vv = kv[:, width + c * LANES:width + (c + 1) * LANES]
        for half in range(2):
            sl = slice(half * MEM_HD, (half + 1) * MEM_HD)
            k_ref[0, 2 * c + half] = kn[:, sl].astype(BF16)
            v_ref[0, 2 * c + half] = vv[:, sl].astype(BF16)


def _mem_kv(mem, g, w, gk2):
    b, m, d = mem.shape
    full = lambda a: pl.BlockSpec(a.shape, lambda i: (0,) * a.ndim)
    spec = pl.BlockSpec((1, MEM_HEADS, m, MEM_HD), lambda i: (i, 0, 0, 0))
    sds = jax.ShapeDtypeStruct((b, MEM_HEADS, m, MEM_HD), BF16)
    return pl.pallas_call(
        _mem_kv_kernel,
        grid=(b,),
        in_specs=[pl.BlockSpec((1, m, d), lambda i: (i, 0, 0)), full(g), full(w), full(gk2)],
        out_specs=[spec, spec],
        out_shape=[sds, sds],
        compiler_params=_cparams(("parallel",)),
        name="mem_kv",
    )(mem, g, w, gk2)


def _post_attn_kernel(x_ref, ya_ref, yb_ref, goa_ref, gob_ref, woa_ref, wob_ref, gmn_ref, wmq_ref, gmq_ref,
                      km_ref, vm_ref, wmo_ref, gmoe_ref, rwh_ref, rwl_ref, rb_ref,
                      x2_ref, h2_ref, topi_ref, gate_ref, cnt_ref):
    first = (pl.program_id(0) == 0) & (pl.program_id(1) == 0)

    @pl.when(first)
    def _():
        cnt_ref[...] = jnp.zeros(cnt_ref.shape, F32)

    mixa = _rms(ya_ref[...], goa_ref[...]).astype(BF16)
    mixb = _rms(yb_ref[...], gob_ref[...]).astype(BF16)
    x1 = x_ref[...] + _dot(mixa, woa_ref[...]) + _dot(mixb, wob_ref[...])

    h = _rms(x1, gmn_ref[...]).astype(BF16)
    q = _dot(h, wmq_ref[...])
    scale = MEM_HD ** -0.5
    outs = []
    for c in range(MEM_HEADS * MEM_HD // LANES):
        qn = _seg_rms64(q[:, c * LANES:(c + 1) * LANES], gmq_ref[...]) * scale
        for half in range(2):
            hd = 2 * c + half
            qh = qn[:, half * MEM_HD:(half + 1) * MEM_HD].astype(BF16)
            s = _dot_nt(qh, km_ref[0, hd])
            e = jnp.exp(s - jnp.max(s, axis=-1, keepdims=True))
            p = e / jnp.sum(e, axis=-1, keepdims=True)
            outs.append(_dot(p.astype(BF16), vm_ref[0, hd]))
    o = jnp.concatenate(outs, axis=1).astype(BF16)
    x2 = x1 + _dot(o, wmo_ref[...])
    x2_ref[...] = x2

    h2 = _rms(x2, gmoe_ref[...])
    for c in range(ROW_TILE):
        h2_ref[pl.ds(c, h2.shape[0], stride=ROW_TILE), :] = h2[:, c * LANES:(c + 1) * LANES]
    h_hi = h2.astype(BF16)
    h_lo = (h2 - h_hi.astype(F32)).astype(BF16)
    logits = _dot(h_hi, rwh_ref[...]) + _dot(h_hi, rwl_ref[...]) + _dot(h_lo, rwh_ref[...]) + rb_ref[...]
    lane = lax.broadcasted_iota(jnp.int32, logits.shape, 1)
    lane_f = lane.astype(F32)
    topi = jnp.zeros(logits.shape, jnp.int32)
    topv = jnp.full(logits.shape, NEG_INF, F32)
    onehot = jnp.zeros(logits.shape, F32)
    for k in range(TOP_K):
        best = jnp.max(logits, axis=-1, keepdims=True)
        first_idx = jnp.min(jnp.where(logits == best, lane_f, float(LANES)), axis=-1, keepdims=True)
        hit = lane_f == first_idx
        topi = jnp.where(lane == k, first_idx.astype(jnp.int32), topi)
        topv = jnp.where(lane == k, best, topv)
        onehot = jnp.where(hit, 1.0, onehot)
        logits = jnp.where(hit, -jnp.inf, logits)
    e = jnp.where(lane < TOP_K, jnp.exp(topv - jnp.max(topv, axis=-1, keepdims=True)), 0.0)
    gate_ref[...] = e / jnp.sum(e, axis=-1, keepdims=True)
    topi_ref[...] = topi
    cnt_ref[...] += jnp.sum(onehot, axis=0, keepdims=True)


def _post_attn(x2d, ya, yb, goa, gob, woa, wob, gmn, wmq, gmq2, km, vm, wmo, gmoe, rwh, rwl, rb, batch, seq):
    n, d = x2d.shape
    tm = 256
    ns = seq // tm
    full = lambda a: pl.BlockSpec(a.shape, lambda b, i: (0,) * a.ndim)
    tok = lambda width: pl.BlockSpec((tm, width), lambda b, i: (b * ns + i, 0))
    mem_spec = pl.BlockSpec((1,) + km.shape[1:], lambda b, i: (b, 0, 0, 0))
    return pl.pallas_call(
        _post_attn_kernel,
        grid=(batch, ns),
        in_specs=[tok(d), tok(ya.shape[1]), tok(yb.shape[1]), full(goa), full(gob), full(woa), full(wob),
                  full(gmn), full(wmq), full(gmq2), mem_spec, mem_spec, full(wmo), full(gmoe),
                  full(rwh), full(rwl), full(rb)],
        out_specs=[tok(d), pl.BlockSpec((tm * ROW_TILE, LANES), lambda b, i: (b * ns + i, 0)),
                   tok(LANES), tok(LANES), pl.BlockSpec((1, LANES), lambda b, i: (0, 0))],
        out_shape=[jax.ShapeDtypeStruct((n, d), F32),
                   jax.ShapeDtypeStruct((n * ROW_TILE, LANES), F32),
                   jax.ShapeDtypeStruct((n, LANES), jnp.int32),
                   jax.ShapeDtypeStruct((n, LANES), F32),
                   jax.ShapeDtypeStruct((1, LANES), F32)],
        compiler_params=_cparams(("arbitrary", "arbitrary")),
        name="post_attn",
    )(x2d, ya, yb, goa, gob, woa, wob, gmn, wmq, gmq2, km, vm, wmo, gmoe, rwh, rwl, rb)


def _moe_pos_kernel(topi_ref, start_ref, tri_ref, pos_ref, carry_ref):
    @pl.when(pl.program_id(0) == 0)
    def _():
        carry_ref[...] = jnp.zeros(carry_ref.shape, F32)

    topi = topi_ref[...]
    lane = lax.broadcasted_iota(jnp.int32, topi.shape, 1)
    hits = [lane == topi[:, k:k + 1] for k in range(TOP_K)]
    onehot = sum(h.astype(F32) for h in hits)
    before = _dot(tri_ref[...], onehot.astype(BF16))
    base = start_ref[...] + carry_ref[...] + before
    pos = jnp.zeros(topi.shape, jnp.int32)
    for k in range(TOP_K):
        pk = jnp.sum(jnp.where(hits[k], base, 0.0), axis=-1, keepdims=True).astype(jnp.int32)
        pos = jnp.where(lane == k, pk, pos)
    pos_ref[...] = pos
    carry_ref[...] += jnp.sum(onehot, axis=0, keepdims=True)


def _moe_pos(topi, pad_start, tri):
    n = topi.shape[0]
    tm = tri.shape[0]
    return pl.pallas_call(
        _moe_pos_kernel,
        grid=(n // tm,),
        in_specs=[pl.BlockSpec((tm, LANES), lambda i: (i, 0)),
                  pl.BlockSpec((1, LANES), lambda i: (0, 0)),
                  pl.BlockSpec((tm, tm), lambda i: (0, 0))],
        out_specs=pl.BlockSpec((tm, LANES), lambda i: (i, 0)),
        out_shape=jax.ShapeDtypeStruct((n, LANES), jnp.int32),
        scratch_shapes=[pltpu.VMEM((1, LANES), F32)],
        compiler_params=_cparams(("arbitrary",)),
        name="moe_pos",
    )(topi, pad_start, tri)


def _moe_scatter_kernel(pos_ref, h_ref, zero_ref, xs_ref, sem, *, tm):
    del zero_ref

    def copy(t, k):
        dst = pl.multiple_of(pos_ref[t * TOP_K + k] * ROW_TILE, ROW_TILE)
        return pltpu.make_async_copy(h_ref.at[pl.ds(pl.multiple_of(t * ROW_TILE, ROW_TILE), ROW_TILE)],
                                     xs_ref.at[pl.ds(dst, ROW_TILE)], sem)

    def issue(t, c):
        for k in range(TOP_K):
            copy(t, k).start(priority=k % 2)
        return c

    def drain(t, c):
        for k in range(TOP_K):
            copy(t, k).wait()
        return c

    lax.fori_loop(0, tm, issue, 0, unroll=4)
    lax.fori_loop(0, tm, drain, 0, unroll=4)


def _moe_scatter(pos_flat, h2, xs_zero):
    n = h2.shape[0] // ROW_TILE
    tm = 512
    kern = functools.partial(_moe_scatter_kernel, tm=tm)
    return pl.pallas_call(
        kern,
        grid=(n // tm,),
        in_specs=[pl.BlockSpec((tm * TOP_K,), lambda i: (i,), memory_space=pltpu.SMEM),
                  pl.BlockSpec((tm * ROW_TILE, LANES), lambda i: (i, 0)),
                  pl.BlockSpec(memory_space=pl.ANY)],
        out_specs=pl.BlockSpec(memory_space=pl.ANY),
        out_shape=jax.ShapeDtypeStruct(xs_zero.shape, xs_zero.dtype),
        scratch_shapes=[pltpu.SemaphoreType.DMA(())],
        input_output_aliases={2: 0},
        compiler_params=_cparams(("arbitrary",)),
        name="moe_scatter",
    )(pos_flat, h2, xs_zero)


GU_GROUP = 2 * LANES


def _moe_ffn_kernel(blk_e_ref, n_used_ref, x_ref, wgu_ref, perm_ref, bg_ref, bu_ref, wd_ref, bd_ref, y_ref,
                    wgu_bf_ref, wd_bf_ref):
    i = pl.program_id(0)
    chunk = lambda c: pl.ds(c, MOE_BLK, stride=ROW_TILE)
    new_expert = (i == 0) | (blk_e_ref[i] != blk_e_ref[jnp.maximum(i - 1, 0)])

    @pl.when(new_expert & (i < n_used_ref[0]))
    def _():
        for c in range(wgu_ref.shape[2] // GU_GROUP):
            sl = slice(c * GU_GROUP, (c + 1) * GU_GROUP)
            wgu_bf_ref[:, sl] = _dot(wgu_ref[0, :, sl].astype(BF16), perm_ref[...]).astype(BF16)
        wd_bf_ref[...] = wd_ref[0].astype(BF16)

    @pl.when(i < n_used_ref[0])
    def _():
        x = jnp.concatenate([x_ref[chunk(c), :] for c in range(ROW_TILE)], axis=1).astype(BF16)
        gu = _dot(x, wgu_bf_ref[...])
        bg, bu = bg_ref[0], bu_ref[0]
        acts = []
        for c in range(gu.shape[1] // GU_GROUP):
            fs = slice(c * LANES, (c + 1) * LANES)
            gate = jnp.minimum(gu[:, c * GU_GROUP:c * GU_GROUP + LANES] + bg[:, fs], SWIGLU_LIMIT)
            up = jnp.clip(gu[:, c * GU_GROUP + LANES:(c + 1) * GU_GROUP] + bu[:, fs], -SWIGLU_LIMIT, SWIGLU_LIMIT)
            acts.append(((up + 1.0) * gate * jax.nn.sigmoid(SWIGLU_ALPHA * gate)).astype(BF16))
        act = jnp.concatenate(acts, axis=1)
        y = _dot(act, wd_bf_ref[...]) + bd_ref[0]
        for c in range(ROW_TILE):
            y_ref[chunk(c), :] = y[:, c * LANES:(c + 1) * LANES]

    @pl.when(i >= n_used_ref[0])
    def _():
        y_ref[...] = jnp.zeros(y_ref.shape, F32)


def _moe_ffn(blk_e, n_used, xs, wgu, perm, bg, bu, wd, bd):
    p = xs.shape[0] // ROW_TILE
    dff, d = wd.shape[1:]
    n_blk = p // MOE_BLK
    w_spec = lambda r, c: pl.BlockSpec((1, r, c), lambda i, be, nu: (be[i], 0, 0))
    row_spec = pl.BlockSpec((MOE_BLK * ROW_TILE, LANES), lambda i, be, nu: (i, 0))
    grid_spec = pltpu.PrefetchScalarGridSpec(
        num_scalar_prefetch=2,
        grid=(n_blk,),
        in_specs=[row_spec,
                  w_spec(d, 2 * dff), pl.BlockSpec(perm.shape, lambda i, be, nu: (0, 0)), w_spec(1, dff), w_spec(1, dff),
                  w_spec(dff, d), w_spec(1, d)],
        out_specs=row_spec,
        scratch_shapes=[pltpu.VMEM((d, 2 * dff), BF16), pltpu.VMEM((dff, d), BF16)],
    )
    return pl.pallas_call(
        _moe_ffn_kernel,
        grid_spec=grid_spec,
        out_shape=jax.ShapeDtypeStruct(xs.shape, F32),
        compiler_params=_cparams(("arbitrary",)),
        name="moe_ffn",
    )(blk_e, n_used, xs, wgu, perm, bg, bu, wd, bd)


def _moe_combine_kernel(pos_ref, gate_ref, x_ref, ys_ref, o_ref, buf_ref, sem, *, tm):
    def copy(t, k):
        src = pl.multiple_of(pos_ref[t * TOP_K + k] * ROW_TILE, ROW_TILE)
        return pltpu.make_async_copy(ys_ref.at[pl.ds(src, ROW_TILE)],
                                     buf_ref.at[k, pl.ds(pl.multiple_of(t * ROW_TILE, ROW_TILE), ROW_TILE)], sem)

    def issue(t, c):
        for k in range(TOP_K):
            copy(t, k).start(priority=k % 2)
        return c

    def drain(t, c):
        for k in range(TOP_K):
            copy(t, k).wait()
        return c

    lax.fori_loop(0, tm, issue, 0, unroll=4)
    lax.fori_loop(0, tm, drain, 0, unroll=4)
    gates = gate_ref[...]
    for c in range(ROW_TILE):
        cs = slice(c * LANES, (c + 1) * LANES)
        acc = x_ref[:, cs]
        for k in range(TOP_K):
            acc = acc + gates[:, k:k + 1] * buf_ref[k, pl.ds(c, tm, stride=ROW_TILE), :]
        o_ref[:, cs] = acc


def _moe_combine(pos_flat, gates, x2, ys):
    n, d = x2.shape
    tm = 256
    kern = functools.partial(_moe_combine_kernel, tm=tm)
    return pl.pallas_call(
        kern,
        grid=(n // tm,),
        in_specs=[pl.BlockSpec((tm * TOP_K,), lambda i: (i,), memory_space=pltpu.SMEM),
                  pl.BlockSpec((tm, LANES), lambda i: (i, 0)),
                  pl.BlockSpec((tm, d), lambda i: (i, 0)),
                  pl.BlockSpec(memory_space=pl.ANY)],
        out_specs=pl.BlockSpec((tm, d), lambda i: (i, 0)),
        out_shape=jax.ShapeDtypeStruct((n, d), F32),
        scratch_shapes=[pltpu.VMEM((TOP_K, tm * ROW_TILE, LANES), F32), pltpu.SemaphoreType.DMA(())],
        compiler_params=_cparams(("arbitrary",)),
        name="moe_combine",
    )(pos_flat, gates, x2, ys)


def _rel_bucket_np(dist):
    n = np.maximum(dist, 0)
    nf = np.maximum(n, 1).astype(np.float32)
    ratio = np.log(nf / np.float32(REL_MAX_EXACT)) / np.float32(math.log(REL_MAX_DIST / REL_MAX_EXACT))
    large = REL_MAX_EXACT + (ratio * np.float32(REL_BUCKETS - REL_MAX_EXACT)).astype(np.int32)
    large = np.minimum(large, REL_BUCKETS - 1)
    return np.where(n < REL_MAX_EXACT, n, large)


def _bias_tables_kernel(tbl_ref, bc_ref, bt_ref, tc_ref, tn_ref):
    h = pl.program_id(0)
    bc, bt = bc_ref[...], bt_ref[...]
    tc = jnp.zeros(bc.shape, F32)
    tn = jnp.zeros(bt.shape, F32)
    far = tbl_ref[REL_BUCKETS - 1, h]
    for b in range(REL_BUCKETS - 1):
        v = (tbl_ref[b, h] - far) * LOG2E
        tc = jnp.where(bc == b, v, tc)
        tn = jnp.where(bt == b, v, tn)
    tc_ref[:, 0] = tc
    tn_ref[0] = tn


def _bias_tables(rel_table):
    q = np.arange(NSA_TQS)[None, :, None]
    blocks_per_tile = NSA_TQS // CMP_STRIDE
    n_res = LANES // blocks_per_tile
    res = np.arange(n_res)[:, None, None]
    base_gap = (blocks_per_tile * res - CMP_WINDOW_BACK) % LANES + CMP_WINDOW_BACK
    w = np.arange(2 * LANES)[None, None, :]
    b_cmp = _rel_bucket_np(CMP_STRIDE * base_gap + q - CMP_STRIDE * w - (CMP_LEN - 1)).astype(np.int32)
    b_cmp = np.ascontiguousarray(b_cmp.transpose(0, 2, 1))
    c = np.arange(2 * NSA_TK + WIN)[None, :]
    b_tok = _rel_bucket_np(np.arange(NSA_TQ)[:, None] + WIN - (c - NSA_TK)).astype(np.int32)
    heads = rel_table.shape[1]
    return pl.pallas_call(
        _bias_tables_kernel,
        grid=(heads,),
        in_specs=[pl.BlockSpec(memory_space=pltpu.SMEM),
                  pl.BlockSpec(b_cmp.shape, lambda h: (0, 0, 0)),
                  pl.BlockSpec(b_tok.shape, lambda h: (0, 0))],
        out_specs=[pl.BlockSpec((n_res, 1) + b_cmp.shape[1:], lambda h: (0, h, 0, 0)),
                   pl.BlockSpec((1,) + b_tok.shape, lambda h: (h, 0, 0))],
        out_shape=[jax.ShapeDtypeStruct((n_res, heads) + b_cmp.shape[1:], F32),
                   jax.ShapeDtypeStruct((heads,) + b_tok.shape, F32)],
        compiler_params=_cparams(("parallel",)),
        name="bias_tables",
    )(rel_table.astype(F32), jnp.asarray(b_cmp), jnp.asarray(b_tok))


def kernel(x, mem, g_attn_norm, w_in, g_cq, w_uq, g_ckv, w_ukv, g_q_mla, g_k_mla, cmp_k_pos, cmp_k_w1, cmp_k_w2, cmp_v_pos, cmp_v_w1, cmp_v_w2, g_q_nsa, g_k_nsa, rel_table, g_out_mla, g_out_nsa, w_out, g_mem_norm, g_mem_src, w_mq, w_mkv, g_mq, g_mk, w_mo, g_moe_norm, router_w, router_b, w_gate_up, b_gate_up, w_down, b_down):
    batch, seq, d = x.shape
    n = batch * seq
    depth = w_in.shape[0]
    assert seq % 512 == 0 and seq // SLC_LEN <= LANES and d == ROW_TILE * LANES
    row = lambda v: v.reshape(1, -1).astype(F32)
    tile2 = lambda v: jnp.concatenate([v, v]).reshape(1, -1).astype(F32)

    x2d = x.reshape(n, d)
    for l in range(depth):
        wi = w_in[l]
        w_in_r = jnp.concatenate(
            [wi[:, 0:384], wi[:, 416:928], wi[:, 928:1696], wi[:, 384:416], wi[:, 1696:1720],
             jnp.zeros((d, IN_COLS_PAD - 1720), wi.dtype)], axis=1).astype(BF16)
        dq = MLA_NOPE + MLA_ROPE
        half = MLA_ROPE // 2
        x1, x2 = slice(MLA_NOPE, MLA_NOPE + half), slice(MLA_NOPE + half, dq)
        head_pad = lambda w, width: jnp.pad(w, ((0, 0), (0, 0), (0, LANES - width))).reshape(
            w.shape[0], MLA_HEADS * LANES).astype(BF16)
        wuq3 = w_uq[l].reshape(MLA_Q_RANK, MLA_HEADS, dq)
        wuq_r = head_pad(wuq3, dq)
        wuq_s = head_pad(jnp.concatenate([jnp.zeros_like(wuq3[:, :, :MLA_NOPE]), wuq3[:, :, x2], wuq3[:, :, x1]], 2), dq)
        wukv = w_ukv[l].reshape(MLA_KV_RANK, MLA_HEADS, MLA_NOPE + MLA_V)
        wuk_r = head_pad(wukv[:, :, :MLA_NOPE], MLA_NOPE)
        wuv_r = head_pad(wukv[:, :, MLA_NOPE:], MLA_V)

        inv = ROPE_THETA ** (-jnp.arange(half, dtype=F32) / half)
        ang = jnp.arange(seq, dtype=F32)[:, None] * inv
        cos, sin = jnp.cos(ang), jnp.sin(ang)
        lane_pad = jnp.zeros((seq, LANES - dq), F32)
        rot_c = jnp.concatenate([jnp.ones((seq, MLA_NOPE), F32), cos, cos, lane_pad], 1)
        rot_s = jnp.concatenate([jnp.zeros((seq, MLA_NOPE), F32), -sin, sin, lane_pad], 1)

        def rope_tables(g, scale):
            g_pad = jnp.pad(g, (0, LANES - dq))
            g_swp = jnp.pad(jnp.concatenate([g[:MLA_NOPE], g[x2], g[x1]]), (0, LANES - dq))
            return rot_c * (g_pad * scale), rot_s * (g_swp * scale)

        cq0, cq1 = rope_tables(g_q_mla[l], dq ** -0.5 * LOG2E)
        ck0, ck1 = rope_tables(g_k_mla[l], 1.0)

        mla_in, qn, kv6, misc = _in_proj(x2d, row(g_attn_norm[l]), w_in_r)
        q_m, k_m, v_m = _mla_prep(mla_in, misc, row(g_cq[l]), wuq_r, wuq_s, row(g_ckv[l]), wuk_r, wuv_r,
                                  cq0, cq1, ck0, ck1, seq)
        y_mla = _mla_attn(q_m, k_m, v_m, batch, seq)

        nc = seq // CMP_STRIDE
        half_len = CMP_LEN // 2
        eye_g = jnp.eye(NSA_KV_GROUPS, dtype=F32)

        def cmp_weights(pos, w1, w2):
            out = []
            for part in range(2):
                sl = slice(part * half_len, (part + 1) * half_len)
                out.append(jnp.broadcast_to(pos[sl][:, None, :], (half_len, NSA_KV_GROUPS, NSA_HD))
                           .reshape(1, -1))
            for part in range(2):
                sl = slice(part * half_len, (part + 1) * half_len)
                wexp = jnp.einsum('ldf,gh->lgdhf', w1[sl], eye_g)
                out.append(wexp.reshape(half_len * NSA_KV_GROUPS * NSA_HD, NSA_KV_GROUPS * CMP_HIDDEN).astype(BF16))
            out.append(w2.astype(BF16))
            return out

        chunk_w = CMP_STRIDE * NSA_KV_GROUPS * NSA_HD
        kcmp, vcmp = _nsa_cmp(kv6[0].reshape(batch, nc, chunk_w), kv6[1].reshape(batch, nc, chunk_w),
                              cmp_weights(cmp_k_pos[l], cmp_k_w1[l], cmp_k_w2[l]),
                              cmp_weights(cmp_v_pos[l], cmp_v_w1[l], cmp_v_w2[l]), row(g_k_nsa[l]))
        q_n, ks, vs, kw, vw, gates_n = _nsa_prep(qn, kv6, misc, tile2(g_q_nsa[l]), tile2(g_k_nsa[l]), seq)
        by_batch = lambda t: t.reshape(NSA_KV_GROUPS, batch, seq, t.shape[-1])
        front_pad = lambda t: jnp.pad(by_batch(t), ((0, 0), (0, 0), (WIN, 0), (0, 0)))
        tc, tn = _bias_tables(rel_table)
        n_idx = np.arange(nc)[:, None]
        j_idx = np.arange(LANES)[None, :]
        ovl = ((CMP_STRIDE * n_idx < SLC_LEN * j_idx + SLC_LEN)
               & (CMP_STRIDE * n_idx + CMP_LEN - 1 >= SLC_LEN * j_idx)
               & (n_idx < nc - 1) & (j_idx < seq // SLC_LEN)).astype(np.float32)
        o_cmp_t, sel = _nsa_select(q_n, kcmp, jnp.swapaxes(vcmp, 2, 3), tc, jnp.asarray(ovl.T, dtype=BF16),
                                   batch, seq)
        y_nsa = _nsa_attn(q_n, by_batch(ks), by_batch(vs), front_pad(kw), front_pad(vw), sel,
                          tn, gates_n, o_cmp_t, batch, seq)

        k_mem, v_mem = _mem_kv(mem, row(g_mem_src[l]), w_mkv[l].reshape(d, 2 * MEM_HEADS * MEM_HD).astype(BF16),
                               tile2(g_mk[l]))
        rw = jnp.pad(router_w[l], ((0, 0), (0, LANES - N_EXPERTS)))
        rw_hi = rw.astype(BF16)
        rw_lo = (rw - rw_hi.astype(F32)).astype(BF16)
        rb =jnp.pad(router_b[l], (0, LANES - N_EXPERTS), constant_values=NEG_INF).reshape(1, LANES)
        wo = w_out[l].astype(BF16)
        n_mla = MLA_HEADS * MLA_V
        x2, h2, topi, gates_e, counts = _post_attn(
            x2d, y_mla, y_nsa, row(g_out_mla[l]), row(g_out_nsa[l]), wo[:n_mla], wo[n_mla:],
            row(g_mem_norm[l]), w_mq[l].astype(BF16), tile2(g_mq[l]), k_mem, v_mem, w_mo[l].astype(BF16),
            row(g_moe_norm[l]), rw_hi, rw_lo, rb, batch, seq)

        cnt = counts[0].astype(jnp.int32)
        padded = (cnt + MOE_BLK - 1) // MOE_BLK * MOE_BLK
        pad_end = jnp.cumsum(padded)
        pad_start = (pad_end - padded).astype(F32).reshape(1, LANES)
        p_rows = (n * TOP_K // MOE_BLK + N_EXPERTS) * MOE_BLK
        n_blk = p_rows // MOE_BLK
        blk_first_row = jnp.arange(n_blk, dtype=jnp.int32) * MOE_BLK
        blk_e = jnp.minimum(jnp.sum(pad_end[None, :N_EXPERTS] <= blk_first_row[:, None], axis=1),
                            N_EXPERTS - 1).astype(jnp.int32)
        n_used = (pad_end[N_EXPERTS - 1] // MOE_BLK).astype(jnp.int32).reshape(1)
        tm_pos = 256
        tri = (np.arange(tm_pos)[None, :] < np.arange(tm_pos)[:, None]).astype(np.float32)
        pos = _moe_pos(topi, pad_start, jnp.asarray(tri, dtype=BF16))
        pos_flat = pos[:, :TOP_K].reshape(n * TOP_K)
        xs = _moe_scatter(pos_flat, h2, jnp.zeros((p_rows * ROW_TILE, LANES), F32))
        src = np.arange(GU_GROUP)
        perm = (np.arange(GU_GROUP)[:, None] == np.where(src < LANES, 2 * src, 2 * (src - LANES) + 1)[None, :])
        bgu = b_gate_up[l]
        ys = _moe_ffn(blk_e, n_used, xs, w_gate_up[l], jnp.asarray(perm.astype(np.float32), dtype=BF16),
                      bgu[:, None, 0::2], bgu[:, None, 1::2], w_down[l], b_down[l][:, None, :])
        x2d = _moe_combine(pos_flat, gates_e, x2, ys)
    return x2d.reshape(batch, seq, d)
```

```python
import functools
import math

import numpy as np
import jax
import jax.numpy as jnp
from jax import lax
from jax.experimental import pallas as pl
from jax.experimental.pallas import tpu as pltpu

F32 = jnp.float32
BF16 = jnp.bfloat16

EPS = 1e-6
NEG_INF = -1e30
LANES = 128

MLA_HEADS = 8
MLA_NOPE = 64
MLA_ROPE = 32
MLA_V = 64
MLA_Q_RANK = 256
MLA_KV_RANK = 128
ROPE_THETA = 10000.0

NSA_HEADS = 8
NSA_KV_GROUPS = 2
NSA_HPG = NSA_HEADS // NSA_KV_GROUPS
NSA_HD = 64
CMP_LEN = 32
CMP_STRIDE = 16
CMP_HIDDEN = 128
SLC_LEN = 64
SLC_TOPK = 16
WIN = 512
FORCE_SCORE = 1e9
NSA_TQ = 256
NSA_TQS = 512
NSA_TK = 512
CMP_WINDOW_BACK = 40

REL_BUCKETS = 32
REL_MAX_EXACT = 16
REL_MAX_DIST = 512

MEM_HEADS = 4
MEM_HD = 64

N_EXPERTS = 32
TOP_K = 4
SWIGLU_LIMIT = 7.0
SWIGLU_ALPHA = 1.702
MOE_BLK = 512
ROW_TILE = 8

LOG2E = math.log2(math.e)
MASK_BIG = 2.0 ** 100

VMEM_LIMIT = 56 * 1024 * 1024


def _cparams(sem, vmem=VMEM_LIMIT):
    return pltpu.CompilerParams(dimension_semantics=sem, vmem_limit_bytes=vmem)


def _rms(x, g):
    return x * lax.rsqrt(jnp.mean(x * x, axis=-1, keepdims=True) + EPS) * g


def _seg_rms64(t, g2):
    lane = lax.broadcasted_iota(jnp.int32, t.shape, 1)
    sq = t * t
    lo = jnp.sum(jnp.where(lane < 64, sq, 0.0), axis=-1, keepdims=True)
    hi = jnp.sum(jnp.where(lane >= 64, sq, 0.0), axis=-1, keepdims=True)
    ms = jnp.where(lane < 64, lo, hi) * (1.0 / 64.0)
    return t * lax.rsqrt(ms + EPS) * g2


def _dot(a, b):
    return jnp.dot(a, b, preferred_element_type=F32)


def _with_ones(v):
    lane = lax.broadcasted_iota(jnp.int32, v.shape, 1)
    return jnp.concatenate([v, jnp.where(lane == 0, 1.0, 0.0).astype(v.dtype)], axis=1)


def _dot_nt(a, b):
    return lax.dot_general(a, b, (((1,), (1,)), ((), ())), preferred_element_type=F32)


def _softmax_weights(s, m_old):
    cols = [s[:, c * LANES:(c + 1) * LANES] for c in range(s.shape[1] // LANES)]
    m_new = jnp.maximum(m_old, jnp.max(functools.reduce(jnp.maximum, cols), axis=-1, keepdims=True))
    return m_new, jnp.concatenate([jnp.exp2((c - m_new).astype(BF16)) for c in cols], axis=1)


IN_COLS_PAD = 1792


def _in_proj_kernel(x_ref, g_ref, w_ref, mla_ref, qn_ref, kv_ref, misc_ref):
    h = _rms(x_ref[...], g_ref[...])
    p = _dot(h.astype(BF16), w_ref[...])
    mla_ref[...] = p[:, 0:384]
    qn_ref[...] = p[:, 384:896]
    for j in range(6):
        kv_ref[j] = p[:, 896 + 128 * j:1024 + 128 * j]
    misc_ref[...] = p[:, 1664:1792]


def _in_proj(x2d, g, w):
    n, d = x2d.shape
    tm = 512
    return pl.pallas_call(
        _in_proj_kernel,
        grid=(n // tm,),
        in_specs=[pl.BlockSpec((tm, d), lambda i: (i, 0)),
                  pl.BlockSpec((1, d), lambda i: (0, 0)),
                  pl.BlockSpec((d, IN_COLS_PAD), lambda i: (0, 0))],
        out_specs=[pl.BlockSpec((tm, 384), lambda i: (i, 0)),
                   pl.BlockSpec((tm, 512), lambda i: (i, 0)),
                   pl.BlockSpec((6, tm, 128), lambda i: (0, i, 0)),
                   pl.BlockSpec((tm, 128), lambda i: (i, 0))],
        out_shape=[jax.ShapeDtypeStruct((n, 384), F32),
                   jax.ShapeDtypeStruct((n, 512), F32),
                   jax.ShapeDtypeStruct((6, n, 128), F32),
                   jax.ShapeDtypeStruct((n, 128), F32)],
        compiler_params=_cparams(("parallel",)),
        name="in_proj",
    )(x2d, g, w)


def _mla_prep_kernel(mla_ref, misc_ref, gcq_ref, wuq_ref, wuqs_ref, gckv_ref, wuk_ref, wuv_ref,
                     cq0_ref, cq1_ref, ck0_ref, ck1_ref, q_ref, k_ref, v_ref):
    p = mla_ref[...]
    cqn = _rms(p[:, 0:MLA_Q_RANK], gcq_ref[...]).astype(BF16)
    ckvn = _rms(p[:, MLA_Q_RANK:MLA_Q_RANK + MLA_KV_RANK], gckv_ref[...]).astype(BF16)
    qall = _dot(cqn, wuq_ref[...])
    qswp = _dot(cqn, wuqs_ref[...])
    kall = _dot(ckvn, wuk_ref[...])
    vall = _dot(ckvn, wuv_ref[...])
    misc = misc_ref[...]
    lane = lax.broadcasted_iota(jnp.int32, misc.shape, 1)
    half = MLA_ROPE // 2
    in_rope = (lane >= MLA_NOPE) & (lane < MLA_NOPE + MLA_ROPE)
    krope = jnp.where(in_rope, pltpu.roll(misc, MLA_NOPE, axis=1), 0.0)
    kswp = jnp.where(lane < MLA_NOPE + half, pltpu.roll(misc, MLA_NOPE - half, axis=1),
                     pltpu.roll(misc, MLA_NOPE + half, axis=1))
    cq0, cq1, ck0 = cq0_ref[...], cq1_ref[...], ck0_ref[...]
    kswp_term = jnp.where(in_rope, kswp, 0.0) * ck1_ref[...]
    ones_col = jnp.where(lane == MLA_V, 1.0, 0.0)
    inv_dk = 1.0 / (MLA_NOPE + MLA_ROPE)

    def inv_rms(t):
        return lax.rsqrt(jnp.sum(t * t, axis=-1, keepdims=True) * inv_dk + EPS)

    for h in range(MLA_HEADS):
        hs = slice(h * LANES, (h + 1) * LANES)
        tq = qall[:, hs]
        q_ref[h] = (inv_rms(tq) * (tq * cq0 + qswp[:, hs] * cq1)).astype(BF16)
        tk = kall[:, hs] + krope
        k_ref[h] = (inv_rms(tk) * (tk * ck0 + kswp_term)).astype(BF16)
        v_ref[h] = (vall[:, hs] + ones_col).astype(BF16)


def _mla_prep(mla, misc, gcq, wuq, wuqs, gckv, wuk, wuv, cq0, cq1, ck0, ck1, seq):
    n = mla.shape[0]
    tm = 512
    ns = seq // tm
    full = lambda a: pl.BlockSpec(a.shape, lambda i: (0,) * a.ndim)
    rope_spec = pl.BlockSpec((tm, LANES), lambda i: (i % ns, 0))
    head_spec = pl.BlockSpec((MLA_HEADS, tm, LANES), lambda i: (0, i, 0))
    head_sds = jax.ShapeDtypeStruct((MLA_HEADS, n, LANES), BF16)
    return pl.pallas_call(
        _mla_prep_kernel,
        grid=(n // tm,),
        in_specs=[pl.BlockSpec((tm, 384), lambda i: (i, 0)),
                  pl.BlockSpec((tm, LANES), lambda i: (i, 0)),
                  full(gcq), full(wuq), full(wuqs), full(gckv), full(wuk), full(wuv),
                  rope_spec, rope_spec, rope_spec, rope_spec],
        out_specs=[head_spec, head_spec, head_spec],
        out_shape=[head_sds, head_sds, head_sds],
        compiler_params=_cparams(("parallel",)),
        name="mla_prep",
    )(mla, misc, gcq, wuq, wuqs, gckv, wuk, wuv, cq0, cq1, ck0, ck1)


def _mla_attn_kernel(q_ref, k_ref, v_ref, o_ref, m_ref, acc_ref):
    qi, ki = pl.program_id(1), pl.program_id(2)

    @pl.when(ki == 0)
    def _():
        m_ref[...] = jnp.full(m_ref.shape, NEG_INF, F32)
        acc_ref[...] = jnp.zeros(acc_ref.shape, F32)

    def step(masked):
        def head(h, carry):
            s = _dot_nt(q_ref[h], k_ref[h])
            if masked:
                row = lax.broadcasted_iota(jnp.int32, s.shape, 0)
                col = lax.broadcasted_iota(jnp.int32, s.shape, 1)
                s = jnp.where(col <= row, s, -jnp.inf)
            m_old = m_ref[h]
            m_new, p = _softmax_weights(s, m_old)
            acc_ref[h] = jnp.exp2(m_old - m_new) * acc_ref[h] + _dot(p, v_ref[h])
            m_ref[h] = m_new
            return carry
        lax.fori_loop(0, MLA_HEADS, head, 0, unroll=True)

    @pl.when(ki < qi)
    def _():
        step(False)

    @pl.when(ki == qi)
    def _():
        step(True)

    @pl.when(ki == pl.num_programs(2) - 1)
    def _():
        for h in range(MLA_HEADS):
            acc = acc_ref[h]
            o_ref[:, h * MLA_V:(h + 1) * MLA_V] = acc[:, :MLA_V] / acc[:, MLA_V:MLA_V + 1]


def _mla_attn(q, k, v, batch, seq):
    n = q.shape[1]
    tq = tk = 512
    nq = seq // tq
    kv_spec = pl.BlockSpec((MLA_HEADS, tk, LANES), lambda b, i, j: (0, b * nq + jnp.minimum(i, j), 0))
    return pl.pallas_call(
        _mla_attn_kernel,
        grid=(batch, nq, nq),
        in_specs=[pl.BlockSpec((MLA_HEADS, tq, LANES), lambda b, i, j: (0, b * nq + i, 0)), kv_spec, kv_spec],
        out_specs=pl.BlockSpec((tq, MLA_HEADS * MLA_V), lambda b, i, j: (b * nq + i, 0)),
        out_shape=jax.ShapeDtypeStruct((n, MLA_HEADS * MLA_V), F32),
        scratch_shapes=[pltpu.VMEM((MLA_HEADS, tq, LANES), F32),
                        pltpu.VMEM((MLA_HEADS, tq, LANES), F32)],
        compiler_params=_cparams(("parallel", "parallel", "arbitrary")),
        name="mla_attn",
    )(q, k, v)


def _nsa_cmp_kernel(kc_ref, vc_ref, pak_ref, pbk_ref, wak_ref, wbk_ref, w2k_ref,
                    pav_ref, pbv_ref, wav_ref, wbv_ref, w2v_ref, gk_ref, kout_ref, vout_ref):
    def compress(chunks, pa, pb, wa, wb, w2):
        nc = chunks.shape[0]
        ha = _dot((chunks + pa).astype(BF16), wa)
        hb = _dot((chunks + pb).astype(BF16), wb)
        hid = jax.nn.gelu(ha + pltpu.roll(hb, nc - 1, axis=0))
        return [_dot(hid[:, g * CMP_HIDDEN:(g + 1) * CMP_HIDDEN].astype(BF16), w2)
                for g in range(NSA_KV_GROUPS)]

    kc = compress(kc_ref[0], pak_ref[...], pbk_ref[...], wak_ref[...], wbk_ref[...], w2k_ref[...])
    vc = compress(vc_ref[0], pav_ref[...], pbv_ref[...], wav_ref[...], wbv_ref[...], w2v_ref[...])
    for g in range(NSA_KV_GROUPS):
        kn = _rms(kc[g], gk_ref[...])
        kout_ref[0, g] = jnp.concatenate([kn, jnp.zeros_like(kn)], axis=1).astype(BF16)
        vout_ref[0, g] = vc[g].astype(BF16)


def _nsa_cmp(kc_chunks, vc_chunks, wk, wv, gk):
    b, nc, width = kc_chunks.shape
    full = lambda a: pl.BlockSpec(a.shape, lambda i: (0,) * a.ndim)
    chunk_spec = pl.BlockSpec((1, nc, width), lambda i: (i, 0, 0))
    out_spec = lambda width: pl.BlockSpec((1, NSA_KV_GROUPS, nc, width), lambda i: (i, 0, 0, 0))
    out_sds = lambda width: jax.ShapeDtypeStruct((b, NSA_KV_GROUPS, nc, width), BF16)
    return pl.pallas_call(
        _nsa_cmp_kernel,
        grid=(b,),
        in_specs=[chunk_spec, chunk_spec] + [full(a) for a in wk] + [full(a) for a in wv] + [full(gk)],
        out_specs=[out_spec(LANES), out_spec(NSA_HD)],
        out_shape=[out_sds(LANES), out_sds(NSA_HD)],
        compiler_params=_cparams(("parallel",)),
        name="nsa_cmp",
    )(kc_chunks, vc_chunks, *wk, *wv, gk)


def _nsa_prep_kernel(qn_ref, ks_ref, vs_ref, kw_ref, vw_ref, misc_ref, gq_ref, gk_ref,
                     q_ref, kso_ref, vso_ref, kwo_ref, vwo_ref, gate_ref, *, seq):
    gq2, gk2 = gq_ref[...], gk_ref[...]
    tm = qn_ref.shape[0]
    scale = NSA_HD ** -0.5 * LOG2E
    zpad = jnp.zeros((tm, NSA_HD), F32)
    for c in range(NSA_HEADS // 2):
        t = _seg_rms64(qn_ref[:, c * LANES:(c + 1) * LANES], gq2) * scale
        q_ref[2 * c] = jnp.concatenate([t[:, :NSA_HD], zpad], axis=1).astype(BF16)
        q_ref[2 * c + 1] = jnp.concatenate([t[:, NSA_HD:], zpad], axis=1).astype(BF16)
    ksn = _seg_rms64(ks_ref[0], gk2)
    kwn = _seg_rms64(kw_ref[0], gk2)
    vs, vw = vs_ref[0], vw_ref[0]
    row = lax.broadcasted_iota(jnp.int32, (tm, LANES), 0)
    lane = lax.broadcasted_iota(jnp.int32, (tm, LANES), 1)
    pos = (pl.program_id(0) * tm) % seq + row
    neg_onehot = jnp.where(lane == pos // SLC_LEN, -1.0, 0.0)
    for g in range(NSA_KV_GROUPS):
        sl = slice(g * NSA_HD, (g + 1) * NSA_HD)
        kso_ref[g] = jnp.concatenate([neg_onehot, ksn[:, sl], zpad], axis=1).astype(BF16)
        kwo_ref[g] = jnp.concatenate([kwn[:, sl], zpad], axis=1).astype(BF16)
        vso_ref[g] = _with_ones(vs[:, sl]).astype(BF16)
        vwo_ref[g] = _with_ones(vw[:, sl]).astype(BF16)
    sig = jax.nn.sigmoid(misc_ref[...])
    per_group = 3 * NSA_HPG
    for g in range(NSA_KV_GROUPS):
        gate_ref[g] = pltpu.roll(sig, LANES - (MLA_ROPE + per_group * g), axis=1)


def _nsa_prep(qn, kv, misc, gq2, gk2, seq):
    n = qn.shape[0]
    tm = 512
    full = lambda a: pl.BlockSpec(a.shape, lambda i: (0,) * a.ndim)
    kv_spec = lambda j: pl.BlockSpec((1, tm, LANES), lambda i, j=j: (j, i, 0))
    g_spec = lambda width: pl.BlockSpec((NSA_KV_GROUPS, tm, width), lambda i: (0, i, 0))
    g_sds = lambda width: jax.ShapeDtypeStruct((NSA_KV_GROUPS, n, width), BF16)
    return pl.pallas_call(
        functools.partial(_nsa_prep_kernel, seq=seq),
        grid=(n // tm,),
        in_specs=[pl.BlockSpec((tm, 512), lambda i: (i, 0)),
                  kv_spec(2), kv_spec(3), kv_spec(4), kv_spec(5),
                  pl.BlockSpec((tm, LANES), lambda i: (i, 0)), full(gq2), full(gk2)],
        out_specs=[pl.BlockSpec((NSA_HEADS, tm, LANES), lambda i: (0, i, 0)),
                   g_spec(2 * LANES), g_spec(LANES), g_spec(LANES), g_spec(LANES),
                   pl.BlockSpec((NSA_KV_GROUPS, tm, LANES), lambda i: (0, i, 0))],
        out_shape=[jax.ShapeDtypeStruct((NSA_HEADS, n, LANES), BF16),
                   g_sds(2 * LANES), g_sds(LANES), g_sds(LANES), g_sds(LANES),
                   jax.ShapeDtypeStruct((NSA_KV_GROUPS, n, LANES), F32)],
        compiler_params=_cparams(("parallel",)),
        name="nsa_prep",
    )(qn, kv, kv, kv, kv, misc, gq2, gk2)


def _nsa_select_kernel(q_ref, kc_ref, vct_ref, tct_ref, ovlt_ref, ocmpt_ref, sel_ref, s_ref, *, nc, n_sel):
    tq, hpg = NSA_TQS, NSA_HPG
    i = pl.program_id(2)
    q0 = i * tq
    n_forced = 3
    n_live = jnp.minimum((q0 + tq - CMP_LEN) // (CMP_STRIDE * LANES) + 1, nc // LANES)
    for ncb in range(1, nc // LANES + 1):
        pl.when(n_live == ncb)(functools.partial(
            _nsa_select_body, q_ref, kc_ref, vct_ref, tct_ref, ovlt_ref, ocmpt_ref, sel_ref, s_ref,
            ncb=ncb, n_sel=n_sel, n_forced=n_forced))


def _nsa_select_body(q_ref, kc_ref, vct_ref, tct_ref, ovlt_ref, ocmpt_ref, sel_ref, s_ref, *, ncb, n_sel, n_forced):
    tq, hpg = NSA_TQS, NSA_HPG
    i = pl.program_id(2)
    q0 = i * tq
    width = ncb * LANES
    s_ref[:width, :] = _dot_nt(kc_ref[0, 0, :width, :], q_ref[...].reshape(hpg * tq, LANES))

    first_blk = ((tq // CMP_STRIDE) * i + LANES - CMP_WINDOW_BACK) // LANES - 1
    for half in range(2):
        blk = first_blk + half

        @pl.when((blk >= 0) & (blk < ncb))
        def _():
            r0 = pl.multiple_of(blk * LANES, LANES)
            for h in range(hpg):
                s_ref[pl.ds(r0, LANES), h * tq:(h + 1) * tq] += tct_ref[0, h, half * LANES:(half + 1) * LANES, :]

    blk_row = lax.broadcasted_iota(jnp.int32, (width, tq), 0)
    qpos = q0 + lax.broadcasted_iota(jnp.int32, (width, tq), 1)
    valid = (CMP_STRIDE * blk_row + (CMP_LEN - 1)) <= qpos
    vct = vct_ref[0, 0, :, :width]
    psum = jnp.zeros((width, tq), F32)
    for h in range(hpg):
        sh = jnp.where(valid, s_ref[:width, h * tq:(h + 1) * tq], -jnp.inf)
        m = jnp.maximum(jnp.max(sh, axis=0, keepdims=True), NEG_INF)
        e = jnp.exp2(sh - m)
        p = e * (1.0 / jnp.maximum(jnp.sum(e, axis=0, keepdims=True), 1e-30))
        psum = psum + p
        ocmpt_ref[h * NSA_HD:(h + 1) * NSA_HD, :] = _dot(vct, p.astype(BF16))

    hi = psum.astype(BF16)
    rest = psum - hi.astype(F32)
    mid = rest.astype(BF16)
    lo = (rest - mid.astype(F32)).astype(BF16)
    ovlt = ovlt_ref[:, :width]
    imp = _dot(ovlt, hi) + _dot(ovlt, mid) + _dot(ovlt, lo)

    jrow = lax.broadcasted_iota(jnp.int32, (LANES, tq), 0)
    qpos = q0 + lax.broadcasted_iota(jnp.int32, (LANES, tq), 1)
    jf = jrow.astype(F32)
    cur = qpos // SLC_LEN
    forced = (jrow == 0) | (jrow == cur) | (jrow == cur - 1)
    imp = jnp.where(jrow * SLC_LEN <= qpos, imp, NEG_INF)
    if n_sel >= n_forced:
        imp, rounds = jnp.where(forced, -jnp.inf, imp), n_sel - n_forced
    else:
        imp, rounds = jnp.where(forced, FORCE_SCORE, imp), n_sel
    for _ in range(rounds):
        best = jnp.max(imp, axis=0, keepdims=True)
        first = jnp.min(jnp.where(imp == best, jf, float(LANES)), axis=0, keepdims=True)
        imp = jnp.where(jf == first, -jnp.inf, imp)
    sel_ref[0] = jnp.where(imp == -jnp.inf, 0.0, MASK_BIG).T.astype(BF16)


def _nsa_select(q, kcmp, vcmp_t, tct, ovl_t, batch, seq):
    n = q.shape[1]
    nc = kcmp.shape[2]
    nq = seq // NSA_TQS
    n_res = tct.shape[0]
    n_sel = min(SLC_TOPK, seq // SLC_LEN)
    g_, hpg = NSA_KV_GROUPS, NSA_HPG
    kern = functools.partial(_nsa_select_kernel, nc=nc, n_sel=n_sel)
    cmp_spec = lambda shape: pl.BlockSpec((1, 1) + shape, lambda b, g, i: (b, g, 0, 0))
    return pl.pallas_call(
        kern,
        grid=(batch, g_, nq),
        in_specs=[pl.BlockSpec((hpg, NSA_TQS, LANES), lambda b, g, i: (g, b * nq + i, 0)),
                  cmp_spec((nc, LANES)), cmp_spec((NSA_HD, nc)),
                  pl.BlockSpec((1, hpg, 2 * LANES, NSA_TQS), lambda b, g, i: (i % n_res, g, 0, 0)),
                  pl.BlockSpec(ovl_t.shape, lambda b, g, i: (0, 0))],
        out_specs=[pl.BlockSpec((hpg * NSA_HD, NSA_TQS), lambda b, g, i: (g, b * nq + i)),
                   pl.BlockSpec((1, NSA_TQS, LANES), lambda b, g, i: (g, b * nq + i, 0))],
        out_shape=[jax.ShapeDtypeStruct((NSA_HEADS * NSA_HD, n), F32),
                   jax.ShapeDtypeStruct((g_, n, LANES), BF16)],
        scratch_shapes=[pltpu.VMEM((nc, hpg * NSA_TQS), F32)],
        compiler_params=_cparams(("parallel", "parallel", "arbitrary")),
        name="nsa_select",
    )(q, kcmp, vcmp_t, tct, ovl_t)


def _nsa_attn_kernel(q_ref, ks_ref, vs_ref, kw_ref, vw_ref, sel_ref, tn_ref, gate_ref, ocmpt_ref,
                     o_ref, m_ref, acc_ref):
    tq, tk, hpg = NSA_TQ, NSA_TK, NSA_HPG
    rows = hpg * tq
    i = pl.program_id(2)
    q0 = i * tq
    q4 = q_ref[...].reshape(rows, LANES)
    q_aug = jnp.concatenate([jnp.concatenate([sel_ref[0]] * hpg, axis=0), q4], axis=1)
    row = lax.broadcasted_iota(jnp.int32, (tq, tk), 0)
    col = lax.broadcasted_iota(jnp.int32, (tq, tk), 1)
    m_ref[...] = jnp.full(m_ref.shape, NEG_INF, F32)
    acc_ref[...] = jnp.zeros(acc_ref.shape, F32)

    def tile(kt, near):
        k0 = pl.multiple_of(kt * tk, tk)
        s = _dot_nt(q_aug, ks_ref[0, 0, pl.ds(k0, tk), :])
        if near:
            causal = (k0 + col) <= (q0 + row)
            start = pl.multiple_of(k0 + tk + WIN - q0, LANES)
        ps, alphas = [], []
        for h in range(hpg):
            rs = slice(h * tq, (h + 1) * tq)
            sh = s[rs]
            if near:
                sh = jnp.where(causal, sh + tn_ref[h, :, pl.ds(start, tk)], -jnp.inf)
            m_old = m_ref[rs]
            m_new, p = _softmax_weights(sh, m_old)
            m_ref[rs] = m_new
            alphas.append(jnp.exp2(m_old - m_new))
            ps.append(p)
        acc_ref[...] = (jnp.concatenate(alphas, axis=0) * acc_ref[...]
                        + _dot(jnp.concatenate(ps, axis=0), vs_ref[0, 0, pl.ds(k0, tk), :]))

    last = q0 // tk
    n_far = jnp.maximum(last - 1, 0)
    def far_pair(pair, c):
        tile(2 * pair, False)
        tile(2 * pair + 1, False)
        return c

    lax.fori_loop(0, n_far // 2, far_pair, 0)
    lax.fori_loop(n_far // 2 * 2, n_far, lambda kt, c: (tile(kt, False), c)[1], 0)
    lax.fori_loop(n_far, last + 1, lambda kt, c: (tile(kt, True), c)[1], 0)

    wk = WIN + tq
    w0 = pl.multiple_of(q0, tq)
    s = _dot_nt(q4, kw_ref[0, 0, pl.ds(w0, wk), :])
    roww = lax.broadcasted_iota(jnp.int32, (tq, wk), 0)
    colw = lax.broadcasted_iota(jnp.int32, (tq, wk), 1)
    dist = roww + WIN - colw
    validw = (dist >= 0) & (dist < WIN) & (q0 - WIN + colw >= 0)
    ps = []
    for h in range(hpg):
        sh = s[h * tq:(h + 1) * tq] + tn_ref[h, :, tk:tk + wk]
        ps.append(_softmax_weights(jnp.where(validw, sh, -jnp.inf), NEG_INF)[1])
    accw = _dot(jnp.concatenate(ps, axis=0), vw_ref[0, 0, pl.ds(w0, wk), :])

    gates = gate_ref[0]
    acc = acc_ref[...]
    outs = []
    for h in range(hpg):
        rs = slice(h * tq, (h + 1) * tq)
        o_cmp = ocmpt_ref[h * NSA_HD:(h + 1) * NSA_HD, :].T
        o_slc = acc[rs, :NSA_HD] / jnp.maximum(acc[rs, NSA_HD:NSA_HD + 1], 1e-30)
        o_win = accw[rs, :NSA_HD] / accw[rs, NSA_HD:NSA_HD + 1]
        outs.append(gates[:, 3 * h:3 * h + 1] * o_cmp + gates[:, 3 * h + 1:3 * h + 2] * o_slc
                    + gates[:, 3 * h + 2:3 * h + 3] * o_win)
    o_ref[...] = jnp.concatenate(outs, axis=1)


def _nsa_attn(q, ks, vs, kwp, vwp, sel, tn, gates, ocmp_t, batch, seq):
    n = q.shape[1]
    nq = seq // NSA_TQ
    g_, hpg = NSA_KV_GROUPS, NSA_HPG
    seq_spec = lambda length, width: pl.BlockSpec((1, 1, length, width), lambda b, g, i: (g, b, 0, 0))
    tok_spec = pl.BlockSpec((1, NSA_TQ, LANES), lambda b, g, i: (g, b * nq + i, 0))
    out_spec = pl.BlockSpec((NSA_TQ, hpg * NSA_HD), lambda b, g, i: (b * nq + i, g))
    return pl.pallas_call(
        _nsa_attn_kernel,
        grid=(batch, g_, nq),
        in_specs=[pl.BlockSpec((hpg, NSA_TQ, LANES), lambda b, g, i: (g, b * nq + i, 0)),
                  seq_spec(seq, 2 * LANES), seq_spec(seq, LANES), seq_spec(seq + WIN, LANES), seq_spec(seq + WIN, LANES),
                  tok_spec,
                  pl.BlockSpec((hpg, NSA_TQ, 2 * NSA_TK + WIN), lambda b, g, i: (g, 0, 0)),
                  tok_spec, pl.BlockSpec((hpg * NSA_HD, NSA_TQ), lambda b, g, i: (g, b * nq + i))],
        out_specs=out_spec,
        out_shape=jax.ShapeDtypeStruct((n, NSA_HEADS * NSA_HD), F32),
        scratch_shapes=[pltpu.VMEM((hpg * NSA_TQ, LANES), F32), pltpu.VMEM((hpg * NSA_TQ, LANES), F32)],
        compiler_params=_cparams(("parallel", "parallel", "arbitrary")),
        name="nsa_attn",
    )(q, ks, vs, kwp, vwp, sel, tn, gates, ocmp_t)


def _mem_kv_kernel(mem_ref, g_ref, w_ref, gk_ref, k_ref, v_ref):
    mn = _rms(mem_ref[0], g_ref[...]).astype(BF16)
    kv = _dot(mn, w_ref[...])
    width = MEM_HEADS * MEM_HD
    for c in range(width // LANES):
        kn = _seg_rms64(kv[:, c * LANES:(c + 1) * LANES], gk_ref[...])
        vv = kv[:, width + c * LANES:width + (c + 1) * LANES]
        for half in range(2):
            sl = slice(half * MEM_HD, (half + 1) * MEM_HD)
            k_ref[0, 2 * c + half] = kn[:, sl].astype(BF16)
            v_ref[0, 2 * c + half] = vv[:, sl].astype(BF16)


def _mem_kv(mem, g, w, gk2):
    b, m, d = mem.shape
    full = lambda a: pl.BlockSpec(a.shape, lambda i: (0,) * a.ndim)
    spec = pl.BlockSpec((1, MEM_HEADS, m, MEM_HD), lambda i: (i, 0, 0, 0))
    sds = jax.ShapeDtypeStruct((b, MEM_HEADS, m, MEM_HD), BF16)
    return pl.pallas_call(
        _mem_kv_kernel,
        grid=(b,),
        in_specs=[pl.BlockSpec((1, m, d), lambda i: (i, 0, 0)), full(g), full(w), full(gk2)],
        out_specs=[spec, spec],
        out_shape=[sds, sds],
        compiler_params=_cparams(("parallel",)),
        name="mem_kv",
    )(mem, g, w, gk2)


def _post_attn_kernel(x_ref, ya_ref, yb_ref, goa_ref, gob_ref, woa_ref, wob_ref, gmn_ref, wmq_ref, gmq_ref,
                      km_ref, vm_ref, wmo_ref, gmoe_ref, rwh_ref, rwl_ref, rb_ref,
                      x2_ref, h2_ref, topi_ref, gate_ref, cnt_ref):
    first = (pl.program_id(0) == 0) & (pl.program_id(1) == 0)

    @pl.when(first)
    def _():
        cnt_ref[...] = jnp.zeros(cnt_ref.shape, F32)

    mixa = _rms(ya_ref[...], goa_ref[...]).astype(BF16)
    mixb = _rms(yb_ref[...], gob_ref[...]).astype(BF16)
    x1 = x_ref[...] + _dot(mixa, woa_ref[...]) + _dot(mixb, wob_ref[...])

    h = _rms(x1, gmn_ref[...]).astype(BF16)
    q = _dot(h, wmq_ref[...])
    scale = MEM_HD ** -0.5
    outs = []
    for c in range(MEM_HEADS * MEM_HD // LANES):
        qn = _seg_rms64(q[:, c * LANES:(c + 1) * LANES], gmq_ref[...]) * scale
        for half in range(2):
            hd = 2 * c + half
            qh = qn[:, half * MEM_HD:(half + 1) * MEM_HD].astype(BF16)
            s = _dot_nt(qh, km_ref[0, hd])
            e = jnp.exp(s - jnp.max(s, axis=-1, keepdims=True))
            p = e / jnp.sum(e, axis=-1, keepdims=True)
            outs.append(_dot(p.astype(BF16), vm_ref[0, hd]))
    o = jnp.concatenate(outs, axis=1).astype(BF16)
    x2 = x1 + _dot(o, wmo_ref[...])
    x2_ref[...] = x2

    h2 = _rms(x2, gmoe_ref[...])
    for c in range(ROW_TILE):
        h2_ref[pl.ds(c, h2.shape[0], stride=ROW_TILE), :] = h2[:, c * LANES:(c + 1) * LANES]
    h_hi = h2.astype(BF16)
    h_lo = (h2 - h_hi.astype(F32)).astype(BF16)
    logits = _dot(h_hi, rwh_ref[...]) + _dot(h_hi, rwl_ref[...]) + _dot(h_lo, rwh_ref[...]) + rb_ref[...]
    lane = lax.broadcasted_iota(jnp.int32, logits.shape, 1)
    lane_f = lane.astype(F32)
    topi = jnp.zeros(logits.shape, jnp.int32)
    topv = jnp.full(logits.shape, NEG_INF, F32)
    onehot = jnp.zeros(logits.shape, F32)
    for k in range(TOP_K):
        best = jnp.max(logits, axis=-1, keepdims=True)
        first_idx = jnp.min(jnp.where(logits == best, lane_f, float(LANES)), axis=-1, keepdims=True)
        hit = lane_f == first_idx
        topi = jnp.where(lane == k, first_idx.astype(jnp.int32), topi)
        topv = jnp.where(lane == k, best, topv)
        onehot = jnp.where(hit, 1.0, onehot)
        logits = jnp.where(hit, -jnp.inf, logits)
    e = jnp.where(lane < TOP_K, jnp.exp(topv - jnp.max(topv, axis=-1, keepdims=True)), 0.0)
    gate_ref[...] = e / jnp.sum(e, axis=-1, keepdims=True)
    topi_ref[...] = topi
    cnt_ref[...] += jnp.sum(onehot, axis=0, keepdims=True)


def _post_attn(x2d, ya, yb, goa, gob, woa, wob, gmn, wmq, gmq2, km, vm, wmo, gmoe, rwh, rwl, rb, batch, seq):
    n, d = x2d.shape
    tm = 512
    ns = seq // tm
    full = lambda a: pl.BlockSpec(a.shape, lambda b, i: (0,) * a.ndim)
    tok = lambda width: pl.BlockSpec((tm, width), lambda b, i: (b * ns + i, 0))
    mem_spec = pl.BlockSpec((1,) + km.shape[1:], lambda b, i: (b, 0, 0, 0))
    return pl.pallas_call(
        _post_attn_kernel,
        grid=(batch, ns),
        in_specs=[tok(d), tok(ya.shape[1]), tok(yb.shape[1]), full(goa), full(gob), full(woa), full(wob),
                  full(gmn), full(wmq), full(gmq2), mem_spec, mem_spec, full(wmo), full(gmoe),
                  full(rwh), full(rwl), full(rb)],
        out_specs=[tok(d), pl.BlockSpec((tm * ROW_TILE, LANES), lambda b, i: (b * ns + i, 0)),
                   tok(LANES), tok(LANES), pl.BlockSpec((1, LANES), lambda b, i: (0, 0))],
        out_shape=[jax.ShapeDtypeStruct((n, d), F32),
                   jax.ShapeDtypeStruct((n * ROW_TILE, LANES), F32),
                   jax.ShapeDtypeStruct((n, LANES), jnp.int32),
                   jax.ShapeDtypeStruct((n, LANES), F32),
                   jax.ShapeDtypeStruct((1, LANES), F32)],
        compiler_params=_cparams(("arbitrary", "arbitrary")),
        name="post_attn",
    )(x2d, ya, yb, goa, gob, woa, wob, gmn, wmq, gmq2, km, vm, wmo, gmoe, rwh, rwl, rb)


def _moe_pos_kernel(topi_ref, start_ref, tri_ref, pos_ref, carry_ref):
    @pl.when(pl.program_id(0) == 0)
    def _():
        carry_ref[...] = jnp.zeros(carry_ref.shape, F32)

    topi = topi_ref[...]
    lane = lax.broadcasted_iota(jnp.int32, topi.shape, 1)
    hits = [lane == topi[:, k:k + 1] for k in range(TOP_K)]
    onehot = sum(h.astype(F32) for h in hits)
    before = _dot(tri_ref[...], onehot.astype(BF16))
    base = start_ref[...] + carry_ref[...] + before
    pos = jnp.zeros(topi.shape, jnp.int32)
    for k in range(TOP_K):
        pk = jnp.sum(jnp.where(hits[k], base, 0.0), axis=-1, keepdims=True).astype(jnp.int32)
        pos = jnp.where(lane == k, pk, pos)
    pos_ref[...] = pos
    carry_ref[...] += jnp.sum(onehot, axis=0, keepdims=True)


def _moe_pos(topi, pad_start, tri):
    n = topi.shape[0]
    tm = tri.shape[0]
    return pl.pallas_call(
        _moe_pos_kernel,
        grid=(n // tm,),
        in_specs=[pl.BlockSpec((tm, LANES), lambda i: (i, 0)),
                  pl.BlockSpec((1, LANES), lambda i: (0, 0)),
                  pl.BlockSpec((tm, tm), lambda i: (0, 0))],
        out_specs=pl.BlockSpec((tm, LANES), lambda i: (i, 0)),
        out_shape=jax.ShapeDtypeStruct((n, LANES), jnp.int32),
        scratch_shapes=[pltpu.VMEM((1, LANES), F32)],
        compiler_params=_cparams(("arbitrary",)),
        name="moe_pos",
    )(topi, pad_start, tri)


def _moe_scatter_kernel(pos_ref, h_ref, zero_ref, xs_ref, sem, *, tm):
    del zero_ref

    def copy(t, k):
        dst = pl.multiple_of(pos_ref[t * TOP_K + k] * ROW_TILE, ROW_TILE)
        return pltpu.make_async_copy(h_ref.at[pl.ds(pl.multiple_of(t * ROW_TILE, ROW_TILE), ROW_TILE)],
                                     xs_ref.at[pl.ds(dst, ROW_TILE)], sem)

    def issue(t, c):
        for k in range(TOP_K):
            copy(t, k).start(priority=k % 2)
        return c

    def drain(t, c):
        for k in range(TOP_K):
            copy(t, k).wait()
        return c

    lax.fori_loop(0, tm, issue, 0, unroll=4)
    lax.fori_loop(0, tm, drain, 0, unroll=4)


def _moe_scatter(pos_flat, h2, xs_zero):
    n = h2.shape[0] // ROW_TILE
    tm = 512
    kern = functools.partial(_moe_scatter_kernel, tm=tm)
    return pl.pallas_call(
        kern,
        grid=(n // tm,),
        in_specs=[pl.BlockSpec((tm * TOP_K,), lambda i: (i,), memory_space=pltpu.SMEM),
                  pl.BlockSpec((tm * ROW_TILE, LANES), lambda i: (i, 0)),
                  pl.BlockSpec(memory_space=pl.ANY)],
        out_specs=pl.BlockSpec(memory_space=pl.ANY),
        out_shape=jax.ShapeDtypeStruct(xs_zero.shape, xs_zero.dtype),
        scratch_shapes=[pltpu.SemaphoreType.DMA(())],
        input_output_aliases={2: 0},
        compiler_params=_cparams(("arbitrary",)),
        name="moe_scatter",
    )(pos_flat, h2, xs_zero)


GU_GROUP = 2 * LANES


def _moe_ffn_kernel(blk_e_ref, n_used_ref, x_ref, wgu_ref, perm_ref, bg_ref, bu_ref, wd_ref, bd_ref, y_ref,
                    wgu_bf_ref, wd_bf_ref):
    i = pl.program_id(0)
    chunk = lambda c: pl.ds(c, MOE_BLK, stride=ROW_TILE)
    new_expert = (i == 0) | (blk_e_ref[i] != blk_e_ref[jnp.maximum(i - 1, 0)])

    @pl.when(new_expert & (i < n_used_ref[0]))
    def _():
        for c in range(wgu_ref.shape[2] // GU_GROUP):
            sl = slice(c * GU_GROUP, (c + 1) * GU_GROUP)
            wgu_bf_ref[:, sl] = _dot(wgu_ref[0, :, sl].astype(BF16), perm_ref[...]).astype(BF16)
        wd_bf_ref[...] = wd_ref[0].astype(BF16)

    @pl.when(i < n_used_ref[0])
    def _():
        x = jnp.concatenate([x_ref[chunk(c), :] for c in range(ROW_TILE)], axis=1).astype(BF16)
        gu = _dot(x, wgu_bf_ref[...])
        bg, bu = bg_ref[0], bu_ref[0]
        acts = []
        for c in range(gu.shape[1] // GU_GROUP):
            fs = slice(c * LANES, (c + 1) * LANES)
            gate = jnp.minimum(gu[:, c * GU_GROUP:c * GU_GROUP + LANES] + bg[:, fs], SWIGLU_LIMIT)
            up = jnp.clip(gu[:, c * GU_GROUP + LANES:(c + 1) * GU_GROUP] + bu[:, fs], -SWIGLU_LIMIT, SWIGLU_LIMIT)
            acts.append(((up + 1.0) * gate * jax.nn.sigmoid(SWIGLU_ALPHA * gate)).astype(BF16))
        act = jnp.concatenate(acts, axis=1)
        y = _dot(act, wd_bf_ref[...]) + bd_ref[0]
        for c in range(ROW_TILE):
            y_ref[chunk(c), :] = y[:, c * LANES:(c + 1) * LANES]

    @pl.when(i >= n_used_ref[0])
    def _():
        y_ref[...] = jnp.zeros(y_ref.shape, F32)


def _moe_ffn(blk_e, n_used, xs, wgu, perm, bg, bu, wd, bd):
    p = xs.shape[0] // ROW_TILE
    dff, d = wd.shape[1:]
    n_blk = p // MOE_BLK
    w_spec = lambda r, c: pl.BlockSpec((1, r, c), lambda i, be, nu: (be[i], 0, 0))
    row_spec = pl.BlockSpec((MOE_BLK * ROW_TILE, LANES), lambda i, be, nu: (i, 0))
    grid_spec = pltpu.PrefetchScalarGridSpec(
        num_scalar_prefetch=2,
        grid=(n_blk,),
        in_specs=[row_spec,
                  w_spec(d, 2 * dff), pl.BlockSpec(perm.shape, lambda i, be, nu: (0, 0)), w_spec(1, dff), w_spec(1, dff),
                  w_spec(dff, d), w_spec(1, d)],
        out_specs=row_spec,
        scratch_shapes=[pltpu.VMEM((d, 2 * dff), BF16), pltpu.VMEM((dff, d), BF16)],
    )
    return pl.pallas_call(
        _moe_ffn_kernel,
        grid_spec=grid_spec,
        out_shape=jax.ShapeDtypeStruct(xs.shape, F32),
        compiler_params=_cparams(("arbitrary",)),
        name="moe_ffn",
    )(blk_e, n_used, xs, wgu, perm, bg, bu, wd, bd)


def _moe_combine_kernel(pos_ref, gate_ref, x_ref, ys_ref, o_ref, buf_ref, sem, *, tm):
    def copy(t, k):
        src = pl.multiple_of(pos_ref[t * TOP_K + k] * ROW_TILE, ROW_TILE)
        return pltpu.make_async_copy(ys_ref.at[pl.ds(src, ROW_TILE)],
                                     buf_ref.at[k, pl.ds(pl.multiple_of(t * ROW_TILE, ROW_TILE), ROW_TILE)], sem)

    def issue(t, c):
        for k in range(TOP_K):
            copy(t, k).start(priority=k % 2)
        return c

    def drain(t, c):
        for k in range(TOP_K):
            copy(t, k).wait()
        return c

    lax.fori_loop(0, tm, issue, 0, unroll=4)
    lax.fori_loop(0, tm, drain, 0, unroll=4)
    gates = gate_ref[...]
    for c in range(ROW_TILE):
        cs = slice(c * LANES, (c + 1) * LANES)
        acc = x_ref[:, cs]
        for k in range(TOP_K):
            acc = acc + gates[:, k:k + 1] * buf_ref[k, pl.ds(c, tm, stride=ROW_TILE), :]
        o_ref[:, cs] = acc


def _moe_combine(pos_flat, gates, x2, ys):
    n, d = x2.shape
    tm = 256
    kern = functools.partial(_moe_combine_kernel, tm=tm)
    return pl.pallas_call(
        kern,
        grid=(n // tm,),
        in_specs=[pl.BlockSpec((tm * TOP_K,), lambda i: (i,), memory_space=pltpu.SMEM),
                  pl.BlockSpec((tm, LANES), lambda i: (i, 0)),
                  pl.BlockSpec((tm, d), lambda i: (i, 0)),
                  pl.BlockSpec(memory_space=pl.ANY)],
        out_specs=pl.BlockSpec((tm, d), lambda i: (i, 0)),
        out_shape=jax.ShapeDtypeStruct((n, d), F32),
        scratch_shapes=[pltpu.VMEM((TOP_K, tm * ROW_TILE, LANES), F32), pltpu.SemaphoreType.DMA(())],
        compiler_params=_cparams(("arbitrary",)),
        name="moe_combine",
    )(pos_flat, gates, x2, ys)


def _rel_bucket_np(dist):
    n = np.maximum(dist, 0)
    nf = np.maximum(n, 1).astype(np.float32)
    ratio = np.log(nf / np.float32(REL_MAX_EXACT)) / np.float32(math.log(REL_MAX_DIST / REL_MAX_EXACT))
    large = REL_MAX_EXACT + (ratio * np.float32(REL_BUCKETS - REL_MAX_EXACT)).astype(np.int32)
    large = np.minimum(large, REL_BUCKETS - 1)
    return np.where(n < REL_MAX_EXACT, n, large)


def _bias_tables_kernel(tbl_ref, bc_ref, bt_ref, tc_ref, tn_ref):
    h = pl.program_id(0)
    bc, bt = bc_ref[...], bt_ref[...]
    tc = jnp.zeros(bc.shape, F32)
    tn = jnp.zeros(bt.shape, F32)
    far = tbl_ref[REL_BUCKETS - 1, h]
    for b in range(REL_BUCKETS - 1):
        v = (tbl_ref[b, h] - far) * LOG2E
        tc = jnp.where(bc == b, v, tc)
        tn = jnp.where(bt == b, v, tn)
    tc_ref[:, 0] = tc
    tn_ref[0] = tn


def _bias_tables(rel_table):
    q = np.arange(NSA_TQS)[None, :, None]
    blocks_per_tile = NSA_TQS // CMP_STRIDE
    n_res = LANES // blocks_per_tile
    res = np.arange(n_res)[:, None, None]
    base_gap = (blocks_per_tile * res - CMP_WINDOW_BACK) % LANES + CMP_WINDOW_BACK
    w = np.arange(2 * LANES)[None, None, :]
    b_cmp = _rel_bucket_np(CMP_STRIDE * base_gap + q - CMP_STRIDE * w - (CMP_LEN - 1)).astype(np.int32)
    b_cmp = np.ascontiguousarray(b_cmp.transpose(0, 2, 1))
    c = np.arange(2 * NSA_TK + WIN)[None, :]
    b_tok = _rel_bucket_np(np.arange(NSA_TQ)[:, None] + WIN - (c - NSA_TK)).astype(np.int32)
    heads = rel_table.shape[1]
    return pl.pallas_call(
        _bias_tables_kernel,
        grid=(heads,),
        in_specs=[pl.BlockSpec(memory_space=pltpu.SMEM),
                  pl.BlockSpec(b_cmp.shape, lambda h: (0, 0, 0)),
                  pl.BlockSpec(b_tok.shape, lambda h: (0, 0))],
        out_specs=[pl.BlockSpec((n_res, 1) + b_cmp.shape[1:], lambda h: (0, h, 0, 0)),
                   pl.BlockSpec((1,) + b_tok.shape, lambda h: (h, 0, 0))],
        out_shape=[jax.ShapeDtypeStruct((n_res, heads) + b_cmp.shape[1:], F32),
                   jax.ShapeDtypeStruct((heads,) + b_tok.shape, F32)],
        compiler_params=_cparams(("parallel",)),
        name="bias_tables",
    )(rel_table.astype(F32), jnp.asarray(b_cmp), jnp.asarray(b_tok))


def kernel(x, mem, g_attn_norm, w_in, g_cq, w_uq, g_ckv, w_ukv, g_q_mla, g_k_mla, cmp_k_pos, cmp_k_w1, cmp_k_w2, cmp_v_pos, cmp_v_w1, cmp_v_w2, g_q_nsa, g_k_nsa, rel_table, g_out_mla, g_out_nsa, w_out, g_mem_norm, g_mem_src, w_mq, w_mkv, g_mq, g_mk, w_mo, g_moe_norm, router_w, router_b, w_gate_up, b_gate_up, w_down, b_down):
    batch, seq, d = x.shape
    n = batch * seq
    depth = w_in.shape[0]
    assert seq % 512 == 0 and seq // SLC_LEN <= LANES and d == ROW_TILE * LANES
    row = lambda v: v.reshape(1, -1).astype(F32)
    tile2 = lambda v: jnp.concatenate([v, v]).reshape(1, -1).astype(F32)

    x2d = x.reshape(n, d)
    for l in range(depth):
        wi = w_in[l]
        w_in_r = jnp.concatenate(
            [wi[:, 0:384], wi[:, 416:928], wi[:, 928:1696], wi[:, 384:416], wi[:, 1696:1720],
             jnp.zeros((d, IN_COLS_PAD - 1720), wi.dtype)], axis=1).astype(BF16)
        dq = MLA_NOPE + MLA_ROPE
        half = MLA_ROPE // 2
        x1, x2 = slice(MLA_NOPE, MLA_NOPE + half), slice(MLA_NOPE + half, dq)
        head_pad = lambda w, width: jnp.pad(w, ((0, 0), (0, 0), (0, LANES - width))).reshape(
            w.shape[0], MLA_HEADS * LANES).astype(BF16)
        wuq3 = w_uq[l].reshape(MLA_Q_RANK, MLA_HEADS, dq)
        wuq_r = head_pad(wuq3, dq)
        wuq_s = head_pad(jnp.concatenate([jnp.zeros_like(wuq3[:, :, :MLA_NOPE]), wuq3[:, :, x2], wuq3[:, :, x1]], 2), dq)
        wukv = w_ukv[l].reshape(MLA_KV_RANK, MLA_HEADS, MLA_NOPE + MLA_V)
        wuk_r = head_pad(wukv[:, :, :MLA_NOPE], MLA_NOPE)
        wuv_r = head_pad(wukv[:, :, MLA_NOPE:], MLA_V)

        inv = ROPE_THETA ** (-jnp.arange(half, dtype=F32) / half)
        ang = jnp.arange(seq, dtype=F32)[:, None] * inv
        cos, sin = jnp.cos(ang), jnp.sin(ang)
        lane_pad = jnp.zeros((seq, LANES - dq), F32)
        rot_c = jnp.concatenate([jnp.ones((seq, MLA_NOPE), F32), cos, cos, lane_pad], 1)
        rot_s = jnp.concatenate([jnp.zeros((seq, MLA_NOPE), F32), -sin, sin, lane_pad], 1)

        def rope_tables(g, scale):
            g_pad = jnp.pad(g, (0, LANES - dq))
            g_swp = jnp.pad(jnp.concatenate([g[:MLA_NOPE], g[x2], g[x1]]), (0, LANES - dq))
            return rot_c * (g_pad * scale), rot_s * (g_swp * scale)

        cq0, cq1 = rope_tables(g_q_mla[l], dq ** -0.5 * LOG2E)
        ck0, ck1 = rope_tables(g_k_mla[l], 1.0)

        mla_in, qn, kv6, misc = _in_proj(x2d, row(g_attn_norm[l]), w_in_r)
        q_m, k_m, v_m = _mla_prep(mla_in, misc, row(g_cq[l]), wuq_r, wuq_s, row(g_ckv[l]), wuk_r, wuv_r,
                                  cq0, cq1, ck0, ck1, seq)
        y_mla = _mla_attn(q_m, k_m, v_m, batch, seq)

        nc = seq // CMP_STRIDE
        half_len = CMP_LEN // 2
        eye_g = jnp.eye(NSA_KV_GROUPS, dtype=F32)

        def cmp_weights(pos, w1, w2):
            out = []
            for part in range(2):
                sl = slice(part * half_len, (part + 1) * half_len)
                out.append(jnp.broadcast_to(pos[sl][:, None, :], (half_len, NSA_KV_GROUPS, NSA_HD))
                           .reshape(1, -1))
            for part in range(2):
                sl = slice(part * half_len, (part + 1) * half_len)
                wexp = jnp.einsum('ldf,gh->lgdhf', w1[sl], eye_g)
                out.append(wexp.reshape(half_len * NSA_KV_GROUPS * NSA_HD, NSA_KV_GROUPS * CMP_HIDDEN).astype(BF16))
            out.append(w2.astype(BF16))
            return out

        chunk_w = CMP_STRIDE * NSA_KV_GROUPS * NSA_HD
        kcmp, vcmp = _nsa_cmp(kv6[0].reshape(batch, nc, chunk_w), kv6[1].reshape(batch, nc, chunk_w),
                              cmp_weights(cmp_k_pos[l], cmp_k_w1[l], cmp_k_w2[l]),
                              cmp_weights(cmp_v_pos[l], cmp_v_w1[l], cmp_v_w2[l]), row(g_k_nsa[l]))
        q_n, ks, vs, kw, vw, gates_n = _nsa_prep(qn, kv6, misc, tile2(g_q_nsa[l]), tile2(g_k_nsa[l]), seq)
        by_batch = lambda t: t.reshape(NSA_KV_GROUPS, batch, seq, t.shape[-1])
        front_pad = lambda t: jnp.pad(by_batch(t), ((0, 0), (0, 0), (WIN, 0), (0, 0)))
        tc, tn = _bias_tables(rel_table)
        n_idx = np.arange(nc)[:, None]
        j_idx = np.arange(LANES)[None, :]
        ovl = ((CMP_STRIDE * n_idx < SLC_LEN * j_idx + SLC_LEN)
               & (CMP_STRIDE * n_idx + CMP_LEN - 1 >= SLC_LEN * j_idx)
               & (n_idx < nc - 1) & (j_idx < seq // SLC_LEN)).astype(np.float32)
        o_cmp_t, sel = _nsa_select(q_n, kcmp, jnp.swapaxes(vcmp, 2, 3), tc, jnp.asarray(ovl.T, dtype=BF16),
                                   batch, seq)
        y_nsa = _nsa_attn(q_n, by_batch(ks), by_batch(vs), front_pad(kw), front_pad(vw), sel,
                          tn, gates_n, o_cmp_t, batch, seq)

        k_mem, v_mem = _mem_kv(mem, row(g_mem_src[l]), w_mkv[l].reshape(d, 2 * MEM_HEADS * MEM_HD).astype(BF16),
                               tile2(g_mk[l]))
        rw = jnp.pad(router_w[l], ((0, 0), (0, LANES - N_EXPERTS)))
        rw_hi = rw.astype(BF16)
        rw_lo = (rw - rw_hi.astype(F32)).astype(BF16)
        rb =jnp.pad(router_b[l], (0, LANES - N_EXPERTS), constant_values=NEG_INF).reshape(1, LANES)
        wo = w_out[l].astype(BF16)
        n_mla = MLA_HEADS * MLA_V
        x2, h2, topi, gates_e, counts = _post_attn(
            x2d, y_mla, y_nsa, row(g_out_mla[l]), row(g_out_nsa[l]), wo[:n_mla], wo[n_mla:],
            row(g_mem_norm[l]), w_mq[l].astype(BF16), tile2(g_mq[l]), k_mem, v_mem, w_mo[l].astype(BF16),
            row(g_moe_norm[l]), rw_hi, rw_lo, rb, batch, seq)

        cnt = counts[0].astype(jnp.int32)
        padded = (cnt + MOE_BLK - 1) // MOE_BLK * MOE_BLK
        pad_end = jnp.cumsum(padded)
        pad_start = (pad_end - padded).astype(F32).reshape(1, LANES)
        p_rows = (n * TOP_K // MOE_BLK + N_EXPERTS) * MOE_BLK
        n_blk = p_rows // MOE_BLK
        blk_first_row = jnp.arange(n_blk, dtype=jnp.int32) * MOE_BLK
        blk_e = jnp.minimum(jnp.sum(pad_end[None, :N_EXPERTS] <= blk_first_row[:, None], axis=1),
                            N_EXPERTS - 1).astype(jnp.int32)
        n_used = (pad_end[N_EXPERTS - 1] // MOE_BLK).astype(jnp.int32).reshape(1)
        tm_pos = 512
        tri = (np.arange(tm_pos)[None, :] < np.arange(tm_pos)[:, None]).astype(np.float32)
        pos = _moe_pos(topi, pad_start, jnp.asarray(tri, dtype=BF16))
        pos_flat = pos[:, :TOP_K].reshape(n * TOP_K)
        xs = _moe_scatter(pos_flat, h2, jnp.zeros((p_rows * ROW_TILE, LANES), F32))
        src = np.arange(GU_GROUP)
        perm = (np.arange(GU_GROUP)[:, None] == np.where(src < LANES, 2 * src, 2 * (src - LANES) + 1)[None, :])
        bgu = b_gate_up[l]
        ys = _moe_ffn(blk_e, n_used, xs, w_gate_up[l], jnp.asarray(perm.astype(np.float32), dtype=BF16),
                      bgu[:, None, 0::2], bgu[:, None, 1::2], w_down[l], b_down[l][:, None, :])
        x2d = _moe_combine(pos_flat, gates_e, x2, ys)
    return x2d.reshape(batch, seq, d)
```

```python
import functools
import math

import numpy as np
import jax
import jax.numpy as jnp
from jax import lax
from jax.experimental import pallas as pl
from jax.experimental.pallas import tpu as pltpu

F32 = jnp.float32
BF16 = jnp.bfloat16

EPS = 1e-6
NEG_INF = -1e30
LANES = 128

MLA_HEADS = 8
MLA_NOPE = 64
MLA_ROPE = 32
MLA_V = 64
MLA_Q_RANK = 256
MLA_KV_RANK = 128
ROPE_THETA = 10000.0

NSA_HEADS = 8
NSA_KV_GROUPS = 2
NSA_HPG = NSA_HEADS // NSA_KV_GROUPS
NSA_HD = 64
CMP_LEN = 32
CMP_STRIDE = 16
CMP_HIDDEN = 128
SLC_LEN = 64
SLC_TOPK = 16
WIN = 512
FORCE_SCORE = 1e9
NSA_TQ = 256
NSA_TQS = 512
NSA_TK = 512
CMP_WINDOW_BACK = 40

REL_BUCKETS = 32
REL_MAX_EXACT = 16
REL_MAX_DIST = 512

MEM_HEADS = 4
MEM_HD = 64

N_EXPERTS = 32
TOP_K = 4
SWIGLU_LIMIT = 7.0
SWIGLU_ALPHA = 1.702
MOE_BLK = 512
ROW_TILE = 8

LOG2E = math.log2(math.e)
MASK_BIG = 2.0 ** 100

VMEM_LIMIT = 56 * 1024 * 1024


def _cparams(sem, vmem=VMEM_LIMIT):
    return pltpu.CompilerParams(dimension_semantics=sem, vmem_limit_bytes=vmem)


def _rms(x, g):
    return x * lax.rsqrt(jnp.mean(x * x, axis=-1, keepdims=True) + EPS) * g


def _seg_rms64(t, g2):
    lane = lax.broadcasted_iota(jnp.int32, t.shape, 1)
    sq = t * t
    lo = jnp.sum(jnp.where(lane < 64, sq, 0.0), axis=-1, keepdims=True)
    hi = jnp.sum(jnp.where(lane >= 64, sq, 0.0), axis=-1, keepdims=True)
    ms = jnp.where(lane < 64, lo, hi) * (1.0 / 64.0)
    return t * lax.rsqrt(ms + EPS) * g2


def _dot(a, b):
    return jnp.dot(a, b, preferred_element_type=F32)


def _with_ones(v):
    lane = lax.broadcasted_iota(jnp.int32, v.shape, 1)
    return jnp.concatenate([v, jnp.where(lane == 0, 1.0, 0.0).astype(v.dtype)], axis=1)


def _dot_nt(a, b):
    return lax.dot_general(a, b, (((1,), (1,)), ((), ())), preferred_element_type=F32)


def _softmax_weights(s, m_old):
    cols = [s[:, c * LANES:(c + 1) * LANES] for c in range(s.shape[1] // LANES)]
    m_new = jnp.maximum(m_old, jnp.max(functools.reduce(jnp.maximum, cols), axis=-1, keepdims=True))
    return m_new, jnp.concatenate([jnp.exp2((c - m_new).astype(BF16)) for c in cols], axis=1)


IN_COLS_PAD = 1792


def _in_proj_kernel(x_ref, g_ref, w_ref, mla_ref, qn_ref, kv_ref, misc_ref):
    h = _rms(x_ref[...], g_ref[...])
    p = _dot(h.astype(BF16), w_ref[...])
    mla_ref[...] = p[:, 0:384]
    qn_ref[...] = p[:, 384:896]
    for j in range(6):
        kv_ref[j] = p[:, 896 + 128 * j:1024 + 128 * j]
    misc_ref[...] = p[:, 1664:1792]


def _in_proj(x2d, g, w):
    n, d = x2d.shape
    tm = 512
    return pl.pallas_call(
        _in_proj_kernel,
        grid=(n // tm,),
        in_specs=[pl.BlockSpec((tm, d), lambda i: (i, 0)),
                  pl.BlockSpec((1, d), lambda i: (0, 0)),
                  pl.BlockSpec((d, IN_COLS_PAD), lambda i: (0, 0))],
        out_specs=[pl.BlockSpec((tm, 384), lambda i: (i, 0)),
                   pl.BlockSpec((tm, 512), lambda i: (i, 0)),
                   pl.BlockSpec((6, tm, 128), lambda i: (0, i, 0)),
                   pl.BlockSpec((tm, 128), lambda i: (i, 0))],
        out_shape=[jax.ShapeDtypeStruct((n, 384), F32),
                   jax.ShapeDtypeStruct((n, 512), F32),
                   jax.ShapeDtypeStruct((6, n, 128), F32),
                   jax.ShapeDtypeStruct((n, 128), F32)],
        compiler_params=_cparams(("parallel",)),
        name="in_proj",
    )(x2d, g, w)


def _mla_prep_kernel(mla_ref, misc_ref, gcq_ref, wuq_ref, wuqs_ref, gckv_ref, wuk_ref, wuv_ref,
                     cq0_ref, cq1_ref, ck0_ref, ck1_ref, q_ref, k_ref, v_ref):
    p = mla_ref[...]
    cqn = _rms(p[:, 0:MLA_Q_RANK], gcq_ref[...]).astype(BF16)
    ckvn = _rms(p[:, MLA_Q_RANK:MLA_Q_RANK + MLA_KV_RANK], gckv_ref[...]).astype(BF16)
    qall = _dot(cqn, wuq_ref[...])
    qswp = _dot(cqn, wuqs_ref[...])
    kall = _dot(ckvn, wuk_ref[...])
    vall = _dot(ckvn, wuv_ref[...])
    misc = misc_ref[...]
    lane = lax.broadcasted_iota(jnp.int32, misc.shape, 1)
    half = MLA_ROPE // 2
    in_rope = (lane >= MLA_NOPE) & (lane < MLA_NOPE + MLA_ROPE)
    krope = jnp.where(in_rope, pltpu.roll(misc, MLA_NOPE, axis=1), 0.0)
    kswp = jnp.where(lane < MLA_NOPE + half, pltpu.roll(misc, MLA_NOPE - half, axis=1),
                     pltpu.roll(misc, MLA_NOPE + half, axis=1))
    cq0, cq1, ck0 = cq0_ref[...], cq1_ref[...], ck0_ref[...]
    kswp_term = jnp.where(in_rope, kswp, 0.0) * ck1_ref[...]
    ones_col = jnp.where(lane == MLA_V, 1.0, 0.0)
    inv_dk = 1.0 / (MLA_NOPE + MLA_ROPE)

    def inv_rms(t):
        return lax.rsqrt(jnp.sum(t * t, axis=-1, keepdims=True) * inv_dk + EPS)

    for h in range(MLA_HEADS):
        hs = slice(h * LANES, (h + 1) * LANES)
        tq = qall[:, hs]
        q_ref[h] = (inv_rms(tq) * (tq * cq0 + qswp[:, hs] * cq1)).astype(BF16)
        tk = kall[:, hs] + krope
        k_ref[h] = (inv_rms(tk) * (tk * ck0 + kswp_term)).astype(BF16)
        v_ref[h] = (vall[:, hs] + ones_col).astype(BF16)


def _mla_prep(mla, misc, gcq, wuq, wuqs, gckv, wuk, wuv, cq0, cq1, ck0, ck1, seq):
    n = mla.shape[0]
    tm = 512
    ns = seq // tm
    full = lambda a: pl.BlockSpec(a.shape, lambda i: (0,) * a.ndim)
    rope_spec = pl.BlockSpec((tm, LANES), lambda i: (i % ns, 0))
    head_spec = pl.BlockSpec((MLA_HEADS, tm, LANES), lambda i: (0, i, 0))
    head_sds = jax.ShapeDtypeStruct((MLA_HEADS, n, LANES), BF16)
    return pl.pallas_call(
        _mla_prep_kernel,
        grid=(n // tm,),
        in_specs=[pl.BlockSpec((tm, 384), lambda i: (i, 0)),
                  pl.BlockSpec((tm, LANES), lambda i: (i, 0)),
                  full(gcq), full(wuq), full(wuqs), full(gckv), full(wuk), full(wuv),
                  rope_spec, rope_spec, rope_spec, rope_spec],
        out_specs=[head_spec, head_spec, head_spec],
        out_shape=[head_sds, head_sds, head_sds],
        compiler_params=_cparams(("parallel",)),
        name="mla_prep",
    )(mla, misc, gcq, wuq, wuqs, gckv, wuk, wuv, cq0, cq1, ck0, ck1)


def _mla_attn_kernel(q_ref, k_ref, v_ref, o_ref, m_ref, acc_ref):
    qi, ki = pl.program_id(1), pl.program_id(2)

    @pl.when(ki == 0)
    def _():
        m_ref[...] = jnp.full(m_ref.shape, NEG_INF, F32)
        acc_ref[...] = jnp.zeros(acc_ref.shape, F32)

    def step(masked):
        def head(h, carry):
            s = _dot_nt(q_ref[h], k_ref[h])
            if masked:
                row = lax.broadcasted_iota(jnp.int32, s.shape, 0)
                col = lax.broadcasted_iota(jnp.int32, s.shape, 1)
                s = jnp.where(col <= row, s, -jnp.inf)
            m_old = m_ref[h]
            m_new, p = _softmax_weights(s, m_old)
            acc_ref[h] = jnp.exp2(m_old - m_new) * acc_ref[h] + _dot(p, v_ref[h])
            m_ref[h] = m_new
            return carry
        lax.fori_loop(0, MLA_HEADS, head, 0, unroll=True)

    @pl.when(ki < qi)
    def _():
        step(False)

    @pl.when(ki == qi)
    def _():
        step(True)

    @pl.when(ki == pl.num_programs(2) - 1)
    def _():
        for h in range(MLA_HEADS):
            acc = acc_ref[h]
            o_ref[:, h * MLA_V:(h + 1) * MLA_V] = acc[:, :MLA_V] / acc[:, MLA_V:MLA_V + 1]


def _mla_attn(q, k, v, batch, seq):
    n = q.shape[1]
    tq = tk = 512
    nq = seq // tq
    kv_spec = pl.BlockSpec((MLA_HEADS, tk, LANES), lambda b, i, j: (0, b * nq + jnp.minimum(i, j), 0))
    return pl.pallas_call(
        _mla_attn_kernel,
        grid=(batch, nq, nq),
        in_specs=[pl.BlockSpec((MLA_HEADS, tq, LANES), lambda b, i, j: (0, b * nq + i, 0)), kv_spec, kv_spec],
        out_specs=pl.BlockSpec((tq, MLA_HEADS * MLA_V), lambda b, i, j: (b * nq + i, 0)),
        out_shape=jax.ShapeDtypeStruct((n, MLA_HEADS * MLA_V), F32),
        scratch_shapes=[pltpu.VMEM((MLA_HEADS, tq, LANES), F32),
                        pltpu.VMEM((MLA_HEADS, tq, LANES), F32)],
        compiler_params=_cparams(("parallel", "parallel", "arbitrary")),
        name="mla_attn",
    )(q, k, v)


def _nsa_cmp_kernel(kc_ref, vc_ref, pak_ref, pbk_ref, wak_ref, wbk_ref, w2k_ref,
                    pav_ref, pbv_ref, wav_ref, wbv_ref, w2v_ref, gk_ref, kout_ref, vout_ref):
    def compress(chunks, pa, pb, wa, wb, w2):
        nc = chunks.shape[0]
        ha = _dot((chunks + pa).astype(BF16), wa)
        hb = _dot((chunks + pb).astype(BF16), wb)
        hid = jax.nn.gelu(ha + pltpu.roll(hb, nc - 1, axis=0))
        return [_dot(hid[:, g * CMP_HIDDEN:(g + 1) * CMP_HIDDEN].astype(BF16), w2)
                for g in range(NSA_KV_GROUPS)]

    kc = compress(kc_ref[0], pak_ref[...], pbk_ref[...], wak_ref[...], wbk_ref[...], w2k_ref[...])
    vc = compress(vc_ref[0], pav_ref[...], pbv_ref[...], wav_ref[...], wbv_ref[...], w2v_ref[...])
    for g in range(NSA_KV_GROUPS):
        kn = _rms(kc[g], gk_ref[...])
        kout_ref[0, g] = jnp.concatenate([kn, jnp.zeros_like(kn)], axis=1).astype(BF16)
        vout_ref[0, g] = vc[g].astype(BF16)


def _nsa_cmp(kc_chunks, vc_chunks, wk, wv, gk):
    b, nc, width = kc_chunks.shape
    full = lambda a: pl.BlockSpec(a.shape, lambda i: (0,) * a.ndim)
    chunk_spec = pl.BlockSpec((1, nc, width), lambda i: (i, 0, 0))
    out_spec = lambda width: pl.BlockSpec((1, NSA_KV_GROUPS, nc, width), lambda i: (i, 0, 0, 0))
    out_sds = lambda width: jax.ShapeDtypeStruct((b, NSA_KV_GROUPS, nc, width), BF16)
    return pl.pallas_call(
        _nsa_cmp_kernel,
        grid=(b,),
        in_specs=[chunk_spec, chunk_spec] + [full(a) for a in wk] + [full(a) for a in wv] + [full(gk)],
        out_specs=[out_spec(LANES), out_spec(NSA_HD)],
        out_shape=[out_sds(LANES), out_sds(NSA_HD)],
        compiler_params=_cparams(("parallel",)),
        name="nsa_cmp",
    )(kc_chunks, vc_chunks, *wk, *wv, gk)


def _nsa_prep_kernel(qn_ref, ks_ref, vs_ref, kw_ref, vw_ref, misc_ref, gq_ref, gk_ref,
                     q_ref, kso_ref, vso_ref, kwo_ref, vwo_ref, gate_ref, *, seq):
    gq2, gk2 = gq_ref[...], gk_ref[...]
    tm = qn_ref.shape[0]
    scale = NSA_HD ** -0.5 * LOG2E
    zpad = jnp.zeros((tm, NSA_HD), F32)
    for c in range(NSA_HEADS // 2):
        t = _seg_rms64(qn_ref[:, c * LANES:(c + 1) * LANES], gq2) * scale
        q_ref[2 * c] = jnp.concatenate([t[:, :NSA_HD], zpad], axis=1).astype(BF16)
        q_ref[2 * c + 1] = jnp.concatenate([t[:, NSA_HD:], zpad], axis=1).astype(BF16)
    ksn = _seg_rms64(ks_ref[0], gk2)
    kwn = _seg_rms64(kw_ref[0], gk2)
    vs, vw = vs_ref[0], vw_ref[0]
    row = lax.broadcasted_iota(jnp.int32, (tm, LANES), 0)
    lane = lax.broadcasted_iota(jnp.int32, (tm, LANES), 1)
    pos = (pl.program_id(0) * tm) % seq + row
    neg_onehot = jnp.where(lane == pos // SLC_LEN, -1.0, 0.0)
    for g in range(NSA_KV_GROUPS):
        sl = slice(g * NSA_HD, (g + 1) * NSA_HD)
        kso_ref[g] = jnp.concatenate([neg_onehot, ksn[:, sl], zpad], axis=1).astype(BF16)
        kwo_ref[g] = jnp.concatenate([kwn[:, sl], zpad], axis=1).astype(BF16)
        vso_ref[g] = _with_ones(vs[:, sl]).astype(BF16)
        vwo_ref[g] = _with_ones(vw[:, sl]).astype(BF16)
    sig = jax.nn.sigmoid(misc_ref[...])
    per_group = 3 * NSA_HPG
    for g in range(NSA_KV_GROUPS):
        gate_ref[g] = pltpu.roll(sig, LANES - (MLA_ROPE + per_group * g), axis=1)


def _nsa_prep(qn, kv, misc, gq2, gk2, seq):
    n = qn.shape[0]
    tm = 512
    full = lambda a: pl.BlockSpec(a.shape, lambda i: (0,) * a.ndim)
    kv_spec = lambda j: pl.BlockSpec((1, tm, LANES), lambda i, j=j: (j, i, 0))
    g_spec = lambda width: pl.BlockSpec((NSA_KV_GROUPS, tm, width), lambda i: (0, i, 0))
    g_sds = lambda width: jax.ShapeDtypeStruct((NSA_KV_GROUPS, n, width), BF16)
    return pl.pallas_call(
        functools.partial(_nsa_prep_kernel, seq=seq),
        grid=(n // tm,),
        in_specs=[pl.BlockSpec((tm, 512), lambda i: (i, 0)),
                  kv_spec(2), kv_spec(3), kv_spec(4), kv_spec(5),
                  pl.BlockSpec((tm, LANES), lambda i: (i, 0)), full(gq2), full(gk2)],
        out_specs=[pl.BlockSpec((NSA_HEADS, tm, LANES), lambda i: (0, i, 0)),
                   g_spec(2 * LANES), g_spec(LANES), g_spec(LANES), g_spec(LANES),
                   pl.BlockSpec((NSA_KV_GROUPS, tm, LANES), lambda i: (0, i, 0))],
        out_shape=[jax.ShapeDtypeStruct((NSA_HEADS, n, LANES), BF16),
                   g_sds(2 * LANES), g_sds(LANES), g_sds(LANES), g_sds(LANES),
                   jax.ShapeDtypeStruct((NSA_KV_GROUPS, n, LANES), F32)],
        compiler_params=_cparams(("parallel",)),
        name="nsa_prep",
    )(qn, kv, kv, kv, kv, misc, gq2, gk2)


def _nsa_select_kernel(q_ref, kc_ref, vct_ref, tct_ref, ovlt_ref, ocmpt_ref, sel_ref, s_ref, *, nc, n_sel):
    tq, hpg = NSA_TQS, NSA_HPG
    i = pl.program_id(2)
    q0 = i * tq
    n_forced = 3
    n_live = jnp.minimum((q0 + tq - CMP_LEN) // (CMP_STRIDE * LANES) + 1, nc // LANES)
    for ncb in range(1, nc // LANES + 1):
        pl.when(n_live == ncb)(functools.partial(
            _nsa_select_body, q_ref, kc_ref, vct_ref, tct_ref, ovlt_ref, ocmpt_ref, sel_ref, s_ref,
            ncb=ncb, n_sel=n_sel, n_forced=n_forced))


def _nsa_select_body(q_ref, kc_ref, vct_ref, tct_ref, ovlt_ref, ocmpt_ref, sel_ref, s_ref, *, ncb, n_sel, n_forced):
    tq, hpg = NSA_TQS, NSA_HPG
    i = pl.program_id(2)
    q0 = i * tq
    width = ncb * LANES
    s_ref[:width, :] = _dot_nt(kc_ref[0, 0, :width, :], q_ref[...].reshape(hpg * tq, LANES))

    first_blk = ((tq // CMP_STRIDE) * i + LANES - CMP_WINDOW_BACK) // LANES - 1
    for half in range(2):
        blk = first_blk + half

        @pl.when((blk >= 0) & (blk < ncb))
        def _():
            r0 = pl.multiple_of(blk * LANES, LANES)
            for h in range(hpg):
                s_ref[pl.ds(r0, LANES), h * tq:(h + 1) * tq] += tct_ref[0, h, half * LANES:(half + 1) * LANES, :]

    blk_row = lax.broadcasted_iota(jnp.int32, (width, tq), 0)
    qpos = q0 + lax.broadcasted_iota(jnp.int32, (width, tq), 1)
    valid = (CMP_STRIDE * blk_row + (CMP_LEN - 1)) <= qpos
    vct = vct_ref[0, 0, :, :width]
    psum = jnp.zeros((width, tq), F32)
    for h in range(hpg):
        sh = jnp.where(valid, s_ref[:width, h * tq:(h + 1) * tq], -jnp.inf)
        m = jnp.maximum(jnp.max(sh, axis=0, keepdims=True), NEG_INF)
        e = jnp.exp2(sh - m)
        p = e * (1.0 / jnp.maximum(jnp.sum(e, axis=0, keepdims=True), 1e-30))
        psum = psum + p
        ocmpt_ref[h * NSA_HD:(h + 1) * NSA_HD, :] = _dot(vct, p.astype(BF16))

    hi = psum.astype(BF16)
    rest = psum - hi.astype(F32)
    mid = rest.astype(BF16)
    lo = (rest - mid.astype(F32)).astype(BF16)
    ovlt = ovlt_ref[:, :width]
    imp = _dot(ovlt, hi) + _dot(ovlt, mid) + _dot(ovlt, lo)

    jrow = lax.broadcasted_iota(jnp.int32, (LANES, tq), 0)
    qpos = q0 + lax.broadcasted_iota(jnp.int32, (LANES, tq), 1)
    jf = jrow.astype(F32)
    cur = qpos // SLC_LEN
    forced = (jrow == 0) | (jrow == cur) | (jrow == cur - 1)
    imp = jnp.where(jrow * SLC_LEN <= qpos, imp, NEG_INF)
    if n_sel >= n_forced:
        imp, rounds = jnp.where(forced, -jnp.inf, imp), n_sel - n_forced
    else:
        imp, rounds = jnp.where(forced, FORCE_SCORE, imp), n_sel
    for _ in range(rounds):
        best = jnp.max(imp, axis=0, keepdims=True)
        first = jnp.min(jnp.where(imp == best, jf, float(LANES)), axis=0, keepdims=True)
        imp = jnp.where(jf == first, -jnp.inf, imp)
    sel_ref[0] = jnp.where(imp == -jnp.inf, 0.0, MASK_BIG).T.astype(BF16)


def _nsa_select(q, kcmp, vcmp_t, tct, ovl_t, batch, seq):
    n = q.shape[1]
    nc = kcmp.shape[2]
    nq = seq // NSA_TQS
    n_res = tct.shape[0]
    n_sel = min(SLC_TOPK, seq // SLC_LEN)
    g_, hpg = NSA_KV_GROUPS, NSA_HPG
    kern = functools.partial(_nsa_select_kernel, nc=nc, n_sel=n_sel)
    cmp_spec = lambda shape: pl.BlockSpec((1, 1) + shape, lambda b, g, i: (b, g, 0, 0))
    return pl.pallas_call(
        kern,
        grid=(batch, g_, nq),
        in_specs=[pl.BlockSpec((hpg, NSA_TQS, LANES), lambda b, g, i: (g, b * nq + i, 0)),
                  cmp_spec((nc, LANES)), cmp_spec((NSA_HD, nc)),
                  pl.BlockSpec((1, hpg, 2 * LANES, NSA_TQS), lambda b, g, i: (i % n_res, g, 0, 0)),
                  pl.BlockSpec(ovl_t.shape, lambda b, g, i: (0, 0))],
        out_specs=[pl.BlockSpec((hpg * NSA_HD, NSA_TQS), lambda b, g, i: (g, b * nq + i)),
                   pl.BlockSpec((1, NSA_TQS, LANES), lambda b, g, i: (g, b * nq + i, 0))],
        out_shape=[jax.ShapeDtypeStruct((NSA_HEADS * NSA_HD, n), F32),
                   jax.ShapeDtypeStruct((g_, n, LANES), BF16)],
        scratch_shapes=[pltpu.VMEM((nc, hpg * NSA_TQS), F32)],
        compiler_params=_cparams(("parallel", "parallel", "arbitrary")),
        name="nsa_select",
    )(q, kcmp, vcmp_t, tct, ovl_t)


def _nsa_attn_kernel(q_ref, ks_ref, vs_ref, kw_ref, vw_ref, sel_ref, tn_ref, gate_ref, ocmpt_ref,
                     o_ref, m_ref, acc_ref):
    tq, tk, hpg = NSA_TQ, NSA_TK, NSA_HPG
    rows = hpg * tq
    i = pl.program_id(2)
    q0 = i * tq
    q4 = q_ref[...].reshape(rows, LANES)
    q_aug = jnp.concatenate([jnp.concatenate([sel_ref[0]] * hpg, axis=0), q4], axis=1)
    row = lax.broadcasted_iota(jnp.int32, (tq, tk), 0)
    col = lax.broadcasted_iota(jnp.int32, (tq, tk), 1)
    m_ref[...] = jnp.full(m_ref.shape, NEG_INF, F32)
    acc_ref[...] = jnp.zeros(acc_ref.shape, F32)

    def tile(kt, near):
        k0 = pl.multiple_of(kt * tk, tk)
        s = _dot_nt(q_aug, ks_ref[0, 0, pl.ds(k0, tk), :])
        if near:
            causal = (k0 + col) <= (q0 + row)
            start = pl.multiple_of(k0 + tk + WIN - q0, LANES)
        ps, alphas = [], []
        for h in range(hpg):
            rs = slice(h * tq, (h + 1) * tq)
            sh = s[rs]
            if near:
                sh = jnp.where(causal, sh + tn_ref[h, :, pl.ds(start, tk)], -jnp.inf)
            m_old = m_ref[rs]
            m_new, p = _softmax_weights(sh, m_old)
            m_ref[rs] = m_new
            alphas.append(jnp.exp2(m_old - m_new))
            ps.append(p)
        acc_ref[...] = (jnp.concatenate(alphas, axis=0) * acc_ref[...]
                        + _dot(jnp.concatenate(ps, axis=0), vs_ref[0, 0, pl.ds(k0, tk), :]))

    last = q0 // tk
    n_far = jnp.maximum(last - 1, 0)
    def far_pair(pair, c):
        tile(2 * pair, False)
        tile(2 * pair + 1, False)
        return c

    lax.fori_loop(0, n_far // 2, far_pair, 0)
    lax.fori_loop(n_far // 2 * 2, n_far, lambda kt, c: (tile(kt, False), c)[1], 0)
    lax.fori_loop(n_far, last + 1, lambda kt, c: (tile(kt, True), c)[1], 0)

    wk = WIN + tq
    w0 = pl.multiple_of(q0, tq)
    s = _dot_nt(q4, kw_ref[0, 0, pl.ds(w0, wk), :])
    roww = lax.broadcasted_iota(jnp.int32, (tq, wk), 0)
    colw = lax.broadcasted_iota(jnp.int32, (tq, wk), 1)
    dist = roww + WIN - colw
    validw = (dist >= 0) & (dist < WIN) & (q0 - WIN + colw >= 0)
    ps = []
    for h in range(hpg):
        sh = s[h * tq:(h + 1) * tq] + tn_ref[h, :, tk:tk + wk]
        ps.append(_softmax_weights(jnp.where(validw, sh, -jnp.inf), NEG_INF)[1])
    accw = _dot(jnp.concatenate(ps, axis=0), vw_ref[0, 0, pl.ds(w0, wk), :])

    gates = gate_ref[0]
    acc = acc_ref[...]
    outs = []
    for h in range(hpg):
        rs = slice(h * tq, (h + 1) * tq)
        o_cmp = ocmpt_ref[h * NSA_HD:(h + 1) * NSA_HD, :].T
        o_slc = acc[rs, :NSA_HD] / jnp.maximum(acc[rs, NSA_HD:NSA_HD + 1], 1e-30)
        o_win = accw[rs, :NSA_HD] / accw[rs, NSA_HD:NSA_HD + 1]
        outs.append(gates[:, 3 * h:3 * h + 1] * o_cmp + gates[:, 3 * h + 1:3 * h + 2] * o_slc
                    + gates[:, 3 * h + 2:3 * h + 3] * o_win)
    o_ref[...] = jnp.concatenate(outs, axis=1)


def _nsa_attn(q, ks, vs, kwp, vwp, sel, tn, gates, ocmp_t, batch, seq):
    n = q.shape[1]
    nq = seq // NSA_TQ
    g_, hpg = NSA_KV_GROUPS, NSA_HPG
    seq_spec = lambda length, width: pl.BlockSpec((1, 1, length, width), lambda b, g, i: (g, b, 0, 0))
    tok_spec = pl.BlockSpec((1, NSA_TQ, LANES), lambda b, g, i: (g, b * nq + i, 0))
    out_spec = pl.BlockSpec((NSA_TQ, hpg * NSA_HD), lambda b, g, i: (b * nq + i, g))
    return pl.pallas_call(
        _nsa_attn_kernel,
        grid=(batch, g_, nq),
        in_specs=[pl.BlockSpec((hpg, NSA_TQ, LANES), lambda b, g, i: (g, b * nq + i, 0)),
                  seq_spec(seq, 2 * LANES), seq_spec(seq, LANES), seq_spec(seq + WIN, LANES), seq_spec(seq + WIN, LANES),
                  tok_spec,
                  pl.BlockSpec((hpg, NSA_TQ, 2 * NSA_TK + WIN), lambda b, g, i: (g, 0, 0)),
                  tok_spec, pl.BlockSpec((hpg * NSA_HD, NSA_TQ), lambda b, g, i: (g, b * nq + i))],
        out_specs=out_spec,
        out_shape=jax.ShapeDtypeStruct((n, NSA_HEADS * NSA_HD), F32),
        scratch_shapes=[pltpu.VMEM((hpg * NSA_TQ, LANES), F32), pltpu.VMEM((hpg * NSA_TQ, LANES), F32)],
        compiler_params=_cparams(("parallel", "parallel", "arbitrary")),
        name="nsa_attn",
    )(q, ks, vs, kwp, vwp, sel, tn, gates, ocmp_t)


def _mem_kv_kernel(mem_ref, g_ref, w_ref, gk_ref, k_ref, v_ref):
    mn = _rms(mem_ref[0], g_ref[...]).astype(BF16)
    kv = _dot(mn, w_ref[...])
    width = MEM_HEADS * MEM_HD
    for c in range(width // LANES):
        kn = _seg_rms64(kv[:, c * LANES:(c + 1) * LANES], gk_ref[...])
        vv = kv[:, width + c * LANES:width + (c + 1) * LANES]
        for half in range(2):
            sl = slice(half * MEM_HD, (half + 1) * MEM_HD)
            k_ref[0, 2 * c + half] = kn[:, sl].astype(BF16)
            v_ref[0, 2 * c + half] = vv[:, sl].astype(BF16)


def _mem_kv(mem, g, w, gk2):
    b, m, d = mem.shape
    full = lambda a: pl.BlockSpec(a.shape, lambda i: (0,) * a.ndim)
    spec = pl.BlockSpec((1, MEM_HEADS, m, MEM_HD), lambda i: (i, 0, 0, 0))
    sds = jax.ShapeDtypeStruct((b, MEM_HEADS, m, MEM_HD), BF16)
    return pl.pallas_call(
        _mem_kv_kernel,
        grid=(b,),
        in_specs=[pl.BlockSpec((1, m, d), lambda i: (i, 0, 0)), full(g), full(w), full(gk2)],
        out_specs=[spec, spec],
        out_shape=[sds, sds],
        compiler_params=_cparams(("parallel",)),
        name="mem_kv",
    )(mem, g, w, gk2)


def _post_attn_kernel(x_ref, ya_ref, yb_ref, goa_ref, gob_ref, woa_ref, wob_ref, gmn_ref, wmq_ref, gmq_ref,
                      km_ref, vm_ref, wmo_ref, gmoe_ref, rwh_ref, rwl_ref, rb_ref,
                      x2_ref, h2_ref, topi_ref, gate_ref, cnt_ref):
    first = (pl.program_id(0) == 0) & (pl.program_id(1) == 0)

    @pl.when(first)
    def _():
        cnt_ref[...] = jnp.zeros(cnt_ref.shape, F32)

    mixa = _rms(ya_ref[...], goa_ref[...]).astype(BF16)
    mixb = _rms(yb_ref[...], gob_ref[...]).astype(BF16)
    x1 = x_ref[...] + _dot(mixa, woa_ref[...]) + _dot(mixb, wob_ref[...])

    h = _rms(x1, gmn_ref[...]).astype(BF16)
    q = _dot(h, wmq_ref[...])
    scale = MEM_HD ** -0.5
    outs = []
    for c in range(MEM_HEADS * MEM_HD // LANES):
        qn = _seg_rms64(q[:, c * LANES:(c + 1) * LANES], gmq_ref[...]) * scale
        for half in range(2):
            hd = 2 * c + half
            qh = qn[:, half * MEM_HD:(half + 1) * MEM_HD].astype(BF16)
            s = _dot_nt(qh, km_ref[0, hd])
            e = jnp.exp(s - jnp.max(s, axis=-1, keepdims=True))
            p = e / jnp.sum(e, axis=-1, keepdims=True)
            outs.append(_dot(p.astype(BF16), vm_ref[0, hd]))
    o = jnp.concatenate(outs, axis=1).astype(BF16)
    x2 = x1 + _dot(o, wmo_ref[...])
    x2_ref[...] = x2

    h2 = _rms(x2, gmoe_ref[...])
    for c in range(ROW_TILE):
        h2_ref[pl.ds(c, h2.shape[0], stride=ROW_TILE), :] = h2[:, c * LANES:(c + 1) * LANES]
    h_hi = h2.astype(BF16)
    h_lo = (h2 - h_hi.astype(F32)).astype(BF16)
    logits = _dot(h_hi, rwh_ref[...]) + _dot(h_hi, rwl_ref[...]) + _dot(h_lo, rwh_ref[...]) + rb_ref[...]
    lane = lax.broadcasted_iota(jnp.int32, logits.shape, 1)
    lane_f = lane.astype(F32)
    topi = jnp.zeros(logits.shape, jnp.int32)
    topv = jnp.full(logits.shape, NEG_INF, F32)
    onehot = jnp.zeros(logits.shape, F32)
    for k in range(TOP_K):
        best = jnp.max(logits, axis=-1, keepdims=True)
        first_idx = jnp.min(jnp.where(logits == best, lane_f, float(LANES)), axis=-1, keepdims=True)
        hit = lane_f == first_idx
        topi = jnp.where(lane == k, first_idx.astype(jnp.int32), topi)
        topv = jnp.where(lane == k, best, topv)
        onehot = jnp.where(hit, 1.0, onehot)
        logits = jnp.where(hit, -jnp.inf, logits)
    e = jnp.where(lane < TOP_K, jnp.exp(topv - jnp.max(topv, axis=-1, keepdims=True)), 0.0)
    gate_ref[...] = e / jnp.sum(e, axis=-1, keepdims=True)
    topi_ref[...] = topi
    cnt_ref[...] += jnp.sum(onehot, axis=0, keepdims=True)


def _post_attn(x2d, ya, yb, goa, gob, woa, wob, gmn, wmq, gmq2, km, vm, wmo, gmoe, rwh, rwl, rb, batch, seq):
    n, d = x2d.shape
    tm = 512
    ns = seq // tm
    full = lambda a: pl.BlockSpec(a.shape, lambda b, i: (0,) * a.ndim)
    tok = lambda width: pl.BlockSpec((tm, width), lambda b, i: (b * ns + i, 0))
    mem_spec = pl.BlockSpec((1,) + km.shape[1:], lambda b, i: (b, 0, 0, 0))
    return pl.pallas_call(
        _post_attn_kernel,
        grid=(batch, ns),
        in_specs=[tok(d), tok(ya.shape[1]), tok(yb.shape[1]), full(goa), full(gob), full(woa), full(wob),
                  full(gmn), full(wmq), full(gmq2), mem_spec, mem_spec, full(wmo), full(gmoe),
                  full(rwh), full(rwl), full(rb)],
        out_specs=[tok(d), pl.BlockSpec((tm * ROW_TILE, LANES), lambda b, i: (b * ns + i, 0)),
                   tok(LANES), tok(LANES), pl.BlockSpec((1, LANES), lambda b, i: (0, 0))],
        out_shape=[jax.ShapeDtypeStruct((n, d), F32),
                   jax.ShapeDtypeStruct((n * ROW_TILE, LANES), F32),
                   jax.ShapeDtypeStruct((n, LANES), jnp.int32),
                   jax.ShapeDtypeStruct((n, LANES), F32),
                   jax.ShapeDtypeStruct((1, LANES), F32)],
        compiler_params=_cparams(("arbitrary", "arbitrary")),
        name="post_attn",
    )(x2d, ya, yb, goa, gob, woa, wob, gmn, wmq, gmq2, km, vm, wmo, gmoe, rwh, rwl, rb)


def _moe_pos_kernel(topi_ref, start_ref, tri_ref, pos_ref, carry_ref):
    @pl.when(pl.program_id(0) == 0)
    def _():
        carry_ref[...] = jnp.zeros(carry_ref.shape, F32)

    topi = topi_ref[...]
    lane = lax.broadcasted_iota(jnp.int32, topi.shape, 1)
    hits = [lane == topi[:, k:k + 1] for k in range(TOP_K)]
    onehot = sum(h.astype(F32) for h in hits)
    before = _dot(tri_ref[...], onehot.astype(BF16))
    base = start_ref[...] + carry_ref[...] + before
    pos = jnp.zeros(topi.shape, jnp.int32)
    for k in range(TOP_K):
        pk = jnp.sum(jnp.where(hits[k], base, 0.0), axis=-1, keepdims=True).astype(jnp.int32)
        pos = jnp.where(lane == k, pk, pos)
    pos_ref[...] = pos
    carry_ref[...] += jnp.sum(onehot, axis=0, keepdims=True)


def _moe_pos(topi, pad_start, tri):
    n = topi.shape[0]
    tm = tri.shape[0]
    return pl.pallas_call(
        _moe_pos_kernel,
        grid=(n // tm,),
        in_specs=[pl.BlockSpec((tm, LANES), lambda i: (i, 0)),
                  pl.BlockSpec((1, LANES), lambda i: (0, 0)),
                  pl.BlockSpec((tm, tm), lambda i: (0, 0))],
        out_specs=pl.BlockSpec((tm, LANES), lambda i: (i, 0)),
        out_shape=jax.ShapeDtypeStruct((n, LANES), jnp.int32),
        scratch_shapes=[pltpu.VMEM((1, LANES), F32)],
        compiler_params=_cparams(("arbitrary",)),
        name="moe_pos",
    )(topi, pad_start, tri)


def _moe_scatter_kernel(pos_ref, h_ref, zero_ref, xs_ref, sem, *, tm):
    del zero_ref

    def copy(t, k):
        dst = pl.multiple_of(pos_ref[t * TOP_K + k] * ROW_TILE, ROW_TILE)
        return pltpu.make_async_copy(h_ref.at[pl.ds(pl.multiple_of(t * ROW_TILE, ROW_TILE), ROW_TILE)],
                                     xs_ref.at[pl.ds(dst, ROW_TILE)], sem)

    def issue(t, c):
        for k in range(TOP_K):
            copy(t, k).start(priority=k % 2)
        return c

    def drain(t, c):
        for k in range(TOP_K):
            copy(t, k).wait()
        return c

    lax.fori_loop(0, tm, issue, 0, unroll=4)
    lax.fori_loop(0, tm, drain, 0, unroll=4)


def _moe_scatter(pos_flat, h2, xs_zero):
    n = h2.shape[0] // ROW_TILE
    tm = 512
    kern = functools.partial(_moe_scatter_kernel, tm=tm)
    return pl.pallas_call(
        kern,
        grid=(n // tm,),
        in_specs=[pl.BlockSpec((tm * TOP_K,), lambda i: (i,), memory_space=pltpu.SMEM),
                  pl.BlockSpec((tm * ROW_TILE, LANES), lambda i: (i, 0)),
                  pl.BlockSpec(memory_space=pl.ANY)],
        out_specs=pl.BlockSpec(memory_space=pl.ANY),
        out_shape=jax.ShapeDtypeStruct(xs_zero.shape, xs_zero.dtype),
        scratch_shapes=[pltpu.SemaphoreType.DMA(())],
        input_output_aliases={2: 0},
        compiler_params=_cparams(("arbitrary",)),
        name="moe_scatter",
    )(pos_flat, h2, xs_zero)


GU_GROUP = 2 * LANES


def _moe_ffn_kernel(blk_e_ref, n_used_ref, x_ref, wgu_ref, perm_ref, bg_ref, bu_ref, wd_ref, bd_ref, y_ref,
                    wgu_bf_ref, wd_bf_ref):
    step = pl.program_id(0)
    i = step - 1
    last_blk = pl.num_programs(0) - 2
    chunk = lambda c: pl.ds(c, MOE_BLK, stride=ROW_TILE)

    @pl.when((step >= 1) & (i < n_used_ref[0]))
    def _():
        x = jnp.concatenate([x_ref[chunk(c), :] for c in range(ROW_TILE)], axis=1).astype(BF16)
        gu = _dot(x, wgu_bf_ref[...])
        bg, bu = bg_ref[0], bu_ref[0]
        acts = []
        for c in range(gu.shape[1] // GU_GROUP):
            fs = slice(c * LANES, (c + 1) * LANES)
            gate = jnp.minimum(gu[:, c * GU_GROUP:c * GU_GROUP + LANES] + bg[:, fs], SWIGLU_LIMIT)
            up = jnp.clip(gu[:, c * GU_GROUP + LANES:(c + 1) * GU_GROUP] + bu[:, fs], -SWIGLU_LIMIT, SWIGLU_LIMIT)
            acts.append(((up + 1.0) * gate * jax.nn.sigmoid(SWIGLU_ALPHA * gate)).astype(BF16))
        act = jnp.concatenate(acts, axis=1)
        y = _dot(act, wd_bf_ref[...]) + bd_ref[0]
        for c in range(ROW_TILE):
            y_ref[chunk(c), :] = y[:, c * LANES:(c + 1) * LANES]

    @pl.when((step >= 1) & (i >= n_used_ref[0]))
    def _():
        y_ref[...] = jnp.zeros(y_ref.shape, F32)

    nxt = jnp.minimum(step, last_blk)
    new_expert = (step == 0) | (blk_e_ref[nxt] != blk_e_ref[jnp.maximum(i, 0)])

    @pl.when(new_expert & (step < n_used_ref[0]))
    def _():
        for c in range(wgu_ref.shape[2] // GU_GROUP):
            sl = slice(c * GU_GROUP, (c + 1) * GU_GROUP)
            wgu_bf_ref[:, sl] = _dot(wgu_ref[0, :, sl].astype(BF16), perm_ref[...]).astype(BF16)
        wd_bf_ref[...] = wd_ref[0].astype(BF16)


def _moe_ffn(blk_e, n_used, xs, wgu, perm, bg, bu, wd, bd):
    p = xs.shape[0] // ROW_TILE
    dff, d = wd.shape[1:]
    n_blk = p // MOE_BLK
    cur = lambda s: jnp.maximum(s - 1, 0)
    ahead = lambda s: jnp.minimum(s, n_blk - 1)
    w_spec = lambda r, c: pl.BlockSpec((1, r, c), lambda s, be, nu: (be[ahead(s)], 0, 0))
    b_spec = lambda c: pl.BlockSpec((1, 1, c), lambda s, be, nu: (be[cur(s)], 0, 0))
    row_spec = pl.BlockSpec((MOE_BLK * ROW_TILE, LANES), lambda s, be, nu: (cur(s), 0))
    grid_spec = pltpu.PrefetchScalarGridSpec(
        num_scalar_prefetch=2,
        grid=(n_blk + 1,),
        in_specs=[row_spec,
                  w_spec(d, 2 * dff), pl.BlockSpec(perm.shape, lambda s, be, nu: (0, 0)), b_spec(dff), b_spec(dff),
                  w_spec(dff, d), b_spec(d)],
        out_specs=row_spec,
        scratch_shapes=[pltpu.VMEM((d, 2 * dff), BF16), pltpu.VMEM((dff, d), BF16)],
    )
    return pl.pallas_call(
        _moe_ffn_kernel,
        grid_spec=grid_spec,
        out_shape=jax.ShapeDtypeStruct(xs.shape, F32),
        compiler_params=_cparams(("arbitrary",)),
        name="moe_ffn",
    )(blk_e, n_used, xs, wgu, perm, bg, bu, wd, bd)


def _moe_combine_kernel(pos_ref, gate_ref, x_ref, ys_ref, o_ref, buf_ref, sem, *, tm):
    def copy(t, k):
        src = pl.multiple_of(pos_ref[t * TOP_K + k] * ROW_TILE, ROW_TILE)
        return pltpu.make_async_copy(ys_ref.at[pl.ds(src, ROW_TILE)],
                                     buf_ref.at[k, pl.ds(pl.multiple_of(t * ROW_TILE, ROW_TILE), ROW_TILE)], sem)

    def issue(t, c):
        for k in range(TOP_K):
            copy(t, k).start(priority=k % 2)
        return c

    def drain(t, c):
        for k in range(TOP_K):
            copy(t, k).wait()
        return c

    lax.fori_loop(0, tm, issue, 0, unroll=4)
    lax.fori_loop(0, tm, drain, 0, unroll=4)
    gates = gate_ref[...]
    for c in range(ROW_TILE):
        cs = slice(c * LANES, (c + 1) * LANES)
        acc = x_ref[:, cs]
        for k in range(TOP_K):
            acc = acc + gates[:, k:k + 1] * buf_ref[k, pl.ds(c, tm, stride=ROW_TILE), :]
        o_ref[:, cs] = acc


def _moe_combine(pos_flat, gates, x2, ys):
    n, d = x2.shape
    tm = 256
    kern = functools.partial(_moe_combine_kernel, tm=tm)
    return pl.pallas_call(
        kern,
        grid=(n // tm,),
        in_specs=[pl.BlockSpec((tm * TOP_K,), lambda i: (i,), memory_space=pltpu.SMEM),
                  pl.BlockSpec((tm, LANES), lambda i: (i, 0)),
                  pl.BlockSpec((tm, d), lambda i: (i, 0)),
                  pl.BlockSpec(memory_space=pl.ANY)],
        out_specs=pl.BlockSpec((tm, d), lambda i: (i, 0)),
        out_shape=jax.ShapeDtypeStruct((n, d), F32),
        scratch_shapes=[pltpu.VMEM((TOP_K, tm * ROW_TILE, LANES), F32), pltpu.SemaphoreType.DMA(())],
        compiler_params=_cparams(("arbitrary",)),
        name="moe_combine",
    )(pos_flat, gates, x2, ys)


def _rel_bucket_np(dist):
    n = np.maximum(dist, 0)
    nf = np.maximum(n, 1).astype(np.float32)
    ratio = np.log(nf / np.float32(REL_MAX_EXACT)) / np.float32(math.log(REL_MAX_DIST / REL_MAX_EXACT))
    large = REL_MAX_EXACT + (ratio * np.float32(REL_BUCKETS - REL_MAX_EXACT)).astype(np.int32)
    large = np.minimum(large, REL_BUCKETS - 1)
    return np.where(n < REL_MAX_EXACT, n, large)


def _bias_tables_kernel(tbl_ref, bc_ref, bt_ref, tc_ref, tn_ref):
    h = pl.program_id(0)
    bc, bt = bc_ref[...], bt_ref[...]
    tc = jnp.zeros(bc.shape, F32)
    tn = jnp.zeros(bt.shape, F32)
    far = tbl_ref[REL_BUCKETS - 1, h]
    for b in range(REL_BUCKETS - 1):
        v = (tbl_ref[b, h] - far) * LOG2E
        tc = jnp.where(bc == b, v, tc)
        tn = jnp.where(bt == b, v, tn)
    tc_ref[:, 0] = tc
    tn_ref[0] = tn


def _bias_tables(rel_table):
    q = np.arange(NSA_TQS)[None, :, None]
    blocks_per_tile = NSA_TQS // CMP_STRIDE
    n_res = LANES // blocks_per_tile
    res = np.arange(n_res)[:, None, None]
    base_gap = (blocks_per_tile * res - CMP_WINDOW_BACK) % LANES + CMP_WINDOW_BACK
    w = np.arange(2 * LANES)[None, None, :]
    b_cmp = _rel_bucket_np(CMP_STRIDE * base_gap + q - CMP_STRIDE * w - (CMP_LEN - 1)).astype(np.int32)
    b_cmp = np.ascontiguousarray(b_cmp.transpose(0, 2, 1))
    c = np.arange(2 * NSA_TK + WIN)[None, :]
    b_tok = _rel_bucket_np(np.arange(NSA_TQ)[:, None] + WIN - (c - NSA_TK)).astype(np.int32)
    heads = rel_table.shape[1]
    return pl.pallas_call(
        _bias_tables_kernel,
        grid=(heads,),
        in_specs=[pl.BlockSpec(memory_space=pltpu.SMEM),
                  pl.BlockSpec(b_cmp.shape, lambda h: (0, 0, 0)),
                  pl.BlockSpec(b_tok.shape, lambda h: (0, 0))],
        out_specs=[pl.BlockSpec((n_res, 1) + b_cmp.shape[1:], lambda h: (0, h, 0, 0)),
                   pl.BlockSpec((1,) + b_tok.shape, lambda h: (h, 0, 0))],
        out_shape=[jax.ShapeDtypeStruct((n_res, heads) + b_cmp.shape[1:], F32),
                   jax.ShapeDtypeStruct((heads,) + b_tok.shape, F32)],
        compiler_params=_cparams(("parallel",)),
        name="bias_tables",
    )(rel_table.astype(F32), jnp.asarray(b_cmp), jnp.asarray(b_tok))


def kernel(x, mem, g_attn_norm, w_in, g_cq, w_uq, g_ckv, w_ukv, g_q_mla, g_k_mla, cmp_k_pos, cmp_k_w1, cmp_k_w2, cmp_v_pos, cmp_v_w1, cmp_v_w2, g_q_nsa, g_k_nsa, rel_table, g_out_mla, g_out_nsa, w_out, g_mem_norm, g_mem_src, w_mq, w_mkv, g_mq, g_mk, w_mo, g_moe_norm, router_w, router_b, w_gate_up, b_gate_up, w_down, b_down):
    batch, seq, d = x.shape
    n = batch * seq
    depth = w_in.shape[0]
    assert seq % 512 == 0 and seq // SLC_LEN <= LANES and d == ROW_TILE * LANES
    row = lambda v: v.reshape(1, -1).astype(F32)
    tile2 = lambda v: jnp.concatenate([v, v]).reshape(1, -1).astype(F32)

    x2d = x.reshape(n, d)
    for l in range(depth):
        wi = w_in[l]
        w_in_r = jnp.concatenate(
            [wi[:, 0:384], wi[:, 416:928], wi[:, 928:1696], wi[:, 384:416], wi[:, 1696:1720],
             jnp.zeros((d, IN_COLS_PAD - 1720), wi.dtype)], axis=1).astype(BF16)
        dq = MLA_NOPE + MLA_ROPE
        half = MLA_ROPE // 2
        x1, x2 = slice(MLA_NOPE, MLA_NOPE + half), slice(MLA_NOPE + half, dq)
        head_pad = lambda w, width: jnp.pad(w, ((0, 0), (0, 0), (0, LANES - width))).reshape(
            w.shape[0], MLA_HEADS * LANES).astype(BF16)
        wuq3 = w_uq[l].reshape(MLA_Q_RANK, MLA_HEADS, dq)
        wuq_r = head_pad(wuq3, dq)
        wuq_s = head_pad(jnp.concatenate([jnp.zeros_like(wuq3[:, :, :MLA_NOPE]), wuq3[:, :, x2], wuq3[:, :, x1]], 2), dq)
        wukv = w_ukv[l].reshape(MLA_KV_RANK, MLA_HEADS, MLA_NOPE + MLA_V)
        wuk_r = head_pad(wukv[:, :, :MLA_NOPE], MLA_NOPE)
        wuv_r = head_pad(wukv[:, :, MLA_NOPE:], MLA_V)

        inv = ROPE_THETA ** (-jnp.arange(half, dtype=F32) / half)
        ang = jnp.arange(seq, dtype=F32)[:, None] * inv
        cos, sin = jnp.cos(ang), jnp.sin(ang)
        lane_pad = jnp.zeros((seq, LANES - dq), F32)
        rot_c = jnp.concatenate([jnp.ones((seq, MLA_NOPE), F32), cos, cos, lane_pad], 1)
        rot_s = jnp.concatenate([jnp.zeros((seq, MLA_NOPE), F32), -sin, sin, lane_pad], 1)

        def rope_tables(g, scale):
            g_pad = jnp.pad(g, (0, LANES - dq))
            g_swp = jnp.pad(jnp.concatenate([g[:MLA_NOPE], g[x2], g[x1]]), (0, LANES - dq))
            return rot_c * (g_pad * scale), rot_s * (g_swp * scale)

        cq0, cq1 = rope_tables(g_q_mla[l], dq ** -0.5 * LOG2E)
        ck0, ck1 = rope_tables(g_k_mla[l], 1.0)

        mla_in, qn, kv6, misc = _in_proj(x2d, row(g_attn_norm[l]), w_in_r)
        q_m, k_m, v_m = _mla_prep(mla_in, misc, row(g_cq[l]), wuq_r, wuq_s, row(g_ckv[l]), wuk_r, wuv_r,
                                  cq0, cq1, ck0, ck1, seq)
        y_mla = _mla_attn(q_m, k_m, v_m, batch, seq)

        nc = seq // CMP_STRIDE
        half_len = CMP_LEN // 2
        eye_g = jnp.eye(NSA_KV_GROUPS, dtype=F32)

        def cmp_weights(pos, w1, w2):
            out = []
            for part in range(2):
                sl = slice(part * half_len, (part + 1) * half_len)
                out.append(jnp.broadcast_to(pos[sl][:, None, :], (half_len, NSA_KV_GROUPS, NSA_HD))
                           .reshape(1, -1))
            for part in range(2):
                sl = slice(part * half_len, (part + 1) * half_len)
                wexp = jnp.einsum('ldf,gh->lgdhf', w1[sl], eye_g)
                out.append(wexp.reshape(half_len * NSA_KV_GROUPS * NSA_HD, NSA_KV_GROUPS * CMP_HIDDEN).astype(BF16))
            out.append(w2.astype(BF16))
            return out

        chunk_w = CMP_STRIDE * NSA_KV_GROUPS * NSA_HD
        kcmp, vcmp = _nsa_cmp(kv6[0].reshape(batch, nc, chunk_w), kv6[1].reshape(batch, nc, chunk_w),
                              cmp_weights(cmp_k_pos[l], cmp_k_w1[l], cmp_k_w2[l]),
                              cmp_weights(cmp_v_pos[l], cmp_v_w1[l], cmp_v_w2[l]), row(g_k_nsa[l]))
        q_n, ks, vs, kw, vw, gates_n = _nsa_prep(qn, kv6, misc, tile2(g_q_nsa[l]), tile2(g_k_nsa[l]), seq)
        by_batch = lambda t: t.reshape(NSA_KV_GROUPS, batch, seq, t.shape[-1])
        front_pad = lambda t: jnp.pad(by_batch(t), ((0, 0), (0, 0), (WIN, 0), (0, 0)))
        tc, tn = _bias_tables(rel_table)
        n_idx = np.arange(nc)[:, None]
        j_idx = np.arange(LANES)[None, :]
        ovl = ((CMP_STRIDE * n_idx < SLC_LEN * j_idx + SLC_LEN)
               & (CMP_STRIDE * n_idx + CMP_LEN - 1 >= SLC_LEN * j_idx)
               & (n_idx < nc - 1) & (j_idx < seq // SLC_LEN)).astype(np.float32)
        o_cmp_t, sel = _nsa_select(q_n, kcmp, jnp.swapaxes(vcmp, 2, 3), tc, jnp.asarray(ovl.T, dtype=BF16),
                                   batch, seq)
        y_nsa = _nsa_attn(q_n, by_batch(ks), by_batch(vs), front_pad(kw), front_pad(vw), sel,
                          tn, gates_n, o_cmp_t, batch, seq)

        k_mem, v_mem = _mem_kv(mem, row(g_mem_src[l]), w_mkv[l].reshape(d, 2 * MEM_HEADS * MEM_HD).astype(BF16),
                               tile2(g_mk[l]))
        rw = jnp.pad(router_w[l], ((0, 0), (0, LANES - N_EXPERTS)))
        rw_hi = rw.astype(BF16)
        rw_lo = (rw - rw_hi.astype(F32)).astype(BF16)
        rb =jnp.pad(router_b[l], (0, LANES - N_EXPERTS), constant_values=NEG_INF).reshape(1, LANES)
        wo = w_out[l].astype(BF16)
        n_mla = MLA_HEADS * MLA_V
        x2, h2, topi, gates_e, counts = _post_attn(
            x2d, y_mla, y_nsa, row(g_out_mla[l]), row(g_out_nsa[l]), wo[:n_mla], wo[n_mla:],
            row(g_mem_norm[l]), w_mq[l].astype(BF16), tile2(g_mq[l]), k_mem, v_mem, w_mo[l].astype(BF16),
            row(g_moe_norm[l]), rw_hi, rw_lo, rb, batch, seq)

        cnt = counts[0].astype(jnp.int32)
        padded = (cnt + MOE_BLK - 1) // MOE_BLK * MOE_BLK
        pad_end = jnp.cumsum(padded)
        pad_start = (pad_end - padded).astype(F32).reshape(1, LANES)
        p_rows = (n * TOP_K // MOE_BLK + N_EXPERTS) * MOE_BLK
        n_blk = p_rows // MOE_BLK
        blk_first_row = jnp.arange(n_blk, dtype=jnp.int32) * MOE_BLK
        blk_e = jnp.minimum(jnp.sum(pad_end[None, :N_EXPERTS] <= blk_first_row[:, None], axis=1),
                            N_EXPERTS - 1).astype(jnp.int32)
        n_used = (pad_end[N_EXPERTS - 1] // MOE_BLK).astype(jnp.int32).reshape(1)
        tm_pos = 512
        tri = (np.arange(tm_pos)[None, :] < np.arange(tm_pos)[:, None]).astype(np.float32)
        pos = _moe_pos(topi, pad_start, jnp.asarray(tri, dtype=BF16))
        pos_flat = pos[:, :TOP_K].reshape(n * TOP_K)
        xs = _moe_scatter(pos_flat, h2, jnp.zeros((p_rows * ROW_TILE, LANES), F32))
        src = np.arange(GU_GROUP)
        perm = (np.arange(GU_GROUP)[:, None] == np.where(src < LANES, 2 * src, 2 * (src - LANES) + 1)[None, :])
        bgu = b_gate_up[l]
        ys = _moe_ffn(blk_e, n_used, xs, w_gate_up[l], jnp.asarray(perm.astype(np.float32), dtype=BF16),
                      bgu[:, None, 0::2], bgu[:, None, 1::2], w_down[l], b_down[l][:, None, :])
        x2d = _moe_combine(pos_flat, gates_e, x2, ys)
    return x2d.reshape(batch, seq, d)
```
